```python
import math
import jax, jax.numpy as jnp
from jax import lax
import numpy as np

D_MODEL = 1024
BATCH = 8
SEQ = 2048
DEPTH = 2
DEC_BATCH = 128
DEC_SEQ = 1
PAST_LEN = 16384
PAGE_SIZE = 128

H_A = 4
DK_A = 128
DV_A = 128
H_B = 4
DK_B = 128
DV_B = 128
CONV_W = 4
CHUNK_A = 64
CHUNK_B = 16
D_FF = -(-8 * D_MODEL // (3 * 256)) * 256
QK_A = H_A * DK_A
V_A = H_A * DV_A
K_B = H_B * DK_B
V_B = H_B * DV_B
CONV_DIM = 2 * QK_A + V_A
MIX_WIDTH = V_A + V_B
IN_SPLITS = (QK_A, QK_A, V_A, V_A, H_A, H_A, K_B, K_B, V_B, V_B)
IN_DIM = sum(IN_SPLITS)
EPS = 1e-6

kernel_name = 'hybrid_gdn_hgrn2_adaln_decode_step'


def _rmsnorm(x, w):
    x32 = x.astype(jnp.float32)
    y = x32 * lax.rsqrt(jnp.mean(x32 * x32, axis=-1, keepdims=True) + EPS)
    return (y * w.astype(jnp.float32)).astype(x.dtype)


def _l2norm(x):
    return x * lax.rsqrt(jnp.sum(x * x, axis=-1, keepdims=True) + EPS)


def _pad_time(t, pad):
    if pad == 0:
        return t
    widths = [(0, 0)] * t.ndim
    widths[1] = (0, pad)
    return jnp.pad(t, widths)


def _to_chunks(t, C):
    Bn, Lp, H = t.shape[:3]
    t = t.reshape((Bn, Lp // C, C, H) + t.shape[3:])
    perm = (1, 0, 3, 2) + tuple(range(4, t.ndim))
    return jnp.transpose(t, perm)


def _from_chunks(o, L):
    n, Bn, H, C, V = o.shape
    return jnp.transpose(o, (1, 0, 3, 2, 4)).reshape(Bn, n * C, H, V)[:, :L]


def _causal_conv(u, buf, w):
    L = u.shape[1]
    full = jnp.concatenate([buf.astype(u.dtype), u], axis=1)
    out = full[:, 0:L] * w[0]
    for j in range(1, CONV_W):
        out = out + full[:, j:j + L] * w[j]
    return jax.nn.silu(out), full[:, full.shape[1] - (CONV_W - 1):]


def _gated_delta_chunked(q, k, v, g, beta, S0):
    Bn, L, H, DK = q.shape
    C = min(CHUNK_A, L)
    n = -(-L // C)
    pad = n * C - L
    q, k, v, g, beta = [_pad_time(t, pad) for t in (q, k, v, g, beta)]
    q = _to_chunks(q * DK ** -0.5, C)
    k = _to_chunks(k, C)
    v = _to_chunks(v, C)
    g = _to_chunks(g, C)
    beta = _to_chunks(beta, C)
    G = jnp.cumsum(g, axis=-1)
    eye = jnp.eye(C, dtype=jnp.float32)
    tril = jnp.tril(jnp.ones((C, C), dtype=bool))
    dec = jnp.exp(jnp.where(tril, G[..., :, None] - G[..., None, :], -jnp.inf))
    kb = k * beta[..., None]
    A = jnp.einsum('nbhik,nbhjk->nbhij', kb, k) * dec * (1.0 - eye)
    T = lax.linalg.triangular_solve(eye + A, jnp.broadcast_to(eye, A.shape), left_side=True, lower=True)
    u = jnp.einsum('nbhij,nbhjv->nbhiv', T, v * beta[..., None])
    w = jnp.einsum('nbhij,nbhjk->nbhik', T, kb * jnp.exp(G)[..., None])
    qk = jnp.einsum('nbhik,nbhjk->nbhij', q, k) * dec
    qd = q * jnp.exp(G)[..., None]
    kd = k * jnp.exp(G[..., -1:] - G)[..., None]
    gl = jnp.exp(G[..., -1])

    def step(S, xs):
        u_c, w_c, qk_c, qd_c, kd_c, gl_c = xs
        v_new = u_c - jnp.einsum('bhck,bhkv->bhcv', w_c, S)
        o = jnp.einsum('bhck,bhkv->bhcv', qd_c, S) + jnp.einsum('bhij,bhjv->bhiv', qk_c, v_new)
        S = S * gl_c[..., None, None] + jnp.einsum('bhck,bhcv->bhkv', kd_c, v_new)
        return S, o

    S, o = lax.scan(step, S0, (u, w, qk, qd, kd, gl))
    return _from_chunks(o, L), S


def _hgrn2_chunked(q, k, v, logf, S0):
    Bn, L, H, DK = q.shape
    C = min(CHUNK_B, L)
    n = -(-L // C)
    pad = n * C - L
    q, k, v, logf = [_to_chunks(_pad_time(t, pad), C) for t in (q, k, v, logf)]
    G = jnp.cumsum(logf, axis=-2)
    tril = jnp.tril(jnp.ones((C, C), dtype=bool))

    def step(S, xs):
        qc, kc, vc, Gc = xs
        diff = Gc[:, :, :, None, :] - Gc[:, :, None, :, :]
        dec = jnp.exp(jnp.where(tril[:, :, None], diff, -jnp.inf))
        A = jnp.sum(qc[:, :, :, None, :] * kc[:, :, None, :, :] * dec, axis=-1)
        o = jnp.einsum('bhik,bhkv->bhiv', qc * jnp.exp(Gc), S) + jnp.einsum('bhij,bhjv->bhiv', A, vc)
        Gl = Gc[:, :, -1:, :]
        S = S * jnp.exp(Gl[:, :, 0, :])[..., None] + jnp.einsum('bhck,bhcv->bhkv', kc * jnp.exp(Gl - Gc), vc)
        return S, o

    S, o = lax.scan(step, S0, (q, k, v, G))
    return _from_chunks(o, L), S


def _layer(x, c, S_a, buf, S_b, lb, w_ada, b_ada, norm1, w_in, conv_w, a_log, dt_bias,
           norm_a, norm_b, w_out, norm2, w_gate, w_up, w_down):
    f32 = jnp.float32
    Bn, L, _ = x.shape
    mods = (jax.nn.silu(c) @ w_ada + b_ada)[:, None, :]
    shift1, scale1, gate1, shift2, scale2, gate2 = jnp.split(mods, 6, axis=-1)
    h = _rmsnorm(x, norm1) * (1 + scale1) + shift1
    proj = h @ w_in
    cuts = np.cumsum(IN_SPLITS)[:-1].tolist()
    qa, ka, va, za, aa, ba, qb, fb, ib, ob = jnp.split(proj, cuts, axis=-1)
    qkv, new_buf = _causal_conv(jnp.concatenate([qa, ka, va], axis=-1), buf, conv_w)
    qkv = qkv.astype(f32)
    qa_, ka_, va_ = jnp.split(qkv, [QK_A, 2 * QK_A], axis=-1)
    qh = _l2norm(qa_.reshape(Bn, L, H_A, DK_A))
    kh = _l2norm(ka_.reshape(Bn, L, H_A, DK_A))
    vh = va_.reshape(Bn, L, H_A, DV_A)
    beta = jax.nn.sigmoid(ba.astype(f32))
    g = -jnp.exp(a_log.astype(f32)) * jax.nn.softplus(aa.astype(f32) + dt_bias.astype(f32))
    o_a, S_a_new = _gated_delta_chunked(qh, kh, vh, g, beta, S_a.astype(f32))
    o_a = _rmsnorm(o_a, norm_a) * jax.nn.silu(za.astype(f32).reshape(Bn, L, H_A, DV_A))
    lbh = lb.astype(f32).reshape(H_B, DK_B)
    f = lbh + (1.0 - lbh) * jax.nn.sigmoid(fb.astype(f32).reshape(Bn, L, H_B, DK_B))
    qbh = jax.nn.silu(qb.astype(f32).reshape(Bn, L, H_B, DK_B))
    ibh = ib.astype(f32).reshape(Bn, L, H_B, DV_B)
    o_b, S_b_new = _hgrn2_chunked(qbh, 1.0 - f, ibh, jnp.log(f), S_b.astype(f32))
    o_b = _rmsnorm(o_b, norm_b) * jax.nn.sigmoid(ob.astype(f32).reshape(Bn, L, H_B, DV_B))
    mix = jnp.concatenate([o_a.reshape(Bn, L, V_A), o_b.reshape(Bn, L, V_B)], axis=-1).astype(x.dtype) @ w_out
    x = x + gate1 * mix
    h = _rmsnorm(x, norm2) * (1 + scale2) + shift2
    x = x + gate2 * ((jax.nn.silu(h @ w_gate) * (h @ w_up)) @ w_down)
    return x, S_a_new, new_buf, S_b_new


def _trunk(x, c, S_a, buf, S_b, lb_all, stacked, norm_f):
    new_a, new_buf, new_b = [], [], []
    for l in range(DEPTH):
        x, sa, bf, sb = _layer(x, c, S_a[l], buf[l], S_b[l], lb_all[l], *[p[l] for p in stacked])
        new_a.append(sa)
        new_buf.append(bf)
        new_b.append(sb)
    return _rmsnorm(x, norm_f), jnp.stack(new_a), jnp.stack(new_buf), jnp.stack(new_b)


def setup_inputs(seed: int = 0) -> dict:
    key = jax.random.key(seed)
    ks = jax.random.split(key, 24)
    f32 = jnp.float32

    def nrm(k, shape, scale):
        return jax.random.normal(k, shape, f32) * scale

    dt = jnp.exp(jax.random.uniform(ks[13], (DEPTH, H_A), f32, math.log(1e-3), math.log(1e-1)))
    return {
        'x_prompt': nrm(ks[0], (BATCH, SEQ, D_MODEL), 1.0),
        'x_sample': nrm(ks[1], (DEC_BATCH, DEC_SEQ, D_MODEL), 1.0),
        'c_prompt': nrm(ks[2], (BATCH, D_MODEL), 1.0),
        'c_sample': nrm(ks[3], (DEC_BATCH, D_MODEL), 1.0),
        'state_delta': nrm(ks[4], (DEPTH, DEC_BATCH, H_A, DK_A, DV_A), 0.3),
        'state_conv': nrm(ks[5], (DEPTH, DEC_BATCH, CONV_W - 1, CONV_DIM), 1.0),
        'state_hgrn': nrm(ks[6], (DEPTH, DEC_BATCH, H_B, DK_B, DV_B), 0.5),
        'w_ada': nrm(ks[7], (DEPTH, D_MODEL, 6 * D_MODEL), 0.5 * D_MODEL ** -0.5),
        'b_ada': nrm(ks[8], (DEPTH, 6 * D_MODEL), 0.01),
        'norm1': 1.0 + nrm(ks[9], (DEPTH, D_MODEL), 0.01),
        'w_in': nrm(ks[10], (DEPTH, D_MODEL, IN_DIM), D_MODEL ** -0.5),
        'conv_w': nrm(ks[11], (DEPTH, CONV_W, CONV_DIM), CONV_W ** -0.5),
        'a_log': jnp.log(jax.random.uniform(ks[12], (DEPTH, H_A), f32, 1.0, 16.0)),
        'dt_bias': dt + jnp.log(-jnp.expm1(-dt)),
        'norm_a': 1.0 + nrm(ks[14], (DEPTH, DV_A), 0.01),
        'norm_b': 1.0 + nrm(ks[15], (DEPTH, DV_B), 0.01),
        'lb_logits': nrm(ks[16], (DEPTH, K_B), 1.0),
        'w_out': nrm(ks[17], (DEPTH, MIX_WIDTH, D_MODEL), MIX_WIDTH ** -0.5),
        'norm2': 1.0 + nrm(ks[18], (DEPTH, D_MODEL), 0.01),
        'w_gate': nrm(ks[19], (DEPTH, D_MODEL, D_FF), D_MODEL ** -0.5),
        'w_up': nrm(ks[20], (DEPTH, D_MODEL, D_FF), D_MODEL ** -0.5),
        'w_down': nrm(ks[21], (DEPTH, D_FF, D_MODEL), D_FF ** -0.5),
        'norm_f': 1.0 + nrm(ks[22], (D_MODEL,), 0.01),
    }


def reference(x_prompt, x_sample, c_prompt, c_sample, state_delta, state_conv, state_hgrn,
              w_ada, b_ada, norm1, w_in, conv_w, a_log, dt_bias, norm_a, norm_b, lb_logits,
              w_out, norm2, w_gate, w_up, w_down, norm_f):
    f32 = jnp.float32
    sm = jax.nn.softmax(lb_logits.astype(f32), axis=0)
    lb_all = jnp.cumsum(sm, axis=0) - sm[0:1]
    stacked = (w_ada, b_ada, norm1, w_in, conv_w, a_log, dt_bias, norm_a, norm_b,
               w_out, norm2, w_gate, w_up, w_down)
    bp = x_prompt.shape[0]
    sa0 = jnp.zeros((DEPTH, bp, H_A, DK_A, DV_A), f32)
    buf0 = jnp.zeros((DEPTH, bp, CONV_W - 1, CONV_DIM), x_prompt.dtype)
    sb0 = jnp.zeros((DEPTH, bp, H_B, DK_B, DV_B), f32)
    y_prompt, delta_p, conv_p, hgrn_p = _trunk(x_prompt, c_prompt, sa0, buf0, sb0, lb_all, stacked, norm_f)
    y_sample, delta_s, conv_s, hgrn_s = _trunk(x_sample, c_sample, state_delta, state_conv, state_hgrn,
                                               lb_all, stacked, norm_f)
    return (y_prompt, y_sample, delta_p, conv_p, hgrn_p, delta_s, conv_s, hgrn_s)
```

```python
import functools

import jax
import jax.numpy as jnp
from jax import lax
from jax.experimental import pallas as pl
from jax.experimental.pallas import tpu as pltpu

F32 = jnp.float32
BF16 = jnp.bfloat16

D_MODEL = 1024
DEPTH = 2
N_HEADS = 4
HEAD_DIM = 128
MIX_HALF = N_HEADS * HEAD_DIM
CONV_W = 4
CONV_DIM = 3 * MIX_HALF
D_FF = 2816
EPS = 1e-6
GDN_CHUNK = 64
GDN_SUB = 16
HGRN_CHUNK = 16

COL_QKV = 0
COL_Z = CONV_DIM
COL_B = COL_Z + MIX_HALF
COL_AB = COL_B + 4 * MIX_HALF
IN_COLS = COL_AB + 128

MOD_SHIFT1, MOD_SCALE1, MOD_GATE1, MOD_SHIFT2, MOD_SCALE2, MOD_GATE2 = range(6)

VMEM_LIMIT = 56 * 1024 * 1024


def _sigmoid(x):
    return 1.0 / (1.0 + jnp.exp(-x))


def _silu(x):
    return x * _sigmoid(x)


def _softplus(x):
    return jnp.maximum(x, 0.0) + jnp.log1p(jnp.exp(-jnp.abs(x)))


def _mm(a, b):
    return jnp.dot(a.astype(BF16), b.astype(BF16), preferred_element_type=F32)


def _mm_nt(a, b):
    return lax.dot_general(a.astype(BF16), b.astype(BF16), (((1,), (1,)), ((), ())),
                           preferred_element_type=F32)


def _mm_tn(a, b):
    return lax.dot_general(a.astype(BF16), b.astype(BF16), (((0,), (0,)), ((), ())),
                           preferred_element_type=F32)


def _rms(x, w):
    return x * lax.rsqrt(jnp.mean(x * x, axis=-1, keepdims=True) + EPS) * w


def _chunk_cumsum(x, chunk):
    rows = lax.broadcasted_iota(jnp.int32, x.shape, 0) & (chunk - 1)
    s = 1
    while s < chunk:
        x = x + jnp.where(rows >= s, pltpu.roll(x, s, 0), 0.0)
        s *= 2
    return x


def _ada_kernel(c_ref, w_ref, b_ref, o_ref):
    a = _silu(c_ref[...])
    o_ref[...] = _mm(a, w_ref[...]) + b_ref[...]


def _ada_call(c_all, w_ada, b_ada):
    rows = c_all.shape[0]
    tn = 1024
    return pl.pallas_call(
        _ada_kernel,
        grid=(DEPTH, 6 * D_MODEL // tn),
        in_specs=[
            pl.BlockSpec((rows, D_MODEL), lambda l, j: (0, 0)),
            pl.BlockSpec((None, D_MODEL, tn), lambda l, j: (l, 0, j)),
            pl.BlockSpec((None, 1, tn), lambda l, j: (l, 0, j)),
        ],
        out_specs=pl.BlockSpec((None, rows, tn), lambda l, j: (l, 0, j)),
        out_shape=jax.ShapeDtypeStruct((DEPTH, rows, 6 * D_MODEL), F32),
        compiler_params=pltpu.CompilerParams(
            dimension_semantics=("arbitrary", "arbitrary"), vmem_limit_bytes=VMEM_LIMIT),
        name="ada_mod",
    )(c_all, w_ada, b_ada.reshape(DEPTH, 1, 6 * D_MODEL))


def _mod_spec(per_token, tm, tiles_per_seq, col):
    if per_token:
        return pl.BlockSpec((tm, D_MODEL), lambda i: (i, col))
    return pl.BlockSpec((None, 1, D_MODEL), lambda i: (i // tiles_per_seq, 0, col))


def _in_proj_kernel(x_ref, scale_ref, shift_ref, nw_ref, w_ref, o_ref):
    h = _rms(x_ref[...], nw_ref[...]) * (1.0 + scale_ref[...]) + shift_ref[...]
    o_ref[...] = jnp.dot(h.astype(BF16), w_ref[...], preferred_element_type=F32)


def _in_proj_call(x, mods, norm_w, w_in, *, per_token, seq_len):
    rows = x.shape[0]
    tm = rows if per_token else 512
    tiles_per_seq = 1 if per_token else seq_len // tm
    return pl.pallas_call(
        _in_proj_kernel,
        grid=(rows // tm,),
        in_specs=[
            pl.BlockSpec((tm, D_MODEL), lambda i: (i, 0)),
            _mod_spec(per_token, tm, tiles_per_seq, MOD_SCALE1),
            _mod_spec(per_token, tm, tiles_per_seq, MOD_SHIFT1),
            pl.BlockSpec((1, D_MODEL), lambda i: (0, 0)),
            pl.BlockSpec((D_MODEL, IN_COLS), lambda i: (0, 0), pipeline_mode=pl.Buffered(1)),
        ],
        out_specs=pl.BlockSpec((tm, IN_COLS), lambda i: (i, 0)),
        out_shape=jax.ShapeDtypeStruct((rows, IN_COLS), F32),
        compiler_params=pltpu.CompilerParams(
            dimension_semantics=("arbitrary",), vmem_limit_bytes=VMEM_LIMIT),
        name="in_proj",
    )(x, mods, mods, norm_w.reshape(1, D_MODEL), w_in)


def _unit_lower_inverse(a, sub_mask, eye):
    d = jnp.where(sub_mask, a, 0.0)
    n = a - d
    d2 = _mm(d, d)
    d4 = _mm(d2, d2)
    d8 = _mm(d4, d4)
    p = eye - d + d2 - _mm(d, d2)
    p = p + _mm(p, d4)
    dinv = p + _mm(p, d8)
    e = _mm(dinv, n)
    e2 = _mm(e, e)
    f = e - e2 + _mm(e, e2)
    return dinv - _mm(f, dinv)


def _gdn_kernel(qkv_ref, z_ref, ab_ref, buf_ref, s0_ref, cw_ref, alog_ref, dtb_ref, na_ref,
                mix_ref, sfin_ref, cfin_ref, xpad, s_scr, *, tile):
    t = pl.program_id(1)

    @pl.when(t == 0)
    def _():
        xpad[0:8, :] = buf_ref[...]
        s_scr[...] = s0_ref[...]

    xpad[8:8 + tile, :] = qkv_ref[...]
    cw = cw_ref[...]
    acc = xpad[5:5 + tile, :] * cw[0:1, :]
    for j in range(1, CONV_W):
        acc = acc + xpad[5 + j:5 + j + tile, :] * cw[j:j + 1, :]
    conv = _silu(acc)
    tail = xpad[tile:tile + 8, :]
    xpad[0:8, :] = tail

    ab = ab_ref[...]
    g_all = -jnp.exp(alog_ref[...]) * _softplus(ab + dtb_ref[...])
    sig_all = _sigmoid(ab)
    g_cum = _chunk_cumsum(g_all, GDN_CHUNK)
    g_cum_t = g_cum.T

    ri = lax.broadcasted_iota(jnp.int32, (tile, tile), 0)
    ci = lax.broadcasted_iota(jnp.int32, (tile, tile), 1)
    block_dist = jnp.bitwise_xor(ri, ci)
    same_chunk = block_dist < GDN_CHUNK
    m_incl = same_chunk & (ri >= ci)
    m_strict = same_chunk & (ri > ci)
    sub_mask = block_dist < GDN_SUB
    eye = jnp.where(ri == ci, 1.0, 0.0).astype(F32)
    n_chunks = tile // GDN_CHUNK

    for h in range(N_HEADS):
        cs = slice(h * HEAD_DIM, (h + 1) * HEAD_DIM)
        qh = conv[:, cs]
        kh = conv[:, MIX_HALF + h * HEAD_DIM:MIX_HALF + (h + 1) * HEAD_DIM]
        vh = conv[:, 2 * MIX_HALF + h * HEAD_DIM:2 * MIX_HALF + (h + 1) * HEAD_DIM]
        qn = qh * lax.rsqrt(jnp.sum(qh * qh, axis=-1, keepdims=True) + EPS) * (HEAD_DIM ** -0.5)
        kn = kh * lax.rsqrt(jnp.sum(kh * kh, axis=-1, keepdims=True) + EPS)
        gc = g_cum[:, h:h + 1]
        gr = g_cum_t[h:h + 1, :]
        beta = sig_all[:, N_HEADS + h:N_HEADS + h + 1]

        dec = jnp.exp(jnp.where(m_incl, gc - gr, -jnp.inf))
        kb = kn * beta
        a = _mm_nt(kb, kn) * jnp.where(m_strict, dec, 0.0)
        qk = _mm_nt(qn, kn) * dec
        tinv = _unit_lower_inverse(a, sub_mask, eye)

        eg = jnp.exp(gc)
        uw = _mm(tinv, jnp.concatenate([vh * beta, kb * eg], axis=1))
        u = uw[:, :HEAD_DIM]
        w = uw[:, HEAD_DIM:]
        qd = qn * eg

        s = s_scr[h]
        v_new_parts = []
        o_inter_parts = []
        for c in range(n_chunks):
            rs = slice(c * GDN_CHUNK, (c + 1) * GDN_CHUNK)
            r = _mm(jnp.concatenate([w[rs], qd[rs]], axis=0), s)
            v_new = u[rs] - r[:GDN_CHUNK]
            o_inter_parts.append(r[GDN_CHUNK:])
            g_last = gc[(c + 1) * GDN_CHUNK - 1:(c + 1) * GDN_CHUNK, :]
            kd = kn[rs] * jnp.exp(g_last - gc[rs])
            s = s * jnp.exp(g_last) + _mm_tn(kd, v_new)
            v_new_parts.append(v_new)
        s_scr[h] = s

        o = jnp.concatenate(o_inter_parts, axis=0) + _mm(qk, jnp.concatenate(v_new_parts, axis=0))
        zh = z_ref[:, cs]
        mix_ref[:, cs] = (_rms(o, na_ref[...]) * _silu(zh)).astype(mix_ref.dtype)

    @pl.when(t == pl.num_programs(1) - 1)
    def _():
        sfin_ref[...] = s_scr[...]
        cfin_ref[...] = tail[8 - (CONV_W - 1):, :]


def _gdn_call(proj, buf8, s0, conv_w, a_log_row, dt_bias_row, norm_a, *, batch, seq_len):
    tile = 256
    nt = seq_len // tile
    kern = functools.partial(_gdn_kernel, tile=tile)
    return pl.pallas_call(
        kern,
        grid=(batch, nt),
        in_specs=[
            pl.BlockSpec((tile, CONV_DIM), lambda b, t: (b * nt + t, COL_QKV // CONV_DIM)),
            pl.BlockSpec((tile, MIX_HALF), lambda b, t: (b * nt + t, COL_Z // MIX_HALF)),
            pl.BlockSpec((tile, 128), lambda b, t: (b * nt + t, COL_AB // 128)),
            pl.BlockSpec((None, 8, CONV_DIM), lambda b, t: (b, 0, 0)),
            pl.BlockSpec((None, N_HEADS, HEAD_DIM, HEAD_DIM), lambda b, t: (b, 0, 0, 0)),
            pl.BlockSpec((CONV_W, CONV_DIM), lambda b, t: (0, 0)),
            pl.BlockSpec((1, 128), lambda b, t: (0, 0)),
            pl.BlockSpec((1, 128), lambda b, t: (0, 0)),
            pl.BlockSpec((1, HEAD_DIM), lambda b, t: (0, 0)),
        ],
        out_specs=[
            pl.BlockSpec((tile, MIX_HALF), lambda b, t: (b * nt + t, 0)),
            pl.BlockSpec((None, N_HEADS, HEAD_DIM, HEAD_DIM), lambda b, t: (b, 0, 0, 0)),
            pl.BlockSpec((None, CONV_W - 1, CONV_DIM), lambda b, t: (b, 0, 0)),
        ],
        out_shape=[
            jax.ShapeDtypeStruct((batch * seq_len, MIX_HALF), BF16),
            jax.ShapeDtypeStruct((batch, N_HEADS, HEAD_DIM, HEAD_DIM), F32),
            jax.ShapeDtypeStruct((batch, CONV_W - 1, CONV_DIM), F32),
        ],
        scratch_shapes=[
            pltpu.VMEM((tile + 8, CONV_DIM), F32),
            pltpu.VMEM((N_HEADS, HEAD_DIM, HEAD_DIM), F32),
        ],
        compiler_params=pltpu.CompilerParams(
            dimension_semantics=("arbitrary", "arbitrary"), vmem_limit_bytes=VMEM_LIMIT),
        name="gdn_prompt",
    )(proj, proj, proj, buf8, s0, conv_w, a_log_row, dt_bias_row, norm_a.reshape(1, HEAD_DIM))


def _forget_lower_bound(lbl, layer):
    m = jnp.max(lbl, axis=0, keepdims=True)
    e = jnp.exp(lbl - m)
    sm = e / jnp.sum(e, axis=0, keepdims=True)
    cs = sm[0:1, :]
    for i in range(1, layer + 1):
        cs = cs + sm[i:i + 1, :]
    return cs - sm[0:1, :]


def _hgrn_kernel(p_ref, lbl_ref, nb_ref, s0_ref, mix_ref, sfin_ref,
                 q_scr, k_scr, g_scr, o_scr, st_scr, *, tile, layer):
    t = pl.program_id(1)

    @pl.when(t == 0)
    def _():
        for h in range(N_HEADS):
            st_scr[h] = s0_ref[h].T

    lb = _forget_lower_bound(lbl_ref[...], layer)
    f = lb + (1.0 - lb) * _sigmoid(p_ref[:, MIX_HALF:2 * MIX_HALF])
    q_scr[...] = _silu(p_ref[:, 0:MIX_HALF])
    k_scr[...] = 1.0 - f
    g_scr[...] = _chunk_cumsum(jnp.log(f), HGRN_CHUNK)

    rows = lax.broadcasted_iota(jnp.int32, (HGRN_CHUNK, HEAD_DIM), 0)

    def chunk_body(c, carry):
        r0 = pl.multiple_of(c * HGRN_CHUNK, HGRN_CHUNK)
        rs = pl.ds(r0, HGRN_CHUNK)
        for h in range(N_HEADS):
            cs = slice(h * HEAD_DIM, (h + 1) * HEAD_DIM)
            qc = q_scr[rs, cs]
            kc = k_scr[rs, cs]
            gc = g_scr[rs, cs]
            vc = p_ref[rs, 2 * MIX_HALF + h * HEAD_DIM:2 * MIX_HALF + (h + 1) * HEAD_DIM]
            o = jnp.zeros((HGRN_CHUNK, HEAD_DIM), F32)
            for j in range(HGRN_CHUNK):
                dec = jnp.exp(jnp.where(rows >= j, gc - gc[j:j + 1, :], -jnp.inf))
                a_j = jnp.sum(qc * kc[j:j + 1, :] * dec, axis=-1, keepdims=True)
                o = o + a_j * vc[j:j + 1, :]
            st = st_scr[h]
            o = o + _mm_nt(qc * jnp.exp(gc), st)
            g_last = gc[HGRN_CHUNK - 1:HGRN_CHUNK, :]
            st_scr[h] = st * jnp.exp(g_last) + _mm_tn(vc, kc * jnp.exp(g_last - gc))
            o_scr[rs, cs] = o
        return carry

    lax.fori_loop(0, tile // HGRN_CHUNK, chunk_body, 0)

    for h in range(N_HEADS):
        cs = slice(h * HEAD_DIM, (h + 1) * HEAD_DIM)
        gate = _sigmoid(p_ref[:, 3 * MIX_HALF + h * HEAD_DIM:3 * MIX_HALF + (h + 1) * HEAD_DIM])
        mix_ref[:, cs] = (_rms(o_scr[:, cs], nb_ref[...]) * gate).astype(mix_ref.dtype)

    @pl.when(t == pl.num_programs(1) - 1)
    def _():
        for h in range(N_HEADS):
            sfin_ref[h] = st_scr[h].T


def _hgrn_call(proj, lb_logits, norm_b, s0, *, layer, batch, seq_len):
    tile = 256
    nt = seq_len // tile
    kern = functools.partial(_hgrn_kernel, tile=tile, layer=layer)
    return pl.pallas_call(
        kern,
        grid=(batch, nt),
        in_specs=[
            pl.BlockSpec((tile, 4 * MIX_HALF), lambda b, t: (b * nt + t, COL_B // (4 * MIX_HALF))),
            pl.BlockSpec((DEPTH, MIX_HALF), lambda b, t: (0, 0)),
            pl.BlockSpec((1, HEAD_DIM), lambda b, t: (0, 0)),
            pl.BlockSpec((None, N_HEADS, HEAD_DIM, HEAD_DIM), lambda b, t: (b, 0, 0, 0)),
        ],
        out_specs=[
            pl.BlockSpec((tile, MIX_HALF), lambda b, t: (b * nt + t, 0)),
            pl.BlockSpec((None, N_HEADS, HEAD_DIM, HEAD_DIM), lambda b, t: (b, 0, 0, 0)),
        ],
        out_shape=[
            jax.ShapeDtypeStruct((batch * seq_len, MIX_HALF), BF16),
            jax.ShapeDtypeStruct((batch, N_HEADS, HEAD_DIM, HEAD_DIM), F32),
        ],
        scratch_shapes=[
            pltpu.VMEM((tile, MIX_HALF), F32),
            pltpu.VMEM((tile, MIX_HALF), F32),
            pltpu.VMEM((tile, MIX_HALF), F32),
            pltpu.VMEM((tile, MIX_HALF), F32),
            pltpu.VMEM((N_HEADS, HEAD_DIM, HEAD_DIM), F32),
        ],
        compiler_params=pltpu.CompilerParams(
            dimension_semantics=("arbitrary", "arbitrary"), vmem_limit_bytes=VMEM_LIMIT),
        name="hgrn_prompt",
    )(proj, lb_logits, norm_b.reshape(1, HEAD_DIM), s0)


def _decode_kernel(qkv_ref, z_ref, ab_ref, pb_ref, conv_ref, sd_ref, sh_ref,
                   cw_ref, alog_ref, dtb_ref, na_ref, nb_ref, lbl_ref,
                   mix_ref, convo_ref, sdo_ref, sho_ref, oa_scr, ob_scr, *, bt, layer):
    cw = cw_ref[...]
    u = qkv_ref[...]
    b0 = conv_ref[:, 0:CONV_DIM]
    b1 = conv_ref[:, CONV_DIM:2 * CONV_DIM]
    b2 = conv_ref[:, 2 * CONV_DIM:3 * CONV_DIM]
    acc = b0 * cw[0:1, :]
    acc = acc + b1 * cw[1:2, :]
    acc = acc + b2 * cw[2:3, :]
    acc = acc + u * cw[3:4, :]
    conv = _silu(acc)
    convo_ref[:, 0:CONV_DIM] = b1
    convo_ref[:, CONV_DIM:2 * CONV_DIM] = b2
    convo_ref[:, 2 * CONV_DIM:3 * CONV_DIM] = u

    ab = ab_ref[...]
    eg_all = jnp.exp(-jnp.exp(alog_ref[...]) * _softplus(ab + dtb_ref[...]))
    beta_all = _sigmoid(ab)

    lb = _forget_lower_bound(lbl_ref[...], layer)
    f = lb + (1.0 - lb) * _sigmoid(pb_ref[:, MIX_HALF:2 * MIX_HALF])
    qb = _silu(pb_ref[:, 0:MIX_HALF])

    k_rows, q_rows, v_rows = [], [], []
    for h in range(N_HEADS):
        cs = slice(h * HEAD_DIM, (h + 1) * HEAD_DIM)
        qh = conv[:, cs]
        kh = conv[:, MIX_HALF + h * HEAD_DIM:MIX_HALF + (h + 1) * HEAD_DIM]
        q_rows.append(qh * lax.rsqrt(jnp.sum(qh * qh, axis=-1, keepdims=True) + EPS)
                      * (HEAD_DIM ** -0.5))
        k_rows.append(kh * lax.rsqrt(jnp.sum(kh * kh, axis=-1, keepdims=True) + EPS))
        v_rows.append(conv[:, 2 * MIX_HALF + h * HEAD_DIM:2 * MIX_HALF + (h + 1) * HEAD_DIM])
    f_rows = [f[:, h * HEAD_DIM:(h + 1) * HEAD_DIM] for h in range(N_HEADS)]
    qb_rows = [qb[:, h * HEAD_DIM:(h + 1) * HEAD_DIM] for h in range(N_HEADS)]

    stacked = jnp.concatenate(k_rows + q_rows + f_rows + qb_rows, axis=0)
    cols = stacked.T

    def col(item, h, b):
        i = (item * N_HEADS + h) * bt + b
        return cols[:, i:i + 1]

    for b in range(bt):
        for h in range(N_HEADS):
            cs = slice(h * HEAD_DIM, (h + 1) * HEAD_DIM)
            s = sd_ref[b, h]
            kc = col(0, h, b)
            eg = eg_all[b:b + 1, h:h + 1]
            beta = beta_all[b:b + 1, N_HEADS + h:N_HEADS + h + 1]
            sk = jnp.sum(s * kc, axis=0, keepdims=True)
            v_new = beta * (v_rows[h][b:b + 1, :] - eg * sk)
            s_new = s * eg + kc * v_new
            sdo_ref[b, h] = s_new
            oa_scr[b:b + 1, cs] = jnp.sum(s_new * col(1, h, b), axis=0, keepdims=True)
            s = sh_ref[b, h]
            fc = col(2, h, b)
            vb = pb_ref[b:b + 1, 2 * MIX_HALF + h * HEAD_DIM:2 * MIX_HALF + (h + 1) * HEAD_DIM]
            s_new = s * fc + (1.0 - fc) * vb
            sho_ref[b, h] = s_new
            ob_scr[b:b + 1, cs] = jnp.sum(s_new * col(3, h, b), axis=0, keepdims=True)

    for h in range(N_HEADS):
        cs = slice(h * HEAD_DIM, (h + 1) * HEAD_DIM)
        mix_ref[:, cs] = (_rms(oa_scr[:, cs], na_ref[...]) * _silu(z_ref[:, cs])).astype(mix_ref.dtype)
        gate = _sigmoid(pb_ref[:, 3 * MIX_HALF + h * HEAD_DIM:3 * MIX_HALF + (h + 1) * HEAD_DIM])
        mix_ref[:, MIX_HALF + h * HEAD_DIM:MIX_HALF + (h + 1) * HEAD_DIM] = (
            _rms(ob_scr[:, cs], nb_ref[...]) * gate).astype(mix_ref.dtype)


def _decode_call(proj, conv_state, s_delta, s_hgrn, conv_w, a_log_row, dt_bias_row,
                 norm_a, norm_b, lb_logits, *, layer):
    batch = proj.shape[0]
    bt = 8
    kern = functools.partial(_decode_kernel, bt=bt, layer=layer)
    state_spec = pl.BlockSpec((bt, N_HEADS, HEAD_DIM, HEAD_DIM), lambda i: (i, 0, 0, 0))
    state_shape = jax.ShapeDtypeStruct((batch, N_HEADS, HEAD_DIM, HEAD_DIM), F32)
    return pl.pallas_call(
        kern,
        grid=(batch // bt,),
        in_specs=[
            pl.BlockSpec((bt, CONV_DIM), lambda i: (i, COL_QKV // CONV_DIM)),
            pl.BlockSpec((bt, MIX_HALF), lambda i: (i, COL_Z // MIX_HALF)),
            pl.BlockSpec((bt, 128), lambda i: (i, COL_AB // 128)),
            pl.BlockSpec((bt, 4 * MIX_HALF), lambda i: (i, COL_B // (4 * MIX_HALF))),
            pl.BlockSpec((bt, (CONV_W - 1) * CONV_DIM), lambda i: (i, 0)),
            state_spec,
            state_spec,
            pl.BlockSpec((CONV_W, CONV_DIM), lambda i: (0, 0)),
            pl.BlockSpec((1, 128), lambda i: (0, 0)),
            pl.BlockSpec((1, 128), lambda i: (0, 0)),
            pl.BlockSpec((1, HEAD_DIM), lambda i: (0, 0)),
            pl.BlockSpec((1, HEAD_DIM), lambda i: (0, 0)),
            pl.BlockSpec((DEPTH, MIX_HALF), lambda i: (0, 0)),
        ],
        out_specs=[
            pl.BlockSpec((bt, 2 * MIX_HALF), lambda i: (i, 0)),
            pl.BlockSpec((bt, (CONV_W - 1) * CONV_DIM), lambda i: (i, 0)),
            state_spec,
            state_spec,
        ],
        out_shape=[
            jax.ShapeDtypeStruct((batch, 2 * MIX_HALF), BF16),
            jax.ShapeDtypeStruct((batch, (CONV_W - 1) * CONV_DIM), F32),
            state_shape,
            state_shape,
        ],
        scratch_shapes=[
            pltpu.VMEM((bt, MIX_HALF), F32),
            pltpu.VMEM((bt, MIX_HALF), F32),
        ],
        compiler_params=pltpu.CompilerParams(
            dimension_semantics=("arbitrary",), vmem_limit_bytes=VMEM_LIMIT),
        name="mixers_decode",
    )(proj, proj, proj, proj, conv_state, s_delta, s_hgrn, conv_w, a_log_row, dt_bias_row,
      norm_a.reshape(1, HEAD_DIM), norm_b.reshape(1, HEAD_DIM), lb_logits)


FF_CHUNK = 256


def _out_ffn_kernel(x_ref, ma_ref, mb_ref, g1_ref, sh2_ref, sc2_ref, g2_ref, wo_ref, n2_ref,
                    wg_ref, wu_ref, wd_ref, nf_ref, o_ref, acc_ref, *, final_norm):
    mix = _mm(ma_ref[...], wo_ref[0:MIX_HALF, :]) + _mm(mb_ref[...], wo_ref[MIX_HALF:, :])
    x1 = x_ref[...] + g1_ref[...] * mix
    h = (_rms(x1, n2_ref[...]) * (1.0 + sc2_ref[...]) + sh2_ref[...]).astype(BF16)
    for c in range(D_FF // FF_CHUNK):
        fs = slice(c * FF_CHUNK, (c + 1) * FF_CHUNK)
        gate = jnp.dot(h, wg_ref[:, fs], preferred_element_type=F32)
        up = jnp.dot(h, wu_ref[:, fs], preferred_element_type=F32)
        part = _mm(_silu(gate) * up, wd_ref[fs, :])
        if c == 0:
            acc_ref[...] = part
        else:
            acc_ref[...] += part
    x2 = x1 + g2_ref[...] * acc_ref[...]
    if final_norm:
        x2 = _rms(x2, nf_ref[...])
    o_ref[...] = x2


def _out_ffn_call(x, mix_a, mix_b, mix_b_col, mods, w_out, norm2, w_gate, w_up, w_down, norm_f,
                  *, per_token, seq_len, final_norm):
    rows = x.shape[0]
    tm = rows if per_token else 512
    tiles_per_seq = 1 if per_token else seq_len // tm
    const = lambda i: (0, 0)
    resident = functools.partial(pl.BlockSpec, index_map=const, pipeline_mode=pl.Buffered(1))
    kern = functools.partial(_out_ffn_kernel, final_norm=final_norm)
    return pl.pallas_call(
        kern,
        grid=(rows // tm,),
        in_specs=[
            pl.BlockSpec((tm, D_MODEL), lambda i: (i, 0)),
            pl.BlockSpec((tm, MIX_HALF), lambda i: (i, 0)),
            pl.BlockSpec((tm, MIX_HALF), lambda i: (i, mix_b_col)),
            _mod_spec(per_token, tm, tiles_per_seq, MOD_GATE1),
            _mod_spec(per_token, tm, tiles_per_seq, MOD_SHIFT2),
            _mod_spec(per_token, tm, tiles_per_seq, MOD_SCALE2),
            _mod_spec(per_token, tm, tiles_per_seq, MOD_GATE2),
            resident((D_MODEL, D_MODEL)),
            pl.BlockSpec((1, D_MODEL), const),
            resident((D_MODEL, D_FF)),
            resident((D_MODEL, D_FF)),
            resident((D_FF, D_MODEL)),
            pl.BlockSpec((1, D_MODEL), const),
        ],
        out_specs=pl.BlockSpec((tm, D_MODEL), lambda i: (i, 0)),
        out_shape=jax.ShapeDtypeStruct((rows, D_MODEL), F32),
        scratch_shapes=[pltpu.VMEM((tm, D_MODEL), F32)],
        compiler_params=pltpu.CompilerParams(
            dimension_semantics=("arbitrary",), vmem_limit_bytes=VMEM_LIMIT),
        name="out_ffn",
    )(x, mix_a, mix_b, mods, mods, mods, mods, w_out, norm2.reshape(1, D_MODEL),
      w_gate, w_up, w_down, norm_f.reshape(1, D_MODEL))


def _reorder_in_proj(w):
    qkvz = w[:, 0:COL_B]
    ab = w[:, COL_B:COL_B + 2 * N_HEADS]
    rest = w[:, COL_B + 2 * N_HEADS:]
    pad = jnp.zeros((D_MODEL, 128 - 2 * N_HEADS), w.dtype)
    return jnp.concatenate([qkvz, rest, ab, pad], axis=1).astype(BF16)


def _lane_row(v):
    return jnp.pad(v.astype(F32), (0, 128 - v.shape[0])).reshape(1, 128)


def kernel(x_prompt, x_sample, c_prompt, c_sample, state_delta, state_conv, state_hgrn, w_ada, b_ada,
           norm1, w_in, conv_w, a_log, dt_bias, norm_a, norm_b, lb_logits, w_out, norm2, w_gate, w_up,
           w_down, norm_f):
    bp, seq_len, _ = x_prompt.shape
    bs = x_sample.shape[0]

    mods = _ada_call(jnp.concatenate([c_prompt, c_sample], axis=0), w_ada, b_ada)

    xp = x_prompt.reshape(bp * seq_len, D_MODEL)
    xs = x_sample.reshape(bs, D_MODEL)
    zero_buf = jnp.zeros((bp, 8, CONV_DIM), F32)
    zero_state = jnp.zeros((bp, N_HEADS, HEAD_DIM, HEAD_DIM), F32)

    delta_p, conv_p, hgrn_p, delta_s, conv_s, hgrn_s = [], [], [], [], [], []
    for l in range(DEPTH):
        last = l == DEPTH - 1
        w_in_l = _reorder_in_proj(w_in[l])
        w_out_l = w_out[l].astype(BF16)
        w_gate_l = w_gate[l].astype(BF16)
        w_up_l = w_up[l].astype(BF16)
        w_down_l = w_down[l].astype(BF16)
        a_log_row = _lane_row(a_log[l])
        dt_bias_row = _lane_row(dt_bias[l])
        mods_p = mods[l, :bp].reshape(bp, 1, 6 * D_MODEL)
        mods_s = mods[l, bp:]

        proj = _in_proj_call(xp, mods_p, norm1[l], w_in_l, per_token=False, seq_len=seq_len)
        mix_a, s_a, cv = _gdn_call(proj, zero_buf, zero_state, conv_w[l], a_log_row, dt_bias_row,
                                   norm_a[l], batch=bp, seq_len=seq_len)
        mix_b, s_b = _hgrn_call(proj, lb_logits, norm_b[l], zero_state, layer=l, batch=bp,
                                seq_len=seq_len)
        xp = _out_ffn_call(xp, mix_a, mix_b, 0, mods_p, w_out_l, norm2[l], w_gate_l, w_up_l, w_down_l,
                           norm_f, per_token=False, seq_len=seq_len, final_norm=last)
        delta_p.append(s_a)
        conv_p.append(cv)
        hgrn_p.append(s_b)

        proj = _in_proj_call(xs, mods_s, norm1[l], w_in_l, per_token=True, seq_len=1)
        mix, cv, s_a, s_b = _decode_call(
            proj, state_conv[l].reshape(bs, (CONV_W - 1) * CONV_DIM), state_delta[l], state_hgrn[l],
            conv_w[l], a_log_row, dt_bias_row, norm_a[l], norm_b[l], lb_logits, layer=l)
        xs = _out_ffn_call(xs, mix, mix, 1, mods_s, w_out_l, norm2[l], w_gate_l, w_up_l, w_down_l,
                           norm_f, per_token=True, seq_len=1, final_norm=last)
        delta_s.append(s_a)
        conv_s.append(cv.reshape(bs, CONV_W - 1, CONV_DIM))
        hgrn_s.append(s_b)

    return (xp.reshape(bp, seq_len, D_MODEL), xs.reshape(bs, 1, D_MODEL),
            jnp.stack(delta_p), jnp.stack(conv_p), jnp.stack(hgrn_p),
            jnp.stack(delta_s), jnp.stack(conv_s), jnp.stack(hgrn_s))
```

```python
import functools

import jax
import jax.numpy as jnp
from jax import lax
from jax.experimental import pallas as pl
from jax.experimental.pallas import tpu as pltpu

F32 = jnp.float32
BF16 = jnp.bfloat16

D_MODEL = 1024
DEPTH = 2
N_HEADS = 4
HEAD_DIM = 128
MIX_HALF = N_HEADS * HEAD_DIM
CONV_W = 4
CONV_DIM = 3 * MIX_HALF
D_FF = 2816
EPS = 1e-6
GDN_CHUNK = 128
GDN_SUB = 16
HGRN_CHUNK = 16

COL_QKV = 0
COL_Z = CONV_DIM
COL_B = COL_Z + MIX_HALF
COL_AB = COL_B + 4 * MIX_HALF
IN_COLS = COL_AB + 128

MOD_SHIFT1, MOD_SCALE1, MOD_GATE1, MOD_SHIFT2, MOD_SCALE2, MOD_GATE2 = range(6)

VMEM_LIMIT = 56 * 1024 * 1024


def _sigmoid(x):
    return 1.0 / (1.0 + jnp.exp(-x))


def _silu(x):
    return x * _sigmoid(x)


def _softplus(x):
    return jnp.maximum(x, 0.0) + jnp.log1p(jnp.exp(-jnp.abs(x)))


def _mm(a, b):
    return jnp.dot(a.astype(BF16), b.astype(BF16), preferred_element_type=F32)


def _mm_nt(a, b):
    return lax.dot_general(a.astype(BF16), b.astype(BF16), (((1,), (1,)), ((), ())),
                           preferred_element_type=F32)


def _mm_tn(a, b):
    return lax.dot_general(a.astype(BF16), b.astype(BF16), (((0,), (0,)), ((), ())),
                           preferred_element_type=F32)


def _rms(x, w):
    return x * lax.rsqrt(jnp.mean(x * x, axis=-1, keepdims=True) + EPS) * w


def _chunk_cumsum(x, chunk):
    rows = lax.broadcasted_iota(jnp.int32, x.shape, 0) & (chunk - 1)
    s = 1
    while s < chunk:
        x = x + jnp.where(rows >= s, pltpu.roll(x, s, 0), 0.0)
        s *= 2
    return x


def _ada_kernel(c_ref, w_ref, b_ref, o_ref):
    a = _silu(c_ref[...])
    o_ref[...] = _mm(a, w_ref[...]) + b_ref[...]


def _ada_call(c_all, w_ada, b_ada):
    rows = c_all.shape[0]
    tn = 1024
    return pl.pallas_call(
        _ada_kernel,
        grid=(DEPTH, 6 * D_MODEL // tn),
        in_specs=[
            pl.BlockSpec((rows, D_MODEL), lambda l, j: (0, 0)),
            pl.BlockSpec((None, D_MODEL, tn), lambda l, j: (l, 0, j)),
            pl.BlockSpec((None, 1, tn), lambda l, j: (l, 0, j)),
        ],
        out_specs=pl.BlockSpec((None, rows, tn), lambda l, j: (l, 0, j)),
        out_shape=jax.ShapeDtypeStruct((DEPTH, rows, 6 * D_MODEL), F32),
        compiler_params=pltpu.CompilerParams(
            dimension_semantics=("arbitrary", "arbitrary"), vmem_limit_bytes=VMEM_LIMIT),
        name="ada_mod",
    )(c_all, w_ada, b_ada.reshape(DEPTH, 1, 6 * D_MODEL))


def _mod_spec(per_token, tm, tiles_per_seq, col):
    if per_token:
        return pl.BlockSpec((tm, D_MODEL), lambda i: (i, col))
    return pl.BlockSpec((None, 1, D_MODEL), lambda i: (i // tiles_per_seq, 0, col))


def _in_proj_kernel(x_ref, scale_ref, shift_ref, nw_ref, w_ref, o_ref):
    h = _rms(x_ref[...], nw_ref[...]) * (1.0 + scale_ref[...]) + shift_ref[...]
    o_ref[...] = jnp.dot(h.astype(BF16), w_ref[...], preferred_element_type=F32)


def _in_proj_call(x, mods, norm_w, w_in, *, per_token, seq_len):
    rows = x.shape[0]
    tm = rows if per_token else 512
    tiles_per_seq = 1 if per_token else seq_len // tm
    return pl.pallas_call(
        _in_proj_kernel,
        grid=(rows // tm,),
        in_specs=[
            pl.BlockSpec((tm, D_MODEL), lambda i: (i, 0)),
            _mod_spec(per_token, tm, tiles_per_seq, MOD_SCALE1),
            _mod_spec(per_token, tm, tiles_per_seq, MOD_SHIFT1),
            pl.BlockSpec((1, D_MODEL), lambda i: (0, 0)),
            pl.BlockSpec((D_MODEL, IN_COLS), lambda i: (0, 0), pipeline_mode=pl.Buffered(1)),
        ],
        out_specs=pl.BlockSpec((tm, IN_COLS), lambda i: (i, 0)),
        out_shape=jax.ShapeDtypeStruct((rows, IN_COLS), F32),
        compiler_params=pltpu.CompilerParams(
            dimension_semantics=("arbitrary",), vmem_limit_bytes=VMEM_LIMIT),
        name="in_proj",
    )(x, mods, mods, norm_w.reshape(1, D_MODEL), w_in)


def _unit_lower_inverse(mats, sub_mask, eye):
    d = [jnp.where(sub_mask, a, 0.0) for a in mats]
    n = [a - x for a, x in zip(mats, d)]
    d2 = [_mm(x, x) for x in d]
    d4 = [_mm(x, x) for x in d2]
    d8 = [_mm(x, x) for x in d4]
    p = [eye - x + x2 - _mm(x, x2) for x, x2 in zip(d, d2)]
    p = [x + _mm(x, y) for x, y in zip(p, d4)]
    dinv = [x + _mm(x, y) for x, y in zip(p, d8)]
    e = [_mm(x, y) for x, y in zip(dinv, n)]
    e2 = [_mm(x, x) for x in e]
    e4 = [_mm(x, x) for x in e2]
    f = [x2 - x - _mm(x, x2) for x, x2 in zip(e, e2)]
    q = [x + y + _mm(x, y) for x, y in zip(f, e4)]
    return [x + _mm(y, x) for x, y in zip(dinv, q)]


def _gdn_kernel(qkv_ref, z_ref, ab_ref, buf_ref, s0_ref, cw_ref, alog_ref, dtb_ref, na_ref,
                mix_ref, sfin_ref, cfin_ref, xpad, s_scr, *, tile):
    t = pl.program_id(1)

    @pl.when(t == 0)
    def _():
        xpad[0:8, :] = buf_ref[...]
        s_scr[...] = s0_ref[...]

    xpad[8:8 + tile, :] = qkv_ref[...]
    cw = cw_ref[...]
    acc = xpad[5:5 + tile, :] * cw[0:1, :]
    for j in range(1, CONV_W):
        acc = acc + xpad[5 + j:5 + j + tile, :] * cw[j:j + 1, :]
    conv = _silu(acc)
    tail = xpad[tile:tile + 8, :]
    xpad[0:8, :] = tail

    ab = ab_ref[...]
    g_all = -jnp.exp(alog_ref[...]) * _softplus(ab + dtb_ref[...])
    sig_all = _sigmoid(ab)
    g_cum = _chunk_cumsum(g_all, GDN_CHUNK)
    g_cum_t = g_cum.T

    ri = lax.broadcasted_iota(jnp.int32, (GDN_CHUNK, GDN_CHUNK), 0)
    ci = lax.broadcasted_iota(jnp.int32, (GDN_CHUNK, GDN_CHUNK), 1)
    m_incl = ri >= ci
    m_strict = ri > ci
    sub_mask = jnp.bitwise_xor(ri, ci) < GDN_SUB
    eye = jnp.where(ri == ci, 1.0, 0.0).astype(F32)
    n_chunks = tile // GDN_CHUNK
    heads = range(N_HEADS)
    chains = [(h, c) for h in heads for c in range(n_chunks)]

    def rows(c):
        return slice(c * GDN_CHUNK, (c + 1) * GDN_CHUNK)

    qn, kn, kb, gc, eg, rhs = [], [], [], [], [], []
    for h in heads:
        cs = slice(h * HEAD_DIM, (h + 1) * HEAD_DIM)
        qh = conv[:, cs]
        kh = conv[:, MIX_HALF + h * HEAD_DIM:MIX_HALF + (h + 1) * HEAD_DIM]
        vh = conv[:, 2 * MIX_HALF + h * HEAD_DIM:2 * MIX_HALF + (h + 1) * HEAD_DIM]
        qn.append(qh * lax.rsqrt(jnp.sum(qh * qh, axis=-1, keepdims=True) + EPS)
                  * (HEAD_DIM ** -0.5))
        kn.append(kh * lax.rsqrt(jnp.sum(kh * kh, axis=-1, keepdims=True) + EPS))
        beta = sig_all[:, N_HEADS + h:N_HEADS + h + 1]
        gc.append(g_cum[:, h:h + 1])
        eg.append(jnp.exp(gc[h]))
        kb.append(kn[h] * beta)
        rhs.append(jnp.concatenate([vh * beta, kb[h] * eg[h]], axis=1))

    dec = {}
    for h, c in chains:
        gr = g_cum_t[h:h + 1, rows(c)]
        dec[h, c] = jnp.exp(jnp.where(m_incl, gc[h][rows(c)] - gr, -jnp.inf))
    st = {hc: _mm_nt(jnp.concatenate([kb[hc[0]][rows(hc[1])], qn[hc[0]][rows(hc[1])]], axis=0),
                     kn[hc[0]][rows(hc[1])]) for hc in chains}
    a = [st[hc][:GDN_CHUNK] * jnp.where(m_strict, dec[hc], 0.0) for hc in chains]
    qk = {hc: st[hc][GDN_CHUNK:] * dec[hc] for hc in chains}
    tinv = dict(zip(chains, _unit_lower_inverse(a, sub_mask, eye)))
    uw = {hc: _mm(tinv[hc], rhs[hc[0]][rows(hc[1])]) for hc in chains}

    s = [s_scr[h] for h in heads]
    for c in range(n_chunks):
        rs = rows(c)
        r = [_mm(jnp.concatenate([uw[h, c][:, HEAD_DIM:], qn[h][rs] * eg[h][rs]], axis=0), s[h])
             for h in heads]
        v_new = [uw[h, c][:, :HEAD_DIM] - r[h][:GDN_CHUNK] for h in heads]
        for h in heads:
            g_last = gc[h][(c + 1) * GDN_CHUNK - 1:(c + 1) * GDN_CHUNK, :]
            kd = kn[h][rs] * jnp.exp(g_last - gc[h][rs])
            s[h] = s[h] * jnp.exp(g_last) + _mm_tn(kd, v_new[h])
        for h in heads:
            cs = slice(h * HEAD_DIM, (h + 1) * HEAD_DIM)
            o = r[h][GDN_CHUNK:] + _mm(qk[h, c], v_new[h])
            mix_ref[rs, cs] = (_rms(o, na_ref[...]) * _silu(z_ref[rs, cs])).astype(mix_ref.dtype)
    for h in heads:
        s_scr[h] = s[h]

    @pl.when(t == pl.num_programs(1) - 1)
    def _():
        sfin_ref[...] = s_scr[...]
        cfin_ref[...] = tail[8 - (CONV_W - 1):, :]


def _gdn_call(proj, buf8, s0, conv_w, a_log_row, dt_bias_row, norm_a, *, batch, seq_len):
    tile = 256
    nt = seq_len // tile
    kern = functools.partial(_gdn_kernel, tile=tile)
    return pl.pallas_call(
        kern,
        grid=(batch, nt),
        in_specs=[
            pl.BlockSpec((tile, CONV_DIM), lambda b, t: (b * nt + t, COL_QKV // CONV_DIM)),
            pl.BlockSpec((tile, MIX_HALF), lambda b, t: (b * nt + t, COL_Z // MIX_HALF)),
            pl.BlockSpec((tile, 128), lambda b, t: (b * nt + t, COL_AB // 128)),
            pl.BlockSpec((None, 8, CONV_DIM), lambda b, t: (b, 0, 0)),
            pl.BlockSpec((None, N_HEADS, HEAD_DIM, HEAD_DIM), lambda b, t: (b, 0, 0, 0)),
            pl.BlockSpec((CONV_W, CONV_DIM), lambda b, t: (0, 0)),
            pl.BlockSpec((1, 128), lambda b, t: (0, 0)),
            pl.BlockSpec((1, 128), lambda b, t: (0, 0)),
            pl.BlockSpec((1, HEAD_DIM), lambda b, t: (0, 0)),
        ],
        out_specs=[
            pl.BlockSpec((tile, MIX_HALF), lambda b, t: (b * nt + t, 0)),
            pl.BlockSpec((None, N_HEADS, HEAD_DIM, HEAD_DIM), lambda b, t: (b, 0, 0, 0)),
            pl.BlockSpec((None, CONV_W - 1, CONV_DIM), lambda b, t: (b, 0, 0)),
        ],
        out_shape=[
            jax.ShapeDtypeStruct((batch * seq_len, MIX_HALF), BF16),
            jax.ShapeDtypeStruct((batch, N_HEADS, HEAD_DIM, HEAD_DIM), F32),
            jax.ShapeDtypeStruct((batch, CONV_W - 1, CONV_DIM), F32),
        ],
        scratch_shapes=[
            pltpu.VMEM((tile + 8, CONV_DIM), F32),
            pltpu.VMEM((N_HEADS, HEAD_DIM, HEAD_DIM), F32),
        ],
        compiler_params=pltpu.CompilerParams(
            dimension_semantics=("arbitrary", "arbitrary"), vmem_limit_bytes=VMEM_LIMIT),
        name="gdn_prompt",
    )(proj, proj, proj, buf8, s0, conv_w, a_log_row, dt_bias_row, norm_a.reshape(1, HEAD_DIM))


def _forget_lower_bound(lbl, layer):
    m = jnp.max(lbl, axis=0, keepdims=True)
    e = jnp.exp(lbl - m)
    sm = e / jnp.sum(e, axis=0, keepdims=True)
    cs = sm[0:1, :]
    for i in range(1, layer + 1):
        cs = cs + sm[i:i + 1, :]
    return cs - sm[0:1, :]


def _hgrn_kernel(p_ref, lbl_ref, nb_ref, s0_ref, mix_ref, sfin_ref,
                 q_scr, k_scr, g_scr, o_scr, st_scr, *, tile, layer):
    t = pl.program_id(1)

    @pl.when(t == 0)
    def _():
        for h in range(N_HEADS):
            st_scr[h] = s0_ref[h].T

    lb = _forget_lower_bound(lbl_ref[...], layer)
    f = lb + (1.0 - lb) * _sigmoid(p_ref[:, MIX_HALF:2 * MIX_HALF])
    q_scr[...] = _silu(p_ref[:, 0:MIX_HALF])
    k_scr[...] = 1.0 - f
    g_scr[...] = _chunk_cumsum(jnp.log(f), HGRN_CHUNK)

    rows = lax.broadcasted_iota(jnp.int32, (HGRN_CHUNK, HEAD_DIM), 0)

    def chunk_body(c, carry):
        r0 = pl.multiple_of(c * HGRN_CHUNK, HGRN_CHUNK)
        rs = pl.ds(r0, HGRN_CHUNK)
        for h in range(N_HEADS):
            cs = slice(h * HEAD_DIM, (h + 1) * HEAD_DIM)
            qc = q_scr[rs, cs]
            kc = k_scr[rs, cs]
            gc = g_scr[rs, cs]
            vc = p_ref[rs, 2 * MIX_HALF + h * HEAD_DIM:2 * MIX_HALF + (h + 1) * HEAD_DIM]
            o = jnp.zeros((HGRN_CHUNK, HEAD_DIM), F32)
            for j in range(HGRN_CHUNK):
                dec = jnp.exp(jnp.where(rows >= j, gc - gc[j:j + 1, :], -jnp.inf))
                a_j = jnp.sum(qc * kc[j:j + 1, :] * dec, axis=-1, keepdims=True)
                o = o + a_j * vc[j:j + 1, :]
            st = st_scr[h]
            o = o + _mm_nt(qc * jnp.exp(gc), st)
            g_last = gc[HGRN_CHUNK - 1:HGRN_CHUNK, :]
            st_scr[h] = st * jnp.exp(g_last) + _mm_tn(vc, kc * jnp.exp(g_last - gc))
            o_scr[rs, cs] = o
        return carry

    lax.fori_loop(0, tile // HGRN_CHUNK, chunk_body, 0)

    for h in range(N_HEADS):
        cs = slice(h * HEAD_DIM, (h + 1) * HEAD_DIM)
        gate = _sigmoid(p_ref[:, 3 * MIX_HALF + h * HEAD_DIM:3 * MIX_HALF + (h + 1) * HEAD_DIM])
        mix_ref[:, cs] = (_rms(o_scr[:, cs], nb_ref[...]) * gate).astype(mix_ref.dtype)

    @pl.when(t == pl.num_programs(1) - 1)
    def _():
        for h in range(N_HEADS):
            sfin_ref[h] = st_scr[h].T


def _hgrn_call(proj, lb_logits, norm_b, s0, *, layer, batch, seq_len):
    tile = 256
    nt = seq_len // tile
    kern = functools.partial(_hgrn_kernel, tile=tile, layer=layer)
    return pl.pallas_call(
        kern,
        grid=(batch, nt),
        in_specs=[
            pl.BlockSpec((tile, 4 * MIX_HALF), lambda b, t: (b * nt + t, COL_B // (4 * MIX_HALF))),
            pl.BlockSpec((DEPTH, MIX_HALF), lambda b, t: (0, 0)),
            pl.BlockSpec((1, HEAD_DIM), lambda b, t: (0, 0)),
            pl.BlockSpec((None, N_HEADS, HEAD_DIM, HEAD_DIM), lambda b, t: (b, 0, 0, 0)),
        ],
        out_specs=[
            pl.BlockSpec((tile, MIX_HALF), lambda b, t: (b * nt + t, 0)),
            pl.BlockSpec((None, N_HEADS, HEAD_DIM, HEAD_DIM), lambda b, t: (b, 0, 0, 0)),
        ],
        out_shape=[
            jax.ShapeDtypeStruct((batch * seq_len, MIX_HALF), BF16),
            jax.ShapeDtypeStruct((batch, N_HEADS, HEAD_DIM, HEAD_DIM), F32),
        ],
        scratch_shapes=[
            pltpu.VMEM((tile, MIX_HALF), F32),
            pltpu.VMEM((tile, MIX_HALF), F32),
            pltpu.VMEM((tile, MIX_HALF), F32),
            pltpu.VMEM((tile, MIX_HALF), F32),
            pltpu.VMEM((N_HEADS, HEAD_DIM, HEAD_DIM), F32),
        ],
        compiler_params=pltpu.CompilerParams(
            dimension_semantics=("arbitrary", "arbitrary"), vmem_limit_bytes=VMEM_LIMIT),
        name="hgrn_prompt",
    )(proj, lb_logits, norm_b.reshape(1, HEAD_DIM), s0)


def _decode_kernel(qkv_ref, z_ref, ab_ref, pb_ref, conv_ref, sd_ref, sh_ref,
                   cw_ref, alog_ref, dtb_ref, na_ref, nb_ref, lbl_ref,
                   mix_ref, convo_ref, sdo_ref, sho_ref, oa_scr, ob_scr, *, bt, layer):
    cw = cw_ref[...]
    u = qkv_ref[...]
    b0 = conv_ref[:, 0:CONV_DIM]
    b1 = conv_ref[:, CONV_DIM:2 * CONV_DIM]
    b2 = conv_ref[:, 2 * CONV_DIM:3 * CONV_DIM]
    acc = b0 * cw[0:1, :]
    acc = acc + b1 * cw[1:2, :]
    acc = acc + b2 * cw[2:3, :]
    acc = acc + u * cw[3:4, :]
    conv = _silu(acc)
    convo_ref[:, 0:CONV_DIM] = b1
    convo_ref[:, CONV_DIM:2 * CONV_DIM] = b2
    convo_ref[:, 2 * CONV_DIM:3 * CONV_DIM] = u

    ab = ab_ref[...]
    eg_all = jnp.exp(-jnp.exp(alog_ref[...]) * _softplus(ab + dtb_ref[...]))
    beta_all = _sigmoid(ab)

    lb = _forget_lower_bound(lbl_ref[...], layer)
    f = lb + (1.0 - lb) * _sigmoid(pb_ref[:, MIX_HALF:2 * MIX_HALF])
    qb = _silu(pb_ref[:, 0:MIX_HALF])

    k_rows, q_rows, v_rows = [], [], []
    for h in range(N_HEADS):
        cs = slice(h * HEAD_DIM, (h + 1) * HEAD_DIM)
        qh = conv[:, cs]
        kh = conv[:, MIX_HALF + h * HEAD_DIM:MIX_HALF + (h + 1) * HEAD_DIM]
        q_rows.append(qh * lax.rsqrt(jnp.sum(qh * qh, axis=-1, keepdims=True) + EPS)
                      * (HEAD_DIM ** -0.5))
        k_rows.append(kh * lax.rsqrt(jnp.sum(kh * kh, axis=-1, keepdims=True) + EPS))
        v_rows.append(conv[:, 2 * MIX_HALF + h * HEAD_DIM:2 * MIX_HALF + (h + 1) * HEAD_DIM])
    f_rows = [f[:, h * HEAD_DIM:(h + 1) * HEAD_DIM] for h in range(N_HEADS)]
    qb_rows = [qb[:, h * HEAD_DIM:(h + 1) * HEAD_DIM] for h in range(N_HEADS)]

    stacked = jnp.concatenate(k_rows + q_rows + f_rows + qb_rows, axis=0)
    cols = stacked.T

    def col(item, h, b):
        i = (item * N_HEADS + h) * bt + b
        return cols[:, i:i + 1]

    for b in range(bt):
        for h in range(N_HEADS):
            cs = slice(h * HEAD_DIM, (h + 1) * HEAD_DIM)
            s = sd_ref[b, h]
            kc = col(0, h, b)
            eg = eg_all[b:b + 1, h:h + 1]
            beta = beta_all[b:b + 1, N_HEADS + h:N_HEADS + h + 1]
            sk = jnp.sum(s * kc, axis=0, keepdims=True)
            v_new = beta * (v_rows[h][b:b + 1, :] - eg * sk)
            s_new = s * eg + kc * v_new
            sdo_ref[b, h] = s_new
            oa_scr[b:b + 1, cs] = jnp.sum(s_new * col(1, h, b), axis=0, keepdims=True)
            s = sh_ref[b, h]
            fc = col(2, h, b)
            vb = pb_ref[b:b + 1, 2 * MIX_HALF + h * HEAD_DIM:2 * MIX_HALF + (h + 1) * HEAD_DIM]
            s_new = s * fc + (1.0 - fc) * vb
            sho_ref[b, h] = s_new
            ob_scr[b:b + 1, cs] = jnp.sum(s_new * col(3, h, b), axis=0, keepdims=True)

    for h in range(N_HEADS):
        cs = slice(h * HEAD_DIM, (h + 1) * HEAD_DIM)
        mix_ref[:, cs] = (_rms(oa_scr[:, cs], na_ref[...]) * _silu(z_ref[:, cs])).astype(mix_ref.dtype)
        gate = _sigmoid(pb_ref[:, 3 * MIX_HALF + h * HEAD_DIM:3 * MIX_HALF + (h + 1) * HEAD_DIM])
        mix_ref[:, MIX_HALF + h * HEAD_DIM:MIX_HALF + (h + 1) * HEAD_DIM] = (
            _rms(ob_scr[:, cs], nb_ref[...]) * gate).astype(mix_ref.dtype)


def _decode_call(proj, conv_state, s_delta, s_hgrn, conv_w, a_log_row, dt_bias_row,
                 norm_a, norm_b, lb_logits, *, layer):
    batch = proj.shape[0]
    bt = 8
    kern = functools.partial(_decode_kernel, bt=bt, layer=layer)
    state_spec = pl.BlockSpec((bt, N_HEADS, HEAD_DIM, HEAD_DIM), lambda i: (i, 0, 0, 0))
    state_shape = jax.ShapeDtypeStruct((batch, N_HEADS, HEAD_DIM, HEAD_DIM), F32)
    return pl.pallas_call(
        kern,
        grid=(batch // bt,),
        in_specs=[
            pl.BlockSpec((bt, CONV_DIM), lambda i: (i, COL_QKV // CONV_DIM)),
            pl.BlockSpec((bt, MIX_HALF), lambda i: (i, COL_Z // MIX_HALF)),
            pl.BlockSpec((bt, 128), lambda i: (i, COL_AB // 128)),
            pl.BlockSpec((bt, 4 * MIX_HALF), lambda i: (i, COL_B // (4 * MIX_HALF))),
            pl.BlockSpec((bt, (CONV_W - 1) * CONV_DIM), lambda i: (i, 0)),
            state_spec,
            state_spec,
            pl.BlockSpec((CONV_W, CONV_DIM), lambda i: (0, 0)),
            pl.BlockSpec((1, 128), lambda i: (0, 0)),
            pl.BlockSpec((1, 128), lambda i: (0, 0)),
            pl.BlockSpec((1, HEAD_DIM), lambda i: (0, 0)),
            pl.BlockSpec((1, HEAD_DIM), lambda i: (0, 0)),
            pl.BlockSpec((DEPTH, MIX_HALF), lambda i: (0, 0)),
        ],
        out_specs=[
            pl.BlockSpec((bt, 2 * MIX_HALF), lambda i: (i, 0)),
            pl.BlockSpec((bt, (CONV_W - 1) * CONV_DIM), lambda i: (i, 0)),
            state_spec,
            state_spec,
        ],
        out_shape=[
            jax.ShapeDtypeStruct((batch, 2 * MIX_HALF), BF16),
            jax.ShapeDtypeStruct((batch, (CONV_W - 1) * CONV_DIM), F32),
            state_shape,
            state_shape,
        ],
        scratch_shapes=[
            pltpu.VMEM((bt, MIX_HALF), F32),
            pltpu.VMEM((bt, MIX_HALF), F32),
        ],
        compiler_params=pltpu.CompilerParams(
            dimension_semantics=("arbitrary",), vmem_limit_bytes=VMEM_LIMIT),
        name="mixers_decode",
    )(proj, proj, proj, proj, conv_state, s_delta, s_hgrn, conv_w, a_log_row, dt_bias_row,
      norm_a.reshape(1, HEAD_DIM), norm_b.reshape(1, HEAD_DIM), lb_logits)


FF_CHUNK = 256


def _out_ffn_kernel(x_ref, ma_ref, mb_ref, g1_ref, sh2_ref, sc2_ref, g2_ref, wo_ref, n2_ref,
                    wg_ref, wu_ref, wd_ref, nf_ref, o_ref, acc_ref, *, final_norm):
    mix = _mm(ma_ref[...], wo_ref[0:MIX_HALF, :]) + _mm(mb_ref[...], wo_ref[MIX_HALF:, :])
    x1 = x_ref[...] + g1_ref[...] * mix
    h = (_rms(x1, n2_ref[...]) * (1.0 + sc2_ref[...]) + sh2_ref[...]).astype(BF16)
    for c in range(D_FF // FF_CHUNK):
        fs = slice(c * FF_CHUNK, (c + 1) * FF_CHUNK)
        gate = jnp.dot(h, wg_ref[:, fs], preferred_element_type=F32)
        up = jnp.dot(h, wu_ref[:, fs], preferred_element_type=F32)
        part = _mm(_silu(gate) * up, wd_ref[fs, :])
        if c == 0:
            acc_ref[...] = part
        else:
            acc_ref[...] += part
    x2 = x1 + g2_ref[...] * acc_ref[...]
    if final_norm:
        x2 = _rms(x2, nf_ref[...])
    o_ref[...] = x2


def _out_ffn_call(x, mix_a, mix_b, mix_b_col, mods, w_out, norm2, w_gate, w_up, w_down, norm_f,
                  *, per_token, seq_len, final_norm):
    rows = x.shape[0]
    tm = rows if per_token else 512
    tiles_per_seq = 1 if per_token else seq_len // tm
    const = lambda i: (0, 0)
    resident = functools.partial(pl.BlockSpec, index_map=const, pipeline_mode=pl.Buffered(1))
    kern = functools.partial(_out_ffn_kernel, final_norm=final_norm)
    return pl.pallas_call(
        kern,
        grid=(rows // tm,),
        in_specs=[
            pl.BlockSpec((tm, D_MODEL), lambda i: (i, 0)),
            pl.BlockSpec((tm, MIX_HALF), lambda i: (i, 0)),
            pl.BlockSpec((tm, MIX_HALF), lambda i: (i, mix_b_col)),
            _mod_spec(per_token, tm, tiles_per_seq, MOD_GATE1),
            _mod_spec(per_token, tm, tiles_per_seq, MOD_SHIFT2),
            _mod_spec(per_token, tm, tiles_per_seq, MOD_SCALE2),
            _mod_spec(per_token, tm, tiles_per_seq, MOD_GATE2),
            resident((D_MODEL, D_MODEL)),
            pl.BlockSpec((1, D_MODEL), const),
            resident((D_MODEL, D_FF)),
            resident((D_MODEL, D_FF)),
            resident((D_FF, D_MODEL)),
            pl.BlockSpec((1, D_MODEL), const),
        ],
        out_specs=pl.BlockSpec((tm, D_MODEL), lambda i: (i, 0)),
        out_shape=jax.ShapeDtypeStruct((rows, D_MODEL), F32),
        scratch_shapes=[pltpu.VMEM((tm, D_MODEL), F32)],
        compiler_params=pltpu.CompilerParams(
            dimension_semantics=("arbitrary",), vmem_limit_bytes=VMEM_LIMIT),
        name="out_ffn",
    )(x, mix_a, mix_b, mods, mods, mods, mods, w_out, norm2.reshape(1, D_MODEL),
      w_gate, w_up, w_down, norm_f.reshape(1, D_MODEL))


def _reorder_in_proj(w):
    qkvz = w[:, 0:COL_B]
    ab = w[:, COL_B:COL_B + 2 * N_HEADS]
    rest = w[:, COL_B + 2 * N_HEADS:]
    pad = jnp.zeros((D_MODEL, 128 - 2 * N_HEADS), w.dtype)
    return jnp.concatenate([qkvz, rest, ab, pad], axis=1).astype(BF16)


def _lane_row(v):
    return jnp.pad(v.astype(F32), (0, 128 - v.shape[0])).reshape(1, 128)


def kernel(x_prompt, x_sample, c_prompt, c_sample, state_delta, state_conv, state_hgrn, w_ada, b_ada,
           norm1, w_in, conv_w, a_log, dt_bias, norm_a, norm_b, lb_logits, w_out, norm2, w_gate, w_up,
           w_down, norm_f):
    bp, seq_len, _ = x_prompt.shape
    bs = x_sample.shape[0]

    mods = _ada_call(jnp.concatenate([c_prompt, c_sample], axis=0), w_ada, b_ada)

    xp = x_prompt.reshape(bp * seq_len, D_MODEL)
    xs = x_sample.reshape(bs, D_MODEL)
    zero_buf = jnp.zeros((bp, 8, CONV_DIM), F32)
    zero_state = jnp.zeros((bp, N_HEADS, HEAD_DIM, HEAD_DIM), F32)

    delta_p, conv_p, hgrn_p, delta_s, conv_s, hgrn_s = [], [], [], [], [], []
    for l in range(DEPTH):
        last = l == DEPTH - 1
        w_in_l = _reorder_in_proj(w_in[l])
        w_out_l = w_out[l].astype(BF16)
        w_gate_l = w_gate[l].astype(BF16)
        w_up_l = w_up[l].astype(BF16)
        w_down_l = w_down[l].astype(BF16)
        a_log_row = _lane_row(a_log[l])
        dt_bias_row = _lane_row(dt_bias[l])
        mods_p = mods[l, :bp].reshape(bp, 1, 6 * D_MODEL)
        mods_s = mods[l, bp:]

        proj = _in_proj_call(xp, mods_p, norm1[l], w_in_l, per_token=False, seq_len=seq_len)
        mix_a, s_a, cv = _gdn_call(proj, zero_buf, zero_state, conv_w[l], a_log_row, dt_bias_row,
                                   norm_a[l], batch=bp, seq_len=seq_len)
        mix_b, s_b = _hgrn_call(proj, lb_logits, norm_b[l], zero_state, layer=l, batch=bp,
                                seq_len=seq_len)
        xp = _out_ffn_call(xp, mix_a, mix_b, 0, mods_p, w_out_l, norm2[l], w_gate_l, w_up_l, w_down_l,
                           norm_f, per_token=False, seq_len=seq_len, final_norm=last)
        delta_p.append(s_a)
        conv_p.append(cv)
        hgrn_p.append(s_b)

        proj = _in_proj_call(xs, mods_s, norm1[l], w_in_l, per_token=True, seq_len=1)
        mix, cv, s_a, s_b = _decode_call(
            proj, state_conv[l].reshape(bs, (CONV_W - 1) * CONV_DIM), state_delta[l], state_hgrn[l],
            conv_w[l], a_log_row, dt_bias_row, norm_a[l], norm_b[l], lb_logits, layer=l)
        xs = _out_ffn_call(xs, mix, mix, 1, mods_s, w_out_l, norm2[l], w_gate_l, w_up_l, w_down_l,
                           norm_f, per_token=True, seq_len=1, final_norm=last)
        delta_s.append(s_a)
        conv_s.append(cv.reshape(bs, CONV_W - 1, CONV_DIM))
        hgrn_s.append(s_b)

    return (xp.reshape(bp, seq_len, D_MODEL), xs.reshape(bs, 1, D_MODEL),
            jnp.stack(delta_p), jnp.stack(conv_p), jnp.stack(hgrn_p),
            jnp.stack(delta_s), jnp.stack(conv_s), jnp.stack(hgrn_s))
```

```python
import functools

import jax
import jax.numpy as jnp
from jax import lax
from jax.experimental import pallas as pl
from jax.experimental.pallas import tpu as pltpu

F32 = jnp.float32
BF16 = jnp.bfloat16

D_MODEL = 1024
DEPTH = 2
N_HEADS = 4
HEAD_DIM = 128
MIX_HALF = N_HEADS * HEAD_DIM
CONV_W = 4
CONV_DIM = 3 * MIX_HALF
D_FF = 2816
EPS = 1e-6
GDN_CHUNK = 128
GDN_SUB = 16
HGRN_CHUNK = 64

COL_QKV = 0
COL_Z = CONV_DIM
COL_B = COL_Z + MIX_HALF
COL_AB = COL_B + 4 * MIX_HALF
IN_COLS = COL_AB + 128

MOD_SHIFT1, MOD_SCALE1, MOD_GATE1, MOD_SHIFT2, MOD_SCALE2, MOD_GATE2 = range(6)

VMEM_LIMIT = 56 * 1024 * 1024


def _sigmoid(x):
    return 1.0 / (1.0 + jnp.exp(-x))


def _silu(x):
    return x * _sigmoid(x)


def _softplus(x):
    return jnp.maximum(x, 0.0) + jnp.log1p(jnp.exp(-jnp.abs(x)))


def _mm(a, b):
    return jnp.dot(a.astype(BF16), b.astype(BF16), preferred_element_type=F32)


def _mm_nt(a, b):
    return lax.dot_general(a.astype(BF16), b.astype(BF16), (((1,), (1,)), ((), ())),
                           preferred_element_type=F32)


def _mm_tn(a, b):
    return lax.dot_general(a.astype(BF16), b.astype(BF16), (((0,), (0,)), ((), ())),
                           preferred_element_type=F32)


def _rms(x, w):
    return x * lax.rsqrt(jnp.mean(x * x, axis=-1, keepdims=True) + EPS) * w


def _chunk_cumsum(x, chunk):
    rows = lax.broadcasted_iota(jnp.int32, x.shape, 0) & (chunk - 1)
    s = 1
    while s < chunk:
        x = x + jnp.where(rows >= s, pltpu.roll(x, s, 0), 0.0)
        s *= 2
    return x


def _ada_kernel(c_ref, w_ref, b_ref, o_ref):
    a = _silu(c_ref[...])
    o_ref[...] = _mm(a, w_ref[...]) + b_ref[...]


def _ada_call(c_all, w_ada, b_ada):
    rows = c_all.shape[0]
    tn = 1024
    return pl.pallas_call(
        _ada_kernel,
        grid=(DEPTH, 6 * D_MODEL // tn),
        in_specs=[
            pl.BlockSpec((rows, D_MODEL), lambda l, j: (0, 0)),
            pl.BlockSpec((None, D_MODEL, tn), lambda l, j: (l, 0, j)),
            pl.BlockSpec((None, 1, tn), lambda l, j: (l, 0, j)),
        ],
        out_specs=pl.BlockSpec((None, rows, tn), lambda l, j: (l, 0, j)),
        out_shape=jax.ShapeDtypeStruct((DEPTH, rows, 6 * D_MODEL), F32),
        compiler_params=pltpu.CompilerParams(
            dimension_semantics=("arbitrary", "arbitrary"), vmem_limit_bytes=VMEM_LIMIT),
        name="ada_mod",
    )(c_all, w_ada, b_ada.reshape(DEPTH, 1, 6 * D_MODEL))


def _mod_spec(mods, layer, n_prompt, per_token, col):
    n_decode = mods.shape[1] - n_prompt
    if per_token:
        return pl.BlockSpec((None, n_decode, D_MODEL), lambda i: (layer, 0, col))
    return pl.BlockSpec((None, n_prompt, D_MODEL), lambda i: (layer, n_decode // n_prompt, col))


def _mod_rows(ref, tiles_per_seq):
    if tiles_per_seq is None:
        return ref[...]
    return ref[pl.ds(pl.program_id(0) // tiles_per_seq, 1), :]


def _layer_spec(shape, layer, **kwargs):
    zeros = (0,) * len(shape)
    return pl.BlockSpec((None,) + tuple(shape), lambda *_: (layer,) + zeros, **kwargs)


def _in_proj_kernel(x_ref, scale_ref, shift_ref, nw_ref, w_ref, o_ref, *, tiles_per_seq):
    scale = _mod_rows(scale_ref, tiles_per_seq)
    shift = _mod_rows(shift_ref, tiles_per_seq)
    h = _rms(x_ref[...], nw_ref[...]) * (1.0 + scale) + shift
    o_ref[...] = jnp.dot(h.astype(BF16), w_ref[...], preferred_element_type=F32)


def _in_proj_call(x, mods, norm1, w_in, *, layer, n_prompt, per_token, seq_len):
    rows = x.shape[0]
    tm = rows if per_token else 512
    tiles_per_seq = None if per_token else seq_len // tm
    return pl.pallas_call(
        functools.partial(_in_proj_kernel, tiles_per_seq=tiles_per_seq),
        grid=(rows // tm,),
        in_specs=[
            pl.BlockSpec((tm, D_MODEL), lambda i: (i, 0)),
            _mod_spec(mods, layer, n_prompt, per_token, MOD_SCALE1),
            _mod_spec(mods, layer, n_prompt, per_token, MOD_SHIFT1),
            _layer_spec((1, D_MODEL), layer),
            _layer_spec((D_MODEL, IN_COLS), layer, pipeline_mode=pl.Buffered(1)),
        ],
        out_specs=pl.BlockSpec((tm, IN_COLS), lambda i: (i, 0)),
        out_shape=jax.ShapeDtypeStruct((rows, IN_COLS), F32),
        compiler_params=pltpu.CompilerParams(
            dimension_semantics=("arbitrary",), vmem_limit_bytes=VMEM_LIMIT),
        name="in_proj",
    )(x, mods, mods, norm1, w_in)


def _unit_lower_inverse(mats, sub_mask, eye):
    d = [jnp.where(sub_mask, a, 0.0) for a in mats]
    n = [a - x for a, x in zip(mats, d)]
    d2 = [_mm(x, x) for x in d]
    d4 = [_mm(x, x) for x in d2]
    d8 = [_mm(x, x) for x in d4]
    p = [eye - x + x2 - _mm(x, x2) for x, x2 in zip(d, d2)]
    p = [x + _mm(x, y) for x, y in zip(p, d4)]
    dinv = [x + _mm(x, y) for x, y in zip(p, d8)]
    e = [_mm(x, y) for x, y in zip(dinv, n)]
    e2 = [_mm(x, x) for x in e]
    e4 = [_mm(x, x) for x in e2]
    f = [x2 - x - _mm(x, x2) for x, x2 in zip(e, e2)]
    q = [x + y + _mm(x, y) for x, y in zip(f, e4)]
    return [x + _mm(y, x) for x, y in zip(dinv, q)]


def _gdn_kernel(qkv_ref, z_ref, ab_ref, buf_ref, s0_ref, cw_ref, alog_ref, dtb_ref, na_ref,
                mix_ref, sfin_ref, cfin_ref, xpad, s_scr, *, tile):
    t = pl.program_id(1)

    @pl.when(t == 0)
    def _():
        xpad[0:8, :] = buf_ref[...]
        s_scr[...] = s0_ref[...]

    xpad[8:8 + tile, :] = qkv_ref[...]
    cw = cw_ref[...]
    acc = xpad[5:5 + tile, :] * cw[0:1, :]
    for j in range(1, CONV_W):
        acc = acc + xpad[5 + j:5 + j + tile, :] * cw[j:j + 1, :]
    conv = _silu(acc)
    tail = xpad[tile:tile + 8, :]
    xpad[0:8, :] = tail

    ab = ab_ref[...]
    g_all = -jnp.exp(alog_ref[...]) * _softplus(ab + dtb_ref[...])
    sig_all = _sigmoid(ab)
    g_cum = _chunk_cumsum(g_all, GDN_CHUNK)
    g_cum_t = g_cum.T

    ri = lax.broadcasted_iota(jnp.int32, (GDN_CHUNK, GDN_CHUNK), 0)
    ci = lax.broadcasted_iota(jnp.int32, (GDN_CHUNK, GDN_CHUNK), 1)
    m_incl = ri >= ci
    m_strict = ri > ci
    sub_mask = jnp.bitwise_xor(ri, ci) < GDN_SUB
    eye = jnp.where(ri == ci, 1.0, 0.0).astype(F32)
    n_chunks = tile // GDN_CHUNK
    heads = range(N_HEADS)
    chains = [(h, c) for h in heads for c in range(n_chunks)]

    def rows(c):
        return slice(c * GDN_CHUNK, (c + 1) * GDN_CHUNK)

    qn, kn, kb, gc, eg, rhs = [], [], [], [], [], []
    for h in heads:
        cs = slice(h * HEAD_DIM, (h + 1) * HEAD_DIM)
        qh = conv[:, cs]
        kh = conv[:, MIX_HALF + h * HEAD_DIM:MIX_HALF + (h + 1) * HEAD_DIM]
        vh = conv[:, 2 * MIX_HALF + h * HEAD_DIM:2 * MIX_HALF + (h + 1) * HEAD_DIM]
        qn.append(qh * lax.rsqrt(jnp.sum(qh * qh, axis=-1, keepdims=True) + EPS)
                  * (HEAD_DIM ** -0.5))
        kn.append(kh * lax.rsqrt(jnp.sum(kh * kh, axis=-1, keepdims=True) + EPS))
        beta = sig_all[:, N_HEADS + h:N_HEADS + h + 1]
        gc.append(g_cum[:, h:h + 1])
        eg.append(jnp.exp(gc[h]))
        kb.append(kn[h] * beta)
        rhs.append(jnp.concatenate([vh * beta, kb[h] * eg[h]], axis=1))

    dec = {}
    for h, c in chains:
        gr = g_cum_t[h:h + 1, rows(c)]
        dec[h, c] = jnp.exp(jnp.where(m_incl, gc[h][rows(c)] - gr, -jnp.inf))
    st = {hc: _mm_nt(jnp.concatenate([kb[hc[0]][rows(hc[1])], qn[hc[0]][rows(hc[1])]], axis=0),
                     kn[hc[0]][rows(hc[1])]) for hc in chains}
    a = [st[hc][:GDN_CHUNK] * jnp.where(m_strict, dec[hc], 0.0) for hc in chains]
    qk = {hc: st[hc][GDN_CHUNK:] * dec[hc] for hc in chains}
    tinv = dict(zip(chains, _unit_lower_inverse(a, sub_mask, eye)))
    uw = {hc: _mm(tinv[hc], rhs[hc[0]][rows(hc[1])]) for hc in chains}

    s = [s_scr[h] for h in heads]
    for c in range(n_chunks):
        rs = rows(c)
        r = [_mm(jnp.concatenate([uw[h, c][:, HEAD_DIM:], qn[h][rs] * eg[h][rs]], axis=0), s[h])
             for h in heads]
        v_new = [uw[h, c][:, :HEAD_DIM] - r[h][:GDN_CHUNK] for h in heads]
        for h in heads:
            g_last = gc[h][(c + 1) * GDN_CHUNK - 1:(c + 1) * GDN_CHUNK, :]
            kd = kn[h][rs] * jnp.exp(g_last - gc[h][rs])
            s[h] = s[h] * jnp.exp(g_last) + _mm_tn(kd, v_new[h])
        for h in heads:
            cs = slice(h * HEAD_DIM, (h + 1) * HEAD_DIM)
            o = r[h][GDN_CHUNK:] + _mm(qk[h, c], v_new[h])
            mix_ref[rs, cs] = (_rms(o, na_ref[...]) * _silu(z_ref[rs, cs])).astype(mix_ref.dtype)
    for h in heads:
        s_scr[h] = s[h]

    @pl.when(t == pl.num_programs(1) - 1)
    def _():
        sfin_ref[...] = s_scr[...]
        cfin_ref[...] = tail[8 - (CONV_W - 1):, :]


def _gdn_call(proj, buf8, s0, conv_w, a_log_rows, dt_bias_rows, norm_a, *, layer, batch, seq_len):
    tile = 256
    nt = seq_len // tile
    kern = functools.partial(_gdn_kernel, tile=tile)
    return pl.pallas_call(
        kern,
        grid=(batch, nt),
        in_specs=[
            pl.BlockSpec((tile, CONV_DIM), lambda b, t: (b * nt + t, COL_QKV // CONV_DIM)),
            pl.BlockSpec((tile, MIX_HALF), lambda b, t: (b * nt + t, COL_Z // MIX_HALF)),
            pl.BlockSpec((tile, 128), lambda b, t: (b * nt + t, COL_AB // 128)),
            pl.BlockSpec((None, 8, CONV_DIM), lambda b, t: (b, 0, 0)),
            pl.BlockSpec((None, N_HEADS, HEAD_DIM, HEAD_DIM), lambda b, t: (b, 0, 0, 0)),
            _layer_spec((CONV_W, CONV_DIM), layer),
            _layer_spec((1, 128), layer),
            _layer_spec((1, 128), layer),
            _layer_spec((1, HEAD_DIM), layer),
        ],
        out_specs=[
            pl.BlockSpec((tile, MIX_HALF), lambda b, t: (b * nt + t, 0)),
            pl.BlockSpec((None, N_HEADS, HEAD_DIM, HEAD_DIM), lambda b, t: (b, 0, 0, 0)),
            pl.BlockSpec((None, CONV_W - 1, CONV_DIM), lambda b, t: (b, 0, 0)),
        ],
        out_shape=[
            jax.ShapeDtypeStruct((batch * seq_len, MIX_HALF), BF16),
            jax.ShapeDtypeStruct((batch, N_HEADS, HEAD_DIM, HEAD_DIM), F32),
            jax.ShapeDtypeStruct((batch, CONV_W - 1, CONV_DIM), F32),
        ],
        scratch_shapes=[
            pltpu.VMEM((tile + 8, CONV_DIM), F32),
            pltpu.VMEM((N_HEADS, HEAD_DIM, HEAD_DIM), F32),
        ],
        compiler_params=pltpu.CompilerParams(
            dimension_semantics=("arbitrary", "arbitrary"), vmem_limit_bytes=VMEM_LIMIT),
        name="gdn_prompt",
    )(proj, proj, proj, buf8, s0, conv_w, a_log_rows, dt_bias_rows, norm_a)


def _forget_lower_bound(lbl, layer):
    m = jnp.max(lbl, axis=0, keepdims=True)
    e = jnp.exp(lbl - m)
    sm = e / jnp.sum(e, axis=0, keepdims=True)
    cs = sm[0:1, :]
    for i in range(1, layer + 1):
        cs = cs + sm[i:i + 1, :]
    return cs - sm[0:1, :]


def _midpoint_rows(g, s, row_id):
    n = g.shape[0]
    if s >= 4:
        parts = [jnp.broadcast_to(g[b + s:b + s + 1, :], (2 * s, g.shape[1]))
                 for b in range(0, n, 2 * s)]
        return jnp.concatenate(parts, axis=0)
    nxt = pltpu.roll(g, n - 1, 0)
    if s == 1:
        return jnp.where((row_id & 1) == 1, g, nxt)
    pos = row_id & 3
    return jnp.where(pos == 0, pltpu.roll(g, n - 2, 0),
                     jnp.where(pos == 1, nxt, jnp.where(pos == 2, g, pltpu.roll(g, 1, 0))))


def _hgrn_kernel(p_ref, lbl_ref, nb_ref, s0_ref, mix_ref, sfin_ref, st_scr, *, tile, layer):
    t = pl.program_id(1)

    @pl.when(t == 0)
    def _():
        for h in range(N_HEADS):
            st_scr[h] = s0_ref[h].T

    heads = range(N_HEADS)
    n_chunks = tile // HGRN_CHUNK
    chains = [(h, c) for h in heads for c in range(n_chunks)]

    def rows(c):
        return slice(c * HGRN_CHUNK, (c + 1) * HGRN_CHUNK)

    lb_all = _forget_lower_bound(lbl_ref[...], layer)
    row_id = lax.broadcasted_iota(jnp.int32, (tile, HEAD_DIM), 0)
    ri = lax.broadcasted_iota(jnp.int32, (HGRN_CHUNK, HGRN_CHUNK), 0)
    ci = lax.broadcasted_iota(jnp.int32, (HGRN_CHUNK, HGRN_CHUNK), 1)
    dist = jnp.bitwise_xor(ri, ci)
    lower = ri > ci

    q, k, g = [], [], []
    for h in heads:
        cs = slice(h * HEAD_DIM, (h + 1) * HEAD_DIM)
        lb = lb_all[:, cs]
        f = lb + (1.0 - lb) * _sigmoid(p_ref[:, MIX_HALF + h * HEAD_DIM:MIX_HALF + (h + 1) * HEAD_DIM])
        q.append(_silu(p_ref[:, cs]))
        k.append(1.0 - f)
        g.append(_chunk_cumsum(jnp.log(f), HGRN_CHUNK))

    a = {(h, c): jnp.where(ri == ci, _mm_nt(q[h][rows(c)], k[h][rows(c)]), 0.0) for h, c in chains}
    s = HGRN_CHUNK // 2
    while s >= 1:
        upper = (row_id & s) != 0
        level = lower & (dist >= s) & (dist < 2 * s)
        x = []
        for h in heads:
            d = g[h] - _midpoint_rows(g[h], s, row_id)
            x.append(jnp.where(upper, q[h], k[h]) * jnp.exp(jnp.where(upper, d, -d)))
        for h, c in chains:
            xc = x[h][rows(c)]
            a[h, c] = a[h, c] + jnp.where(level, _mm_nt(xc, xc), 0.0)
        s //= 2

    st = [st_scr[h] for h in heads]
    for c in range(n_chunks):
        rs = rows(c)
        for h in heads:
            cs = slice(h * HEAD_DIM, (h + 1) * HEAD_DIM)
            gc = g[h][rs]
            vc = p_ref[rs, 2 * MIX_HALF + h * HEAD_DIM:2 * MIX_HALF + (h + 1) * HEAD_DIM]
            o = _mm(a[h, c], vc) + _mm_nt(q[h][rs] * jnp.exp(gc), st[h])
            g_last = gc[HGRN_CHUNK - 1:HGRN_CHUNK, :]
            st[h] = st[h] * jnp.exp(g_last) + _mm_tn(vc, k[h][rs] * jnp.exp(g_last - gc))
            gate = _sigmoid(p_ref[rs, 3 * MIX_HALF + h * HEAD_DIM:3 * MIX_HALF + (h + 1) * HEAD_DIM])
            mix_ref[rs, cs] = (_rms(o, nb_ref[...]) * gate).astype(mix_ref.dtype)
    for h in heads:
        st_scr[h] = st[h]

    @pl.when(t == pl.num_programs(1) - 1)
    def _():
        for h in heads:
            sfin_ref[h] = st_scr[h].T


def _hgrn_call(proj, lb_logits, norm_b, s0, *, layer, batch, seq_len):
    tile = 256
    nt = seq_len // tile
    kern = functools.partial(_hgrn_kernel, tile=tile, layer=layer)
    return pl.pallas_call(
        kern,
        grid=(batch, nt),
        in_specs=[
            pl.BlockSpec((tile, 4 * MIX_HALF), lambda b, t: (b * nt + t, COL_B // (4 * MIX_HALF))),
            pl.BlockSpec((DEPTH, MIX_HALF), lambda b, t: (0, 0)),
            _layer_spec((1, HEAD_DIM), layer),
            pl.BlockSpec((None, N_HEADS, HEAD_DIM, HEAD_DIM), lambda b, t: (b, 0, 0, 0)),
        ],
        out_specs=[
            pl.BlockSpec((tile, MIX_HALF), lambda b, t: (b * nt + t, 0)),
            pl.BlockSpec((None, N_HEADS, HEAD_DIM, HEAD_DIM), lambda b, t: (b, 0, 0, 0)),
        ],
        out_shape=[
            jax.ShapeDtypeStruct((batch * seq_len, MIX_HALF), BF16),
            jax.ShapeDtypeStruct((batch, N_HEADS, HEAD_DIM, HEAD_DIM), F32),
        ],
        scratch_shapes=[pltpu.VMEM((N_HEADS, HEAD_DIM, HEAD_DIM), F32)],
        compiler_params=pltpu.CompilerParams(
            dimension_semantics=("arbitrary", "arbitrary"), vmem_limit_bytes=VMEM_LIMIT),
        name="hgrn_prompt",
    )(proj, lb_logits, norm_b, s0)


def _decode_kernel(*refs, bt, layer):
    (qkv_ref, z_ref, ab_ref, pb_ref, conv_ref, sd_ref, sh_ref,
     cw_ref, alog_ref, dtb_ref, na_ref, nb_ref, lbl_ref) = refs[:13]
    mix_ref, convo_ref, sdo_ref, sho_ref, oa_scr, ob_scr = refs[-6:]
    if layer > 0:
        prev_sd_ref, prev_sh_ref = refs[13:15]
        sdo_ref[0:layer] = prev_sd_ref[...]
        sho_ref[0:layer] = prev_sh_ref[...]
    cw = cw_ref[...]
    u = qkv_ref[...]
    b0 = conv_ref[:, 0:CONV_DIM]
    b1 = conv_ref[:, CONV_DIM:2 * CONV_DIM]
    b2 = conv_ref[:, 2 * CONV_DIM:3 * CONV_DIM]
    acc = b0 * cw[0:1, :]
    acc = acc + b1 * cw[1:2, :]
    acc = acc + b2 * cw[2:3, :]
    acc = acc + u * cw[3:4, :]
    conv = _silu(acc)
    convo_ref[:, 0:CONV_DIM] = b1
    convo_ref[:, CONV_DIM:2 * CONV_DIM] = b2
    convo_ref[:, 2 * CONV_DIM:3 * CONV_DIM] = u

    ab = ab_ref[...]
    eg_all = jnp.exp(-jnp.exp(alog_ref[...]) * _softplus(ab + dtb_ref[...]))
    beta_all = _sigmoid(ab)

    lb = _forget_lower_bound(lbl_ref[...], layer)
    f = lb + (1.0 - lb) * _sigmoid(pb_ref[:, MIX_HALF:2 * MIX_HALF])
    qb = _silu(pb_ref[:, 0:MIX_HALF])

    k_rows, q_rows, v_rows = [], [], []
    for h in range(N_HEADS):
        cs = slice(h * HEAD_DIM, (h + 1) * HEAD_DIM)
        qh = conv[:, cs]
        kh = conv[:, MIX_HALF + h * HEAD_DIM:MIX_HALF + (h + 1) * HEAD_DIM]
        q_rows.append(qh * lax.rsqrt(jnp.sum(qh * qh, axis=-1, keepdims=True) + EPS)
                      * (HEAD_DIM ** -0.5))
        k_rows.append(kh * lax.rsqrt(jnp.sum(kh * kh, axis=-1, keepdims=True) + EPS))
        v_rows.append(conv[:, 2 * MIX_HALF + h * HEAD_DIM:2 * MIX_HALF + (h + 1) * HEAD_DIM])
    f_rows = [f[:, h * HEAD_DIM:(h + 1) * HEAD_DIM] for h in range(N_HEADS)]
    qb_rows = [qb[:, h * HEAD_DIM:(h + 1) * HEAD_DIM] for h in range(N_HEADS)]

    stacked = jnp.concatenate(k_rows + q_rows + f_rows + qb_rows, axis=0)
    cols = stacked.T

    def col(item, h, b):
        i = (item * N_HEADS + h) * bt + b
        return cols[:, i:i + 1]

    for b in range(bt):
        for h in range(N_HEADS):
            cs = slice(h * HEAD_DIM, (h + 1) * HEAD_DIM)
            s = sd_ref[b, h]
            kc = col(0, h, b)
            eg = eg_all[b:b + 1, h:h + 1]
            beta = beta_all[b:b + 1, N_HEADS + h:N_HEADS + h + 1]
            sk = jnp.sum(s * kc, axis=0, keepdims=True)
            v_new = beta * (v_rows[h][b:b + 1, :] - eg * sk)
            s_new = s * eg + kc * v_new
            sdo_ref[layer, b, h] = s_new
            oa_scr[b:b + 1, cs] = jnp.sum(s_new * col(1, h, b), axis=0, keepdims=True)
            s = sh_ref[b, h]
            fc = col(2, h, b)
            vb = pb_ref[b:b + 1, 2 * MIX_HALF + h * HEAD_DIM:2 * MIX_HALF + (h + 1) * HEAD_DIM]
            s_new = s * fc + (1.0 - fc) * vb
            sho_ref[layer, b, h] = s_new
            ob_scr[b:b + 1, cs] = jnp.sum(s_new * col(3, h, b), axis=0, keepdims=True)

    for h in range(N_HEADS):
        cs = slice(h * HEAD_DIM, (h + 1) * HEAD_DIM)
        mix_ref[:, cs] = (_rms(oa_scr[:, cs], na_ref[...]) * _silu(z_ref[:, cs])).astype(mix_ref.dtype)
        gate = _sigmoid(pb_ref[:, 3 * MIX_HALF + h * HEAD_DIM:3 * MIX_HALF + (h + 1) * HEAD_DIM])
        mix_ref[:, MIX_HALF + h * HEAD_DIM:MIX_HALF + (h + 1) * HEAD_DIM] = (
            _rms(ob_scr[:, cs], nb_ref[...]) * gate).astype(mix_ref.dtype)


def _decode_call(proj, conv_state, s_delta, s_hgrn, prev_states, conv_w, a_log_rows, dt_bias_rows,
                 norm_a, norm_b, lb_logits, *, layer):
    batch = proj.shape[0]
    bt = 8
    kern = functools.partial(_decode_kernel, bt=bt, layer=layer)
    state_dims = (bt, N_HEADS, HEAD_DIM, HEAD_DIM)
    old_state_spec = pl.BlockSpec((None,) + state_dims, lambda i: (layer, i, 0, 0, 0))
    prev_state_spec = pl.BlockSpec((layer,) + state_dims, lambda i: (0, i, 0, 0, 0))
    new_state_spec = pl.BlockSpec((layer + 1,) + state_dims, lambda i: (0, i, 0, 0, 0))
    new_state_shape = jax.ShapeDtypeStruct((layer + 1, batch, N_HEADS, HEAD_DIM, HEAD_DIM), F32)
    return pl.pallas_call(
        kern,
        grid=(batch // bt,),
        in_specs=[
            pl.BlockSpec((bt, CONV_DIM), lambda i: (i, COL_QKV // CONV_DIM)),
            pl.BlockSpec((bt, MIX_HALF), lambda i: (i, COL_Z // MIX_HALF)),
            pl.BlockSpec((bt, 128), lambda i: (i, COL_AB // 128)),
            pl.BlockSpec((bt, 4 * MIX_HALF), lambda i: (i, COL_B // (4 * MIX_HALF))),
            pl.BlockSpec((None, bt, (CONV_W - 1) * CONV_DIM), lambda i: (layer, i, 0)),
            old_state_spec,
            old_state_spec,
            _layer_spec((CONV_W, CONV_DIM), layer),
            _layer_spec((1, 128), layer),
            _layer_spec((1, 128), layer),
            _layer_spec((1, HEAD_DIM), layer),
            _layer_spec((1, HEAD_DIM), layer),
            pl.BlockSpec((DEPTH, MIX_HALF), lambda i: (0, 0)),
        ] + [prev_state_spec] * len(prev_states),
        out_specs=[
            pl.BlockSpec((bt, 2 * MIX_HALF), lambda i: (i, 0)),
            pl.BlockSpec((bt, (CONV_W - 1) * CONV_DIM), lambda i: (i, 0)),
            new_state_spec,
            new_state_spec,
        ],
        out_shape=[
            jax.ShapeDtypeStruct((batch, 2 * MIX_HALF), BF16),
            jax.ShapeDtypeStruct((batch, (CONV_W - 1) * CONV_DIM), F32),
            new_state_shape,
            new_state_shape,
        ],
        scratch_shapes=[
            pltpu.VMEM((bt, MIX_HALF), F32),
            pltpu.VMEM((bt, MIX_HALF), F32),
        ],
        compiler_params=pltpu.CompilerParams(
            dimension_semantics=("arbitrary",), vmem_limit_bytes=VMEM_LIMIT),
        name="mixers_decode",
    )(proj, proj, proj, proj, conv_state, s_delta, s_hgrn, conv_w, a_log_rows, dt_bias_rows,
      norm_a, norm_b, lb_logits, *prev_states)


FF_CHUNK = 256


def _out_ffn_kernel(x_ref, ma_ref, mb_ref, g1_ref, sh2_ref, sc2_ref, g2_ref, wo_ref, n2_ref,
                    wg_ref, wu_ref, wd_ref, nf_ref, o_ref, acc_ref, *, final_norm, tiles_per_seq):
    gate1 = _mod_rows(g1_ref, tiles_per_seq)
    shift2 = _mod_rows(sh2_ref, tiles_per_seq)
    scale2 = _mod_rows(sc2_ref, tiles_per_seq)
    gate2 = _mod_rows(g2_ref, tiles_per_seq)
    mix = _mm(ma_ref[...], wo_ref[0:MIX_HALF, :]) + _mm(mb_ref[...], wo_ref[MIX_HALF:, :])
    x1 = x_ref[...] + gate1 * mix
    h = (_rms(x1, n2_ref[...]) * (1.0 + scale2) + shift2).astype(BF16)
    for c in range(D_FF // FF_CHUNK):
        fs = slice(c * FF_CHUNK, (c + 1) * FF_CHUNK)
        gate = jnp.dot(h, wg_ref[:, fs], preferred_element_type=F32)
        up = jnp.dot(h, wu_ref[:, fs], preferred_element_type=F32)
        part = _mm(_silu(gate) * up, wd_ref[fs, :])
        if c == 0:
            acc_ref[...] = part
        else:
            acc_ref[...] += part
    x2 = x1 + gate2 * acc_ref[...]
    if final_norm:
        x2 = _rms(x2, nf_ref[...])
    o_ref[...] = x2


def _out_ffn_call(x, mix_a, mix_b, mix_b_col, mods, w_out, norm2, w_gate, w_up, w_down, norm_f,
                  *, layer, n_prompt, per_token, seq_len, final_norm):
    rows = x.shape[0]
    tm = rows if per_token else 512
    tiles_per_seq = None if per_token else seq_len // tm
    resident = functools.partial(_layer_spec, layer=layer, pipeline_mode=pl.Buffered(1))
    mod = functools.partial(_mod_spec, mods, layer, n_prompt, per_token)
    kern = functools.partial(_out_ffn_kernel, final_norm=final_norm, tiles_per_seq=tiles_per_seq)
    return pl.pallas_call(
        kern,
        grid=(rows // tm,),
        in_specs=[
            pl.BlockSpec((tm, D_MODEL), lambda i: (i, 0)),
            pl.BlockSpec((tm, MIX_HALF), lambda i: (i, 0)),
            pl.BlockSpec((tm, MIX_HALF), lambda i: (i, mix_b_col)),
            mod(MOD_GATE1),
            mod(MOD_SHIFT2),
            mod(MOD_SCALE2),
            mod(MOD_GATE2),
            resident((D_MODEL, D_MODEL)),
            _layer_spec((1, D_MODEL), layer),
            resident((D_MODEL, D_FF)),
            resident((D_MODEL, D_FF)),
            resident((D_FF, D_MODEL)),
            pl.BlockSpec((1, D_MODEL), lambda i: (0, 0)),
        ],
        out_specs=pl.BlockSpec((tm, D_MODEL), lambda i: (i, 0)),
        out_shape=jax.ShapeDtypeStruct((rows, D_MODEL), F32),
        scratch_shapes=[pltpu.VMEM((tm, D_MODEL), F32)],
        compiler_params=pltpu.CompilerParams(
            dimension_semantics=("arbitrary",), vmem_limit_bytes=VMEM_LIMIT),
        name="out_ffn",
    )(x, mix_a, mix_b, mods, mods, mods, mods, w_out, norm2, w_gate, w_up, w_down, norm_f)


def _reorder_in_proj(w):
    qkvz = w[..., 0:COL_B]
    ab = w[..., COL_B:COL_B + 2 * N_HEADS]
    rest = w[..., COL_B + 2 * N_HEADS:]
    pad = jnp.zeros(w.shape[:-1] + (128 - 2 * N_HEADS,), w.dtype)
    return jnp.concatenate([qkvz, rest, ab, pad], axis=-1).astype(BF16)


def _lane_rows(v):
    return jnp.pad(v.astype(F32), ((0, 0), (0, 128 - v.shape[1])))[:, None, :]


def kernel(x_prompt, x_sample, c_prompt, c_sample, state_delta, state_conv, state_hgrn, w_ada, b_ada,
           norm1, w_in, conv_w, a_log, dt_bias, norm_a, norm_b, lb_logits, w_out, norm2, w_gate, w_up,
           w_down, norm_f):
    bp, seq_len, _ = x_prompt.shape
    bs = x_sample.shape[0]

    mods = _ada_call(jnp.concatenate([c_sample, c_prompt], axis=0), w_ada, b_ada)

    xp = x_prompt.reshape(bp * seq_len, D_MODEL)
    xs = x_sample.reshape(bs, D_MODEL)
    zero_buf = jnp.zeros((bp, 8, CONV_DIM), F32)
    zero_state = jnp.zeros((bp, N_HEADS, HEAD_DIM, HEAD_DIM), F32)

    w_in_b = _reorder_in_proj(w_in)
    w_out_b = w_out.astype(BF16)
    w_gate_b = w_gate.astype(BF16)
    w_up_b = w_up.astype(BF16)
    w_down_b = w_down.astype(BF16)
    a_log_rows = _lane_rows(a_log)
    dt_bias_rows = _lane_rows(dt_bias)
    norm1_r = norm1[:, None, :]
    norm2_r = norm2[:, None, :]
    norm_a_r = norm_a[:, None, :]
    norm_b_r = norm_b[:, None, :]
    norm_f_r = norm_f[None, :]
    conv_state = state_conv.reshape(DEPTH, bs, (CONV_W - 1) * CONV_DIM)

    delta_p, conv_p, hgrn_p, conv_s = [], [], [], []
    decode_states = ()
    for l in range(DEPTH):
        last = l == DEPTH - 1
        dense = dict(layer=l, n_prompt=bp)

        proj = _in_proj_call(xp, mods, norm1_r, w_in_b, per_token=False, seq_len=seq_len, **dense)
        mix_a, s_a, cv = _gdn_call(proj, zero_buf, zero_state, conv_w, a_log_rows, dt_bias_rows,
                                   norm_a_r, layer=l, batch=bp, seq_len=seq_len)
        mix_b, s_b = _hgrn_call(proj, lb_logits, norm_b_r, zero_state, layer=l, batch=bp,
                                seq_len=seq_len)
        xp = _out_ffn_call(xp, mix_a, mix_b, 0, mods, w_out_b, norm2_r, w_gate_b, w_up_b, w_down_b,
                           norm_f_r, per_token=False, seq_len=seq_len, final_norm=last, **dense)
        delta_p.append(s_a)
        conv_p.append(cv)
        hgrn_p.append(s_b)

        proj = _in_proj_call(xs, mods, norm1_r, w_in_b, per_token=True, seq_len=1, **dense)
        mix, cv, *decode_states = _decode_call(
            proj, conv_state, state_delta, state_hgrn, decode_states,
            conv_w, a_log_rows, dt_bias_rows, norm_a_r, norm_b_r, lb_logits, layer=l)
        xs = _out_ffn_call(xs, mix, mix, 1, mods, w_out_b, norm2_r, w_gate_b, w_up_b, w_down_b,
                           norm_f_r, per_token=True, seq_len=1, final_norm=last, **dense)
        conv_s.append(cv.reshape(bs, CONV_W - 1, CONV_DIM))

    delta_s, hgrn_s = decode_states
    return (xp.reshape(bp, seq_len, D_MODEL), xs.reshape(bs, 1, D_MODEL),
            jnp.stack(delta_p), jnp.stack(conv_p), jnp.stack(hgrn_p),
            delta_s, jnp.stack(conv_s), hgrn_s)
```

```python
import functools

import jax
import jax.numpy as jnp
from jax import lax
from jax.experimental import pallas as pl
from jax.experimental.pallas import tpu as pltpu

F32 = jnp.float32
BF16 = jnp.bfloat16

D_MODEL = 1024
DEPTH = 2
N_HEADS = 4
HEAD_DIM = 128
MIX_HALF = N_HEADS * HEAD_DIM
CONV_W = 4
CONV_DIM = 3 * MIX_HALF
D_FF = 2816
EPS = 1e-6
GDN_CHUNK = 128
GDN_SUB = 16
HGRN_CHUNK = 64

COL_QKV = 0
COL_Z = CONV_DIM
COL_B = COL_Z + MIX_HALF
COL_AB = COL_B + 4 * MIX_HALF
IN_COLS = COL_AB + 128
PROMPT_COL_G = COL_AB
PROMPT_COL_AB = PROMPT_COL_G + MIX_HALF
PROMPT_COLS = PROMPT_COL_AB + 128

MOD_SHIFT1, MOD_SCALE1, MOD_GATE1, MOD_SHIFT2, MOD_SCALE2, MOD_GATE2 = range(6)

VMEM_LIMIT = 56 * 1024 * 1024


def _sigmoid(x):
    return 1.0 / (1.0 + jnp.exp(-x))


def _silu(x):
    return x * _sigmoid(x)


def _softplus(x):
    return jnp.maximum(x, 0.0) + jnp.log1p(jnp.exp(-jnp.abs(x)))


def _mm(a, b):
    return jnp.dot(a.astype(BF16), b.astype(BF16), preferred_element_type=F32)


def _mm_nt(a, b):
    return lax.dot_general(a.astype(BF16), b.astype(BF16), (((1,), (1,)), ((), ())),
                           preferred_element_type=F32)


def _mm_tn(a, b):
    return lax.dot_general(a.astype(BF16), b.astype(BF16), (((0,), (0,)), ((), ())),
                           preferred_element_type=F32)


def _rms(x, w):
    return x * lax.rsqrt(jnp.mean(x * x, axis=-1, keepdims=True) + EPS) * w


def _chunk_cumsum(x, chunk):
    rows = lax.broadcasted_iota(jnp.int32, x.shape, 0) & (chunk - 1)
    s = 1
    while s < chunk:
        x = x + jnp.where(rows >= s, pltpu.roll(x, s, 0), 0.0)
        s *= 2
    return x


def _ada_kernel(c_ref, w_ref, b_ref, o_ref):
    a = _silu(c_ref[...])
    o_ref[...] = _mm(a, w_ref[...]) + b_ref[...]


def _ada_call(c_all, w_ada, b_ada):
    rows = c_all.shape[0]
    tn = 1024
    return pl.pallas_call(
        _ada_kernel,
        grid=(DEPTH, 6 * D_MODEL // tn),
        in_specs=[
            pl.BlockSpec((rows, D_MODEL), lambda l, j: (0, 0)),
            pl.BlockSpec((None, D_MODEL, tn), lambda l, j: (l, 0, j)),
            pl.BlockSpec((None, 1, tn), lambda l, j: (l, 0, j)),
        ],
        out_specs=pl.BlockSpec((None, rows, tn), lambda l, j: (l, 0, j)),
        out_shape=jax.ShapeDtypeStruct((DEPTH, rows, 6 * D_MODEL), F32),
        compiler_params=pltpu.CompilerParams(
            dimension_semantics=("arbitrary", "arbitrary"), vmem_limit_bytes=VMEM_LIMIT),
        name="ada_mod",
    )(c_all, w_ada, b_ada.reshape(DEPTH, 1, 6 * D_MODEL))


def _mod_spec(mods, layer, n_prompt, per_token, col):
    n_decode = mods.shape[1] - n_prompt
    if per_token:
        return pl.BlockSpec((None, n_decode, D_MODEL), lambda i: (layer, 0, col))
    return pl.BlockSpec((None, n_prompt, D_MODEL), lambda i: (layer, n_decode // n_prompt, col))


def _mod_rows(ref, tiles_per_seq):
    if tiles_per_seq is None:
        return ref[...]
    return ref[pl.ds(pl.program_id(0) // tiles_per_seq, 1), :]


def _layer_spec(shape, layer, **kwargs):
    zeros = (0,) * len(shape)
    return pl.BlockSpec((None,) + tuple(shape), lambda *_: (layer,) + zeros, **kwargs)


def _modulated_input(x_ref, scale_ref, shift_ref, nw_ref, tile, tiles_per_seq):
    if tiles_per_seq is None:
        scale, shift = scale_ref[...], shift_ref[...]
    else:
        seq = tile // tiles_per_seq
        scale, shift = scale_ref[pl.ds(seq, 1), :], shift_ref[pl.ds(seq, 1), :]
    return (_rms(x_ref[...], nw_ref[...]) * (1.0 + scale) + shift).astype(BF16)


def _in_proj_decode_kernel(x_ref, scale_ref, shift_ref, nw_ref, w_ref, o_ref):
    h = _modulated_input(x_ref, scale_ref, shift_ref, nw_ref, 0, None)
    o_ref[...] = jnp.dot(h, w_ref[...], preferred_element_type=F32)


def _in_proj_prompt_kernel(x_ref, scale_ref, shift_ref, nw_ref, w_ref, buf_ref, cw_ref, lbl_ref,
                           o_ref, cfin_ref, raw, h_scr, *, tiles_per_seq, layer):
    i = pl.program_id(0)
    tm = x_ref.shape[0]
    tile_in_seq = i % tiles_per_seq
    body = slice(8, 8 + tm)

    @pl.when(tile_in_seq == 0)
    def _():
        raw[0:8, :] = buf_ref[...]

    h_scr[...] = _modulated_input(x_ref, scale_ref, shift_ref, nw_ref, i, tiles_per_seq)
    lb_all = _forget_lower_bound(lbl_ref[...], layer)

    def project(cs):
        return jnp.dot(h_scr[...], w_ref[:, cs], preferred_element_type=F32)

    def conv_cols(cs):
        raw[body, cs] = project(cs)
        acc = raw[5:5 + tm, cs] * cw_ref[0:1, cs]
        for j in range(1, CONV_W):
            acc = acc + raw[5 + j:5 + j + tm, cs] * cw_ref[j:j + 1, cs]
        return _silu(acc)

    def l2_normalised(cs, scale):
        conv = conv_cols(cs)
        for off in range(0, cs.stop - cs.start, HEAD_DIM):
            v = conv[:, off:off + HEAD_DIM]
            o_ref[:, cs.start + off:cs.start + off + HEAD_DIM] = (
                v * lax.rsqrt(jnp.sum(v * v, axis=-1, keepdims=True) + EPS) * scale)

    def post_q(cs):
        l2_normalised(cs, HEAD_DIM ** -0.5)

    def post_k(cs):
        l2_normalised(cs, 1.0)

    def post_v(cs):
        o_ref[:, cs] = conv_cols(cs)

    def post_copy(cs):
        o_ref[:, cs] = project(cs)

    def post_silu(cs):
        o_ref[:, cs] = _silu(project(cs))

    def post_forget(cs):
        gate_cols = slice(cs.start - COL_B - MIX_HALF, cs.stop - COL_B - MIX_HALF)
        lb = lb_all[:, gate_cols]
        f = lb + (1.0 - lb) * _sigmoid(project(cs))
        o_ref[:, cs] = 1.0 - f
        o_ref[:, PROMPT_COL_G + gate_cols.start:PROMPT_COL_G + gate_cols.stop] = (
            _chunk_cumsum(jnp.log(f), HGRN_CHUNK))

    width = 2 * HEAD_DIM
    pieces = ([post_q] * 2 + [post_k] * 2 + [post_v] * 2 + [post_copy] * 2
              + [post_silu] * 2 + [post_forget] * 2 + [post_copy] * 4)
    for piece, post in enumerate(pieces):
        post(slice(piece * width, (piece + 1) * width))
    o_ref[:, PROMPT_COL_AB:PROMPT_COLS] = project(slice(COL_AB, IN_COLS))

    tail = raw[tm:tm + 8, :]
    raw[0:8, :] = tail

    @pl.when(tile_in_seq == tiles_per_seq - 1)
    def _():
        cfin_ref[...] = tail[8 - (CONV_W - 1):, :]


def _in_proj_call(x, mods, norm1, w_in, conv_buf, conv_w, lb_logits, *, layer, n_prompt, per_token,
                  seq_len):
    rows = x.shape[0]
    in_specs = [
        None,
        _mod_spec(mods, layer, n_prompt, per_token, MOD_SCALE1),
        _mod_spec(mods, layer, n_prompt, per_token, MOD_SHIFT1),
        _layer_spec((1, D_MODEL), layer),
        _layer_spec((D_MODEL, IN_COLS), layer, pipeline_mode=pl.Buffered(1)),
    ]
    params = pltpu.CompilerParams(dimension_semantics=("arbitrary",), vmem_limit_bytes=VMEM_LIMIT)
    if per_token:
        in_specs[0] = pl.BlockSpec((rows, D_MODEL), lambda i: (0, 0))
        return pl.pallas_call(
            _in_proj_decode_kernel,
            grid=(1,),
            in_specs=in_specs,
            out_specs=pl.BlockSpec((rows, IN_COLS), lambda i: (0, 0)),
            out_shape=jax.ShapeDtypeStruct((rows, IN_COLS), F32),
            compiler_params=params,
            name="in_proj_decode",
        )(x, mods, mods, norm1, w_in)

    tm = 512
    tiles_per_seq = seq_len // tm
    in_specs[0] = pl.BlockSpec((tm, D_MODEL), lambda i: (i, 0))
    in_specs += [
        pl.BlockSpec((None, 8, CONV_DIM), lambda i: (i // tiles_per_seq, 0, 0)),
        _layer_spec((CONV_W, CONV_DIM), layer),
        pl.BlockSpec((DEPTH, MIX_HALF), lambda i: (0, 0)),
    ]
    kern = functools.partial(_in_proj_prompt_kernel, tiles_per_seq=tiles_per_seq, layer=layer)
    return pl.pallas_call(
        kern,
        grid=(rows // tm,),
        in_specs=in_specs,
        out_specs=[
            pl.BlockSpec((tm, PROMPT_COLS), lambda i: (i, 0)),
            pl.BlockSpec((None, CONV_W - 1, CONV_DIM), lambda i: (i // tiles_per_seq, 0, 0)),
        ],
        out_shape=[
            jax.ShapeDtypeStruct((rows, PROMPT_COLS), F32),
            jax.ShapeDtypeStruct((n_prompt, CONV_W - 1, CONV_DIM), F32),
        ],
        scratch_shapes=[
            pltpu.VMEM((tm + 8, CONV_DIM), F32),
            pltpu.VMEM((tm, D_MODEL), BF16),
        ],
        compiler_params=params,
        name="in_proj_prompt",
    )(x, mods, mods, norm1, w_in, conv_buf, conv_w, lb_logits)


def _unit_lower_inverse(mats, sub_mask, eye):
    d = [jnp.where(sub_mask, a, 0.0) for a in mats]
    n = [a - x for a, x in zip(mats, d)]
    d2 = [_mm(x, x) for x in d]
    d4 = [_mm(x, x) for x in d2]
    d8 = [_mm(x, x) for x in d4]
    p = [eye - x + x2 - _mm(x, x2) for x, x2 in zip(d, d2)]
    p = [x + _mm(x, y) for x, y in zip(p, d4)]
    dinv = [x + _mm(x, y) for x, y in zip(p, d8)]
    e = [_mm(x, y) for x, y in zip(dinv, n)]
    e2 = [_mm(x, x) for x in e]
    e4 = [_mm(x, x) for x in e2]
    f = [x2 - x - _mm(x, x2) for x, x2 in zip(e, e2)]
    q = [x + y + _mm(x, y) for x, y in zip(f, e4)]
    return [x + _mm(y, x) for x, y in zip(dinv, q)]


def _gdn_kernel(qkv_ref, z_ref, ab_ref, s0_ref, alog_ref, dtb_ref, na_ref,
                mix_ref, sfin_ref, s_scr, *, tile):
    t = pl.program_id(1)

    @pl.when(t == 0)
    def _():
        s_scr[...] = s0_ref[...]

    ab = ab_ref[...]
    g_all = -jnp.exp(alog_ref[...]) * _softplus(ab + dtb_ref[...])
    sig_all = _sigmoid(ab)
    g_cum = _chunk_cumsum(g_all, GDN_CHUNK)
    g_cum_t = g_cum.T

    ri = lax.broadcasted_iota(jnp.int32, (GDN_CHUNK, GDN_CHUNK), 0)
    ci = lax.broadcasted_iota(jnp.int32, (GDN_CHUNK, GDN_CHUNK), 1)
    m_incl = ri >= ci
    m_strict = ri > ci
    sub_mask = jnp.bitwise_xor(ri, ci) < GDN_SUB
    eye = jnp.where(ri == ci, 1.0, 0.0).astype(F32)
    n_chunks = tile // GDN_CHUNK
    heads = range(N_HEADS)
    chains = [(h, c) for h in heads for c in range(n_chunks)]

    def rows(c):
        return slice(c * GDN_CHUNK, (c + 1) * GDN_CHUNK)

    qn, kn, kb, gc, eg, rhs = [], [], [], [], [], []
    for h in heads:
        qn.append(qkv_ref[:, h * HEAD_DIM:(h + 1) * HEAD_DIM])
        kn.append(qkv_ref[:, MIX_HALF + h * HEAD_DIM:MIX_HALF + (h + 1) * HEAD_DIM])
        vh = qkv_ref[:, 2 * MIX_HALF + h * HEAD_DIM:2 * MIX_HALF + (h + 1) * HEAD_DIM]
        beta = sig_all[:, N_HEADS + h:N_HEADS + h + 1]
        gc.append(g_cum[:, h:h + 1])
        eg.append(jnp.exp(gc[h]))
        kb.append(kn[h] * beta)
        rhs.append(jnp.concatenate([vh * beta, kb[h] * eg[h]], axis=1))

    dec = {}
    for h, c in chains:
        gr = g_cum_t[h:h + 1, rows(c)]
        dec[h, c] = jnp.exp(jnp.where(m_incl, gc[h][rows(c)] - gr, -jnp.inf))
    st = {hc: _mm_nt(jnp.concatenate([kb[hc[0]][rows(hc[1])], qn[hc[0]][rows(hc[1])]], axis=0),
                     kn[hc[0]][rows(hc[1])]) for hc in chains}
    a = [st[hc][:GDN_CHUNK] * jnp.where(m_strict, dec[hc], 0.0) for hc in chains]
    qk = {hc: st[hc][GDN_CHUNK:] * dec[hc] for hc in chains}
    tinv = dict(zip(chains, _unit_lower_inverse(a, sub_mask, eye)))
    uw = {hc: _mm(tinv[hc], rhs[hc[0]][rows(hc[1])]) for hc in chains}

    s = [s_scr[h] for h in heads]
    for c in range(n_chunks):
        rs = rows(c)
        r = [_mm(jnp.concatenate([uw[h, c][:, HEAD_DIM:], qn[h][rs] * eg[h][rs]], axis=0), s[h])
             for h in heads]
        v_new = [uw[h, c][:, :HEAD_DIM] - r[h][:GDN_CHUNK] for h in heads]
        for h in heads:
            g_last = gc[h][(c + 1) * GDN_CHUNK - 1:(c + 1) * GDN_CHUNK, :]
            kd = kn[h][rs] * jnp.exp(g_last - gc[h][rs])
            s[h] = s[h] * jnp.exp(g_last) + _mm_tn(kd, v_new[h])
        for h in heads:
            cs = slice(h * HEAD_DIM, (h + 1) * HEAD_DIM)
            o = r[h][GDN_CHUNK:] + _mm(qk[h, c], v_new[h])
            mix_ref[rs, cs] = (_rms(o, na_ref[...]) * _silu(z_ref[rs, cs])).astype(mix_ref.dtype)
    for h in heads:
        s_scr[h] = s[h]

    @pl.when(t == pl.num_programs(1) - 1)
    def _():
        sfin_ref[...] = s_scr[...]


def _gdn_call(proj, s0, a_log_rows, dt_bias_rows, norm_a, *, layer, batch, seq_len):
    tile = 256
    nt = seq_len // tile
    kern = functools.partial(_gdn_kernel, tile=tile)
    return pl.pallas_call(
        kern,
        grid=(batch, nt),
        in_specs=[
            pl.BlockSpec((tile, CONV_DIM), lambda b, t: (b * nt + t, COL_QKV // CONV_DIM)),
            pl.BlockSpec((tile, MIX_HALF), lambda b, t: (b * nt + t, COL_Z // MIX_HALF)),
            pl.BlockSpec((tile, 128), lambda b, t: (b * nt + t, PROMPT_COL_AB // 128)),
            pl.BlockSpec((None, N_HEADS, HEAD_DIM, HEAD_DIM), lambda b, t: (b, 0, 0, 0)),
            _layer_spec((1, 128), layer),
            _layer_spec((1, 128), layer),
            _layer_spec((1, HEAD_DIM), layer),
        ],
        out_specs=[
            pl.BlockSpec((tile, MIX_HALF), lambda b, t: (b * nt + t, 0)),
            pl.BlockSpec((None, N_HEADS, HEAD_DIM, HEAD_DIM), lambda b, t: (b, 0, 0, 0)),
        ],
        out_shape=[
            jax.ShapeDtypeStruct((batch * seq_len, MIX_HALF), BF16),
            jax.ShapeDtypeStruct((batch, N_HEADS, HEAD_DIM, HEAD_DIM), F32),
        ],
        scratch_shapes=[pltpu.VMEM((N_HEADS, HEAD_DIM, HEAD_DIM), F32)],
        compiler_params=pltpu.CompilerParams(
            dimension_semantics=("arbitrary", "arbitrary"), vmem_limit_bytes=VMEM_LIMIT),
        name="gdn_prompt",
    )(proj, proj, proj, s0, a_log_rows, dt_bias_rows, norm_a)


def _forget_lower_bound(lbl, layer):
    m = jnp.max(lbl, axis=0, keepdims=True)
    e = jnp.exp(lbl - m)
    sm = e / jnp.sum(e, axis=0, keepdims=True)
    cs = sm[0:1, :]
    for i in range(1, layer + 1):
        cs = cs + sm[i:i + 1, :]
    return cs - sm[0:1, :]


def _midpoint_rows(g, s, row_id):
    n = g.shape[0]
    if s >= 4:
        parts = [jnp.broadcast_to(g[b + s:b + s + 1, :], (2 * s, g.shape[1]))
                 for b in range(0, n, 2 * s)]
        return jnp.concatenate(parts, axis=0)
    nxt = pltpu.roll(g, n - 1, 0)
    if s == 1:
        return jnp.where((row_id & 1) == 1, g, nxt)
    pos = row_id & 3
    return jnp.where(pos == 0, pltpu.roll(g, n - 2, 0),
                     jnp.where(pos == 1, nxt, jnp.where(pos == 2, g, pltpu.roll(g, 1, 0))))


def _hgrn_kernel(p_ref, g_ref, nb_ref, s0_ref, mix_ref, sfin_ref, st_scr, *, tile):
    t = pl.program_id(1)

    @pl.when(t == 0)
    def _():
        for h in range(N_HEADS):
            st_scr[h] = s0_ref[h].T

    heads = range(N_HEADS)
    n_chunks = tile // HGRN_CHUNK
    chains = [(h, c) for h in heads for c in range(n_chunks)]

    def rows(c):
        return slice(c * HGRN_CHUNK, (c + 1) * HGRN_CHUNK)

    row_id = lax.broadcasted_iota(jnp.int32, (tile, HEAD_DIM), 0)
    ri = lax.broadcasted_iota(jnp.int32, (HGRN_CHUNK, HGRN_CHUNK), 0)
    ci = lax.broadcasted_iota(jnp.int32, (HGRN_CHUNK, HGRN_CHUNK), 1)
    dist = jnp.bitwise_xor(ri, ci)
    lower = ri > ci

    q = [p_ref[:, h * HEAD_DIM:(h + 1) * HEAD_DIM] for h in heads]
    k = [p_ref[:, MIX_HALF + h * HEAD_DIM:MIX_HALF + (h + 1) * HEAD_DIM] for h in heads]
    g = [g_ref[:, h * HEAD_DIM:(h + 1) * HEAD_DIM] for h in heads]

    a = {(h, c): jnp.where(ri == ci, _mm_nt(q[h][rows(c)], k[h][rows(c)]), 0.0) for h, c in chains}
    s = HGRN_CHUNK // 2
    while s >= 1:
        upper = (row_id & s) != 0
        level = lower & (dist >= s) & (dist < 2 * s)
        x = []
        for h in heads:
            d = g[h] - _midpoint_rows(g[h], s, row_id)
            x.append(jnp.where(upper, q[h], k[h]) * jnp.exp(jnp.where(upper, d, -d)))
        for h, c in chains:
            xc = x[h][rows(c)]
            a[h, c] = a[h, c] + jnp.where(level, _mm_nt(xc, xc), 0.0)
        s //= 2

    st = [st_scr[h] for h in heads]
    for c in range(n_chunks):
        rs = rows(c)
        for h in heads:
            cs = slice(h * HEAD_DIM, (h + 1) * HEAD_DIM)
            gc = g[h][rs]
            vc = p_ref[rs, 2 * MIX_HALF + h * HEAD_DIM:2 * MIX_HALF + (h + 1) * HEAD_DIM]
            o = _mm(a[h, c], vc) + _mm_nt(q[h][rs] * jnp.exp(gc), st[h])
            g_last = gc[HGRN_CHUNK - 1:HGRN_CHUNK, :]
            st[h] = st[h] * jnp.exp(g_last) + _mm_tn(vc, k[h][rs] * jnp.exp(g_last - gc))
            gate = _sigmoid(p_ref[rs, 3 * MIX_HALF + h * HEAD_DIM:3 * MIX_HALF + (h + 1) * HEAD_DIM])
            mix_ref[rs, cs] = (_rms(o, nb_ref[...]) * gate).astype(mix_ref.dtype)
    for h in heads:
        st_scr[h] = st[h]

    @pl.when(t == pl.num_programs(1) - 1)
    def _():
        for h in heads:
            sfin_ref[h] = st_scr[h].T


def _hgrn_call(proj, norm_b, s0, *, layer, batch, seq_len):
    tile = 256
    nt = seq_len // tile
    kern = functools.partial(_hgrn_kernel, tile=tile)
    return pl.pallas_call(
        kern,
        grid=(batch, nt),
        in_specs=[
            pl.BlockSpec((tile, 4 * MIX_HALF), lambda b, t: (b * nt + t, COL_B // (4 * MIX_HALF))),
            pl.BlockSpec((tile, MIX_HALF), lambda b, t: (b * nt + t, PROMPT_COL_G // MIX_HALF)),
            _layer_spec((1, HEAD_DIM), layer),
            pl.BlockSpec((None, N_HEADS, HEAD_DIM, HEAD_DIM), lambda b, t: (b, 0, 0, 0)),
        ],
        out_specs=[
            pl.BlockSpec((tile, MIX_HALF), lambda b, t: (b * nt + t, 0)),
            pl.BlockSpec((None, N_HEADS, HEAD_DIM, HEAD_DIM), lambda b, t: (b, 0, 0, 0)),
        ],
        out_shape=[
            jax.ShapeDtypeStruct((batch * seq_len, MIX_HALF), BF16),
            jax.ShapeDtypeStruct((batch, N_HEADS, HEAD_DIM, HEAD_DIM), F32),
        ],
        scratch_shapes=[pltpu.VMEM((N_HEADS, HEAD_DIM, HEAD_DIM), F32)],
        compiler_params=pltpu.CompilerParams(
            dimension_semantics=("arbitrary", "arbitrary"), vmem_limit_bytes=VMEM_LIMIT),
        name="hgrn_prompt",
    )(proj, proj, norm_b, s0)


def _decode_kernel(*refs, bt, layer):
    (qkv_ref, z_ref, ab_ref, pb_ref, conv_ref, sd_ref, sh_ref,
     cw_ref, alog_ref, dtb_ref, na_ref, nb_ref, lbl_ref) = refs[:13]
    mix_ref, convo_ref, sdo_ref, sho_ref, oa_scr, ob_scr = refs[-6:]
    if layer > 0:
        prev_sd_ref, prev_sh_ref = refs[13:15]
        sdo_ref[0:layer] = prev_sd_ref[...]
        sho_ref[0:layer] = prev_sh_ref[...]
    cw = cw_ref[...]
    u = qkv_ref[...]
    b0 = conv_ref[:, 0:CONV_DIM]
    b1 = conv_ref[:, CONV_DIM:2 * CONV_DIM]
    b2 = conv_ref[:, 2 * CONV_DIM:3 * CONV_DIM]
    acc = b0 * cw[0:1, :]
    acc = acc + b1 * cw[1:2, :]
    acc = acc + b2 * cw[2:3, :]
    acc = acc + u * cw[3:4, :]
    conv = _silu(acc)
    convo_ref[:, 0:CONV_DIM] = b1
    convo_ref[:, CONV_DIM:2 * CONV_DIM] = b2
    convo_ref[:, 2 * CONV_DIM:3 * CONV_DIM] = u

    ab = ab_ref[...]
    eg_all = jnp.exp(-jnp.exp(alog_ref[...]) * _softplus(ab + dtb_ref[...]))
    beta_all = _sigmoid(ab)

    lb = _forget_lower_bound(lbl_ref[...], layer)
    f = lb + (1.0 - lb) * _sigmoid(pb_ref[:, MIX_HALF:2 * MIX_HALF])
    qb = _silu(pb_ref[:, 0:MIX_HALF])

    k_rows, q_rows, v_rows = [], [], []
    for h in range(N_HEADS):
        cs = slice(h * HEAD_DIM, (h + 1) * HEAD_DIM)
        qh = conv[:, cs]
        kh = conv[:, MIX_HALF + h * HEAD_DIM:MIX_HALF + (h + 1) * HEAD_DIM]
        q_rows.append(qh * lax.rsqrt(jnp.sum(qh * qh, axis=-1, keepdims=True) + EPS)
                      * (HEAD_DIM ** -0.5))
        k_rows.append(kh * lax.rsqrt(jnp.sum(kh * kh, axis=-1, keepdims=True) + EPS))
        v_rows.append(conv[:, 2 * MIX_HALF + h * HEAD_DIM:2 * MIX_HALF + (h + 1) * HEAD_DIM])
    f_rows = [f[:, h * HEAD_DIM:(h + 1) * HEAD_DIM] for h in range(N_HEADS)]
    qb_rows = [qb[:, h * HEAD_DIM:(h + 1) * HEAD_DIM] for h in range(N_HEADS)]

    stacked = jnp.concatenate(k_rows + q_rows + f_rows + qb_rows, axis=0)
    cols = stacked.T

    def col(item, h, b):
        i = (item * N_HEADS + h) * bt + b
        return cols[:, i:i + 1]

    for b in range(bt):
        for h in range(N_HEADS):
            cs = slice(h * HEAD_DIM, (h + 1) * HEAD_DIM)
            s = sd_ref[b, h]
            kc = col(0, h, b)
            eg = eg_all[b:b + 1, h:h + 1]
            beta = beta_all[b:b + 1, N_HEADS + h:N_HEADS + h + 1]
            sk = jnp.sum(s * kc, axis=0, keepdims=True)
            v_new = beta * (v_rows[h][b:b + 1, :] - eg * sk)
            s_new = s * eg + kc * v_new
            sdo_ref[layer, b, h] = s_new
            oa_scr[b:b + 1, cs] = jnp.sum(s_new * col(1, h, b), axis=0, keepdims=True)
            s = sh_ref[b, h]
            fc = col(2, h, b)
            vb = pb_ref[b:b + 1, 2 * MIX_HALF + h * HEAD_DIM:2 * MIX_HALF + (h + 1) * HEAD_DIM]
            s_new = s * fc + (1.0 - fc) * vb
            sho_ref[layer, b, h] = s_new
            ob_scr[b:b + 1, cs] = jnp.sum(s_new * col(3, h, b), axis=0, keepdims=True)

    for h in range(N_HEADS):
        cs = slice(h * HEAD_DIM, (h + 1) * HEAD_DIM)
        mix_ref[:, cs] = (_rms(oa_scr[:, cs], na_ref[...]) * _silu(z_ref[:, cs])).astype(mix_ref.dtype)
        gate = _sigmoid(pb_ref[:, 3 * MIX_HALF + h * HEAD_DIM:3 * MIX_HALF + (h + 1) * HEAD_DIM])
        mix_ref[:, MIX_HALF + h * HEAD_DIM:MIX_HALF + (h + 1) * HEAD_DIM] = (
            _rms(ob_scr[:, cs], nb_ref[...]) * gate).astype(mix_ref.dtype)


def _decode_call(proj, conv_state, s_delta, s_hgrn, prev_states, conv_w, a_log_rows, dt_bias_rows,
                 norm_a, norm_b, lb_logits, *, layer):
    batch = proj.shape[0]
    bt = 8
    kern = functools.partial(_decode_kernel, bt=bt, layer=layer)
    state_dims = (bt, N_HEADS, HEAD_DIM, HEAD_DIM)
    old_state_spec = pl.BlockSpec((None,) + state_dims, lambda i: (layer, i, 0, 0, 0))
    prev_state_spec = pl.BlockSpec((layer,) + state_dims, lambda i: (0, i, 0, 0, 0))
    new_state_spec = pl.BlockSpec((layer + 1,) + state_dims, lambda i: (0, i, 0, 0, 0))
    new_state_shape = jax.ShapeDtypeStruct((layer + 1, batch, N_HEADS, HEAD_DIM, HEAD_DIM), F32)
    return pl.pallas_call(
        kern,
        grid=(batch // bt,),
        in_specs=[
            pl.BlockSpec((bt, CONV_DIM), lambda i: (i, COL_QKV // CONV_DIM)),
            pl.BlockSpec((bt, MIX_HALF), lambda i: (i, COL_Z // MIX_HALF)),
            pl.BlockSpec((bt, 128), lambda i: (i, COL_AB // 128)),
            pl.BlockSpec((bt, 4 * MIX_HALF), lambda i: (i, COL_B // (4 * MIX_HALF))),
            pl.BlockSpec((None, bt, (CONV_W - 1) * CONV_DIM), lambda i: (layer, i, 0)),
            old_state_spec,
            old_state_spec,
            _layer_spec((CONV_W, CONV_DIM), layer),
            _layer_spec((1, 128), layer),
            _layer_spec((1, 128), layer),
            _layer_spec((1, HEAD_DIM), layer),
            _layer_spec((1, HEAD_DIM), layer),
            pl.BlockSpec((DEPTH, MIX_HALF), lambda i: (0, 0)),
        ] + [prev_state_spec] * len(prev_states),
        out_specs=[
            pl.BlockSpec((bt, 2 * MIX_HALF), lambda i: (i, 0)),
            pl.BlockSpec((bt, (CONV_W - 1) * CONV_DIM), lambda i: (i, 0)),
            new_state_spec,
            new_state_spec,
        ],
        out_shape=[
            jax.ShapeDtypeStruct((batch, 2 * MIX_HALF), BF16),
            jax.ShapeDtypeStruct((batch, (CONV_W - 1) * CONV_DIM), F32),
            new_state_shape,
            new_state_shape,
        ],
        scratch_shapes=[
            pltpu.VMEM((bt, MIX_HALF), F32),
            pltpu.VMEM((bt, MIX_HALF), F32),
        ],
        compiler_params=pltpu.CompilerParams(
            dimension_semantics=("arbitrary",), vmem_limit_bytes=VMEM_LIMIT),
        name="mixers_decode",
    )(proj, proj, proj, proj, conv_state, s_delta, s_hgrn, conv_w, a_log_rows, dt_bias_rows,
      norm_a, norm_b, lb_logits, *prev_states)


FF_CHUNK = 256


def _out_ffn_kernel(x_ref, ma_ref, mb_ref, g1_ref, sh2_ref, sc2_ref, g2_ref, wo_ref, n2_ref,
                    wg_ref, wu_ref, wd_ref, nf_ref, o_ref, acc_ref, *, final_norm, tiles_per_seq):
    gate1 = _mod_rows(g1_ref, tiles_per_seq)
    shift2 = _mod_rows(sh2_ref, tiles_per_seq)
    scale2 = _mod_rows(sc2_ref, tiles_per_seq)
    gate2 = _mod_rows(g2_ref, tiles_per_seq)
    mix = _mm(ma_ref[...], wo_ref[0:MIX_HALF, :]) + _mm(mb_ref[...], wo_ref[MIX_HALF:, :])
    x1 = x_ref[...] + gate1 * mix
    h = (_rms(x1, n2_ref[...]) * (1.0 + scale2) + shift2).astype(BF16)
    for c in range(D_FF // FF_CHUNK):
        fs = slice(c * FF_CHUNK, (c + 1) * FF_CHUNK)
        gate = jnp.dot(h, wg_ref[:, fs], preferred_element_type=F32)
        up = jnp.dot(h, wu_ref[:, fs], preferred_element_type=F32)
        part = _mm(_silu(gate) * up, wd_ref[fs, :])
        if c == 0:
            acc_ref[...] = part
        else:
            acc_ref[...] += part
    x2 = x1 + gate2 * acc_ref[...]
    if final_norm:
        x2 = _rms(x2, nf_ref[...])
    o_ref[...] = x2


def _out_ffn_call(x, mix_a, mix_b, mix_b_col, mods, w_out, norm2, w_gate, w_up, w_down, norm_f,
                  *, layer, n_prompt, per_token, seq_len, final_norm):
    rows = x.shape[0]
    tm = rows if per_token else 512
    tiles_per_seq = None if per_token else seq_len // tm
    resident = functools.partial(_layer_spec, layer=layer, pipeline_mode=pl.Buffered(1))
    mod = functools.partial(_mod_spec, mods, layer, n_prompt, per_token)
    kern = functools.partial(_out_ffn_kernel, final_norm=final_norm, tiles_per_seq=tiles_per_seq)
    return pl.pallas_call(
        kern,
        grid=(rows // tm,),
        in_specs=[
            pl.BlockSpec((tm, D_MODEL), lambda i: (i, 0)),
            pl.BlockSpec((tm, MIX_HALF), lambda i: (i, 0)),
            pl.BlockSpec((tm, MIX_HALF), lambda i: (i, mix_b_col)),
            mod(MOD_GATE1),
            mod(MOD_SHIFT2),
            mod(MOD_SCALE2),
            mod(MOD_GATE2),
            resident((D_MODEL, D_MODEL)),
            _layer_spec((1, D_MODEL), layer),
            resident((D_MODEL, D_FF)),
            resident((D_MODEL, D_FF)),
            resident((D_FF, D_MODEL)),
            pl.BlockSpec((1, D_MODEL), lambda i: (0, 0)),
        ],
        out_specs=pl.BlockSpec((tm, D_MODEL), lambda i: (i, 0)),
        out_shape=jax.ShapeDtypeStruct((rows, D_MODEL), F32),
        scratch_shapes=[pltpu.VMEM((tm, D_MODEL), F32)],
        compiler_params=pltpu.CompilerParams(
            dimension_semantics=("arbitrary",), vmem_limit_bytes=VMEM_LIMIT),
        name="out_ffn",
    )(x, mix_a, mix_b, mods, mods, mods, mods, w_out, norm2, w_gate, w_up, w_down, norm_f)


def _reorder_in_proj(w):
    qkvz = w[..., 0:COL_B]
    ab = w[..., COL_B:COL_B + 2 * N_HEADS]
    rest = w[..., COL_B + 2 * N_HEADS:]
    pad = jnp.zeros(w.shape[:-1] + (128 - 2 * N_HEADS,), w.dtype)
    return jnp.concatenate([qkvz, rest, ab, pad], axis=-1).astype(BF16)


def _lane_rows(v):
    return jnp.pad(v.astype(F32), ((0, 0), (0, 128 - v.shape[1])))[:, None, :]


def kernel(x_prompt, x_sample, c_prompt, c_sample, state_delta, state_conv, state_hgrn, w_ada, b_ada,
           norm1, w_in, conv_w, a_log, dt_bias, norm_a, norm_b, lb_logits, w_out, norm2, w_gate, w_up,
           w_down, norm_f):
    bp, seq_len, _ = x_prompt.shape
    bs = x_sample.shape[0]

    mods = _ada_call(jnp.concatenate([c_sample, c_prompt], axis=0), w_ada, b_ada)

    xp = x_prompt.reshape(bp * seq_len, D_MODEL)
    xs = x_sample.reshape(bs, D_MODEL)
    zero_buf = jnp.zeros((bp, 8, CONV_DIM), F32)
    zero_state = jnp.zeros((bp, N_HEADS, HEAD_DIM, HEAD_DIM), F32)

    w_in_b = _reorder_in_proj(w_in)
    w_out_b = w_out.astype(BF16)
    w_gate_b = w_gate.astype(BF16)
    w_up_b = w_up.astype(BF16)
    w_down_b = w_down.astype(BF16)
    a_log_rows = _lane_rows(a_log)
    dt_bias_rows = _lane_rows(dt_bias)
    norm1_r = norm1[:, None, :]
    norm2_r = norm2[:, None, :]
    norm_a_r = norm_a[:, None, :]
    norm_b_r = norm_b[:, None, :]
    norm_f_r = norm_f[None, :]
    conv_state = state_conv.reshape(DEPTH, bs, (CONV_W - 1) * CONV_DIM)

    delta_p, conv_p, hgrn_p, conv_s = [], [], [], []
    decode_states = ()
    for l in range(DEPTH):
        last = l == DEPTH - 1
        dense = dict(layer=l, n_prompt=bp)

        proj, cv = _in_proj_call(xp, mods, norm1_r, w_in_b, zero_buf, conv_w, lb_logits,
                                 per_token=False, seq_len=seq_len, **dense)
        mix_a, s_a = _gdn_call(proj, zero_state, a_log_rows, dt_bias_rows, norm_a_r,
                               layer=l, batch=bp, seq_len=seq_len)
        mix_b, s_b = _hgrn_call(proj, norm_b_r, zero_state, layer=l, batch=bp, seq_len=seq_len)
        xp = _out_ffn_call(xp, mix_a, mix_b, 0, mods, w_out_b, norm2_r, w_gate_b, w_up_b, w_down_b,
                           norm_f_r, per_token=False, seq_len=seq_len, final_norm=last, **dense)
        delta_p.append(s_a)
        conv_p.append(cv)
        hgrn_p.append(s_b)

        proj = _in_proj_call(xs, mods, norm1_r, w_in_b, None, None, None,
                             per_token=True, seq_len=1, **dense)
        mix, cv, *decode_states = _decode_call(
            proj, conv_state, state_delta, state_hgrn, decode_states,
            conv_w, a_log_rows, dt_bias_rows, norm_a_r, norm_b_r, lb_logits, layer=l)
        xs = _out_ffn_call(xs, mix, mix, 1, mods, w_out_b, norm2_r, w_gate_b, w_up_b, w_down_b,
                           norm_f_r, per_token=True, seq_len=1, final_norm=last, **dense)
        conv_s.append(cv.reshape(bs, CONV_W - 1, CONV_DIM))

    delta_s, hgrn_s = decode_states
    return (xp.reshape(bp, seq_len, D_MODEL), xs.reshape(bs, 1, D_MODEL),
            jnp.stack(delta_p), jnp.stack(conv_p), jnp.stack(hgrn_p),
            delta_s, jnp.stack(conv_s), hgrn_s)
```

```python
import functools

import jax
import jax.numpy as jnp
from jax import lax
from jax.experimental import pallas as pl
from jax.experimental.pallas import tpu as pltpu

F32 = jnp.float32
BF16 = jnp.bfloat16

D_MODEL = 1024
DEPTH = 2
N_HEADS = 4
HEAD_DIM = 128
MIX_HALF = N_HEADS * HEAD_DIM
CONV_W = 4
CONV_DIM = 3 * MIX_HALF
D_FF = 2816
EPS = 1e-6
GDN_CHUNK = 128
GDN_SUB = 16
HGRN_CHUNK = 64

COL_QKV = 0
COL_Z = CONV_DIM
COL_B = COL_Z + MIX_HALF
COL_AB = COL_B + 4 * MIX_HALF
IN_COLS = COL_AB + 128
PROMPT_COL_G = COL_AB
PROMPT_COL_AB = PROMPT_COL_G + MIX_HALF
PROMPT_COLS = PROMPT_COL_AB + 128

MOD_SHIFT1, MOD_SCALE1, MOD_GATE1, MOD_SHIFT2, MOD_SCALE2, MOD_GATE2 = range(6)

VMEM_LIMIT = 56 * 1024 * 1024


def _sigmoid(x):
    return 1.0 / (1.0 + jnp.exp(-x))


def _silu(x):
    return x * _sigmoid(x)


def _softplus(x):
    return jnp.maximum(x, 0.0) + jnp.log1p(jnp.exp(-jnp.abs(x)))


def _mm(a, b):
    return jnp.dot(a.astype(BF16), b.astype(BF16), preferred_element_type=F32)


def _mm_nt(a, b):
    return lax.dot_general(a.astype(BF16), b.astype(BF16), (((1,), (1,)), ((), ())),
                           preferred_element_type=F32)


def _mm_tn(a, b):
    return lax.dot_general(a.astype(BF16), b.astype(BF16), (((0,), (0,)), ((), ())),
                           preferred_element_type=F32)


def _rms(x, w):
    return x * lax.rsqrt(jnp.mean(x * x, axis=-1, keepdims=True) + EPS) * w


def _chunk_cumsum(x, chunk):
    rows = lax.broadcasted_iota(jnp.int32, x.shape, 0) & (chunk - 1)
    s = 1
    while s < chunk:
        x = x + jnp.where(rows >= s, pltpu.roll(x, s, 0), 0.0)
        s *= 2
    return x


def _ada_kernel(c_ref, w_ref, b_ref, o_ref):
    a = _silu(c_ref[...])
    o_ref[...] = _mm(a, w_ref[...]) + b_ref[...]


def _ada_call(c_all, w_ada, b_ada):
    rows = c_all.shape[0]
    tn = 1024
    return pl.pallas_call(
        _ada_kernel,
        grid=(DEPTH, 6 * D_MODEL // tn),
        in_specs=[
            pl.BlockSpec((rows, D_MODEL), lambda l, j: (0, 0)),
            pl.BlockSpec((None, D_MODEL, tn), lambda l, j: (l, 0, j)),
            pl.BlockSpec((None, 1, tn), lambda l, j: (l, 0, j)),
        ],
        out_specs=pl.BlockSpec((None, rows, tn), lambda l, j: (l, 0, j)),
        out_shape=jax.ShapeDtypeStruct((DEPTH, rows, 6 * D_MODEL), F32),
        compiler_params=pltpu.CompilerParams(
            dimension_semantics=("arbitrary", "arbitrary"), vmem_limit_bytes=VMEM_LIMIT),
        name="ada_mod",
    )(c_all, w_ada, b_ada.reshape(DEPTH, 1, 6 * D_MODEL))


def _mod_spec(mods, layer, n_prompt, per_token, col):
    n_decode = mods.shape[1] - n_prompt
    if per_token:
        return pl.BlockSpec((None, n_decode, D_MODEL), lambda i: (layer, 0, col))
    return pl.BlockSpec((None, n_prompt, D_MODEL), lambda i: (layer, n_decode // n_prompt, col))


def _mod_rows(ref, tiles_per_seq):
    if tiles_per_seq is None:
        return ref[...]
    return ref[pl.ds(pl.program_id(0) // tiles_per_seq, 1), :]


def _layer_spec(shape, layer, **kwargs):
    zeros = (0,) * len(shape)
    return pl.BlockSpec((None,) + tuple(shape), lambda *_: (layer,) + zeros, **kwargs)


def _modulated_input(x_ref, scale_ref, shift_ref, nw_ref, tile, tiles_per_seq):
    if tiles_per_seq is None:
        scale, shift = scale_ref[...], shift_ref[...]
    else:
        seq = tile // tiles_per_seq
        scale, shift = scale_ref[pl.ds(seq, 1), :], shift_ref[pl.ds(seq, 1), :]
    return (_rms(x_ref[...], nw_ref[...]) * (1.0 + scale) + shift).astype(BF16)


def _in_proj_decode_kernel(x_ref, scale_ref, shift_ref, nw_ref, w_ref, o_ref):
    h = _modulated_input(x_ref, scale_ref, shift_ref, nw_ref, 0, None)
    o_ref[...] = jnp.dot(h, w_ref[...], preferred_element_type=F32)


def _in_proj_prompt_kernel(x_ref, scale_ref, shift_ref, nw_ref, w_ref, buf_ref, cw_ref, lbl_ref,
                           o_ref, cfin_ref, raw, h_scr, *, tiles_per_seq, layer):
    i = pl.program_id(0)
    tm = x_ref.shape[0]
    tile_in_seq = i % tiles_per_seq
    body = slice(8, 8 + tm)

    @pl.when(tile_in_seq == 0)
    def _():
        raw[0:8, :] = buf_ref[...]

    h_scr[...] = _modulated_input(x_ref, scale_ref, shift_ref, nw_ref, i, tiles_per_seq)
    lb_all = _forget_lower_bound(lbl_ref[...], layer)

    def project(cs):
        return jnp.dot(h_scr[...], w_ref[:, cs], preferred_element_type=F32)

    def conv_cols(cs):
        raw[body, cs] = project(cs)
        acc = raw[5:5 + tm, cs] * cw_ref[0:1, cs]
        for j in range(1, CONV_W):
            acc = acc + raw[5 + j:5 + j + tm, cs] * cw_ref[j:j + 1, cs]
        return _silu(acc)

    def l2_normalised(cs, scale):
        conv = conv_cols(cs)
        for off in range(0, cs.stop - cs.start, HEAD_DIM):
            v = conv[:, off:off + HEAD_DIM]
            o_ref[:, cs.start + off:cs.start + off + HEAD_DIM] = (
                v * lax.rsqrt(jnp.sum(v * v, axis=-1, keepdims=True) + EPS) * scale)

    def post_q(cs):
        l2_normalised(cs, HEAD_DIM ** -0.5)

    def post_k(cs):
        l2_normalised(cs, 1.0)

    def post_v(cs):
        o_ref[:, cs] = conv_cols(cs)

    def post_copy(cs):
        o_ref[:, cs] = project(cs)

    def post_silu(cs):
        o_ref[:, cs] = _silu(project(cs))

    def post_forget(cs):
        gate_cols = slice(cs.start - COL_B - MIX_HALF, cs.stop - COL_B - MIX_HALF)
        lb = lb_all[:, gate_cols]
        f = lb + (1.0 - lb) * _sigmoid(project(cs))
        o_ref[:, cs] = 1.0 - f
        o_ref[:, PROMPT_COL_G + gate_cols.start:PROMPT_COL_G + gate_cols.stop] = (
            _chunk_cumsum(jnp.log(f), HGRN_CHUNK))

    width = 2 * HEAD_DIM
    pieces = ([post_q] * 2 + [post_k] * 2 + [post_v] * 2 + [post_copy] * 2
              + [post_silu] * 2 + [post_forget] * 2 + [post_copy] * 4)
    for piece, post in enumerate(pieces):
        post(slice(piece * width, (piece + 1) * width))
    o_ref[:, PROMPT_COL_AB:PROMPT_COLS] = project(slice(COL_AB, IN_COLS))

    tail = raw[tm:tm + 8, :]
    raw[0:8, :] = tail

    @pl.when(tile_in_seq == tiles_per_seq - 1)
    def _():
        cfin_ref[...] = tail[8 - (CONV_W - 1):, :]


def _in_proj_call(x, mods, norm1, w_in, conv_buf, conv_w, lb_logits, *, layer, n_prompt, per_token,
                  seq_len):
    rows = x.shape[0]
    in_specs = [
        None,
        _mod_spec(mods, layer, n_prompt, per_token, MOD_SCALE1),
        _mod_spec(mods, layer, n_prompt, per_token, MOD_SHIFT1),
        _layer_spec((1, D_MODEL), layer),
        _layer_spec((D_MODEL, IN_COLS), layer, pipeline_mode=pl.Buffered(1)),
    ]
    params = pltpu.CompilerParams(dimension_semantics=("arbitrary",), vmem_limit_bytes=VMEM_LIMIT)
    if per_token:
        in_specs[0] = pl.BlockSpec((rows, D_MODEL), lambda i: (0, 0))
        return pl.pallas_call(
            _in_proj_decode_kernel,
            grid=(1,),
            in_specs=in_specs,
            out_specs=pl.BlockSpec((rows, IN_COLS), lambda i: (0, 0)),
            out_shape=jax.ShapeDtypeStruct((rows, IN_COLS), F32),
            compiler_params=params,
            name="in_proj_decode",
        )(x, mods, mods, norm1, w_in)

    tm = 512
    tiles_per_seq = seq_len // tm
    in_specs[0] = pl.BlockSpec((tm, D_MODEL), lambda i: (i, 0))
    in_specs += [
        pl.BlockSpec((None, 8, CONV_DIM), lambda i: (i // tiles_per_seq, 0, 0)),
        _layer_spec((CONV_W, CONV_DIM), layer),
        pl.BlockSpec((DEPTH, MIX_HALF), lambda i: (0, 0)),
    ]
    kern = functools.partial(_in_proj_prompt_kernel, tiles_per_seq=tiles_per_seq, layer=layer)
    return pl.pallas_call(
        kern,
        grid=(rows // tm,),
        in_specs=in_specs,
        out_specs=[
            pl.BlockSpec((tm, PROMPT_COLS), lambda i: (i, 0)),
            pl.BlockSpec((None, CONV_W - 1, CONV_DIM), lambda i: (i // tiles_per_seq, 0, 0)),
        ],
        out_shape=[
            jax.ShapeDtypeStruct((rows, PROMPT_COLS), F32),
            jax.ShapeDtypeStruct((n_prompt, CONV_W - 1, CONV_DIM), F32),
        ],
        scratch_shapes=[
            pltpu.VMEM((tm + 8, CONV_DIM), F32),
            pltpu.VMEM((tm, D_MODEL), BF16),
        ],
        compiler_params=params,
        name="in_proj_prompt",
    )(x, mods, mods, norm1, w_in, conv_buf, conv_w, lb_logits)


def _unit_lower_inverse(mats, sub_mask, eye):
    d = [jnp.where(sub_mask, a, 0.0) for a in mats]
    n = [a - x for a, x in zip(mats, d)]
    d2 = [_mm(x, x) for x in d]
    d4 = [_mm(x, x) for x in d2]
    d8 = [_mm(x, x) for x in d4]
    p = [eye - x + x2 - _mm(x, x2) for x, x2 in zip(d, d2)]
    p = [x + _mm(x, y) for x, y in zip(p, d4)]
    dinv = [x + _mm(x, y) for x, y in zip(p, d8)]
    e = [_mm(x, y) for x, y in zip(dinv, n)]
    e2 = [_mm(x, x) for x in e]
    e4 = [_mm(x, x) for x in e2]
    f = [x2 - x - _mm(x, x2) for x, x2 in zip(e, e2)]
    q = [x + y + _mm(x, y) for x, y in zip(f, e4)]
    return [x + _mm(y, x) for x, y in zip(dinv, q)]


def _gdn_kernel(qkv_ref, z_ref, ab_ref, s0_ref, alog_ref, dtb_ref, na_ref,
                mix_ref, sfin_ref, s_scr, *, tile):
    t = pl.program_id(1)

    @pl.when(t == 0)
    def _():
        s_scr[...] = s0_ref[...]

    ri = lax.broadcasted_iota(jnp.int32, (GDN_CHUNK, GDN_CHUNK), 0)
    ci = lax.broadcasted_iota(jnp.int32, (GDN_CHUNK, GDN_CHUNK), 1)
    m_incl = ri >= ci
    m_strict = ri > ci
    sub_mask = jnp.bitwise_xor(ri, ci) < GDN_SUB
    eye = jnp.where(ri == ci, 1.0, 0.0).astype(F32)
    n_chunks = tile // GDN_CHUNK
    pairs = [(b, h) for b in range(qkv_ref.shape[0]) for h in range(N_HEADS)]
    chains = [(p, c) for p in pairs for c in range(n_chunks)]

    def rows(c):
        return slice(c * GDN_CHUNK, (c + 1) * GDN_CHUNK)

    g_cum_t = {}
    qn, kn, kb, gc, eg, rhs = {}, {}, {}, {}, {}, {}
    for b in range(qkv_ref.shape[0]):
        ab = ab_ref[b]
        g_all = -jnp.exp(alog_ref[...]) * _softplus(ab + dtb_ref[...])
        sig_all = _sigmoid(ab)
        g_cum = _chunk_cumsum(g_all, GDN_CHUNK)
        g_cum_t[b] = g_cum.T
        for h in range(N_HEADS):
            p = (b, h)
            qn[p] = qkv_ref[b, :, h * HEAD_DIM:(h + 1) * HEAD_DIM]
            kn[p] = qkv_ref[b, :, MIX_HALF + h * HEAD_DIM:MIX_HALF + (h + 1) * HEAD_DIM]
            vh = qkv_ref[b, :, 2 * MIX_HALF + h * HEAD_DIM:2 * MIX_HALF + (h + 1) * HEAD_DIM]
            beta = sig_all[:, N_HEADS + h:N_HEADS + h + 1]
            gc[p] = g_cum[:, h:h + 1]
            eg[p] = jnp.exp(gc[p])
            kb[p] = kn[p] * beta
            rhs[p] = jnp.concatenate([vh * beta, kb[p] * eg[p]], axis=1)

    dec = {}
    for p, c in chains:
        gr = g_cum_t[p[0]][p[1]:p[1] + 1, rows(c)]
        dec[p, c] = jnp.exp(jnp.where(m_incl, gc[p][rows(c)] - gr, -jnp.inf))
    st = {(p, c): _mm_nt(jnp.concatenate([kb[p][rows(c)], qn[p][rows(c)]], axis=0), kn[p][rows(c)])
          for p, c in chains}
    a = [st[pc][:GDN_CHUNK] * jnp.where(m_strict, dec[pc], 0.0) for pc in chains]
    qk = {pc: st[pc][GDN_CHUNK:] * dec[pc] for pc in chains}
    tinv = dict(zip(chains, _unit_lower_inverse(a, sub_mask, eye)))
    uw = {(p, c): _mm(tinv[p, c], rhs[p][rows(c)]) for p, c in chains}

    s = {p: s_scr[p] for p in pairs}
    for c in range(n_chunks):
        rs = rows(c)
        r = {p: _mm(jnp.concatenate([uw[p, c][:, HEAD_DIM:], qn[p][rs] * eg[p][rs]], axis=0), s[p])
             for p in pairs}
        v_new = {p: uw[p, c][:, :HEAD_DIM] - r[p][:GDN_CHUNK] for p in pairs}
        for p in pairs:
            g_last = gc[p][(c + 1) * GDN_CHUNK - 1:(c + 1) * GDN_CHUNK, :]
            kd = kn[p][rs] * jnp.exp(g_last - gc[p][rs])
            s[p] = s[p] * jnp.exp(g_last) + _mm_tn(kd, v_new[p])
        for b, h in pairs:
            cs = slice(h * HEAD_DIM, (h + 1) * HEAD_DIM)
            o = r[b, h][GDN_CHUNK:] + _mm(qk[(b, h), c], v_new[b, h])
            mix_ref[b, rs, cs] = (_rms(o, na_ref[...]) * _silu(z_ref[b, rs, cs])).astype(mix_ref.dtype)
    for p in pairs:
        s_scr[p] = s[p]

    @pl.when(t == pl.num_programs(1) - 1)
    def _():
        sfin_ref[...] = s_scr[...]


MIXER_SEQS = 2
MIXER_TILE = 256


def _mixer_specs(cols, col_block):
    return pl.BlockSpec((MIXER_SEQS, MIXER_TILE, cols), lambda b, t: (b, t, col_block))


_STATE_BLOCK = (MIXER_SEQS, N_HEADS, HEAD_DIM, HEAD_DIM)


def _gdn_call(proj, s0, a_log_rows, dt_bias_rows, norm_a, *, layer):
    batch, seq_len, _ = proj.shape
    state_spec = pl.BlockSpec(_STATE_BLOCK, lambda b, t: (b, 0, 0, 0))
    return pl.pallas_call(
        functools.partial(_gdn_kernel, tile=MIXER_TILE),
        grid=(batch // MIXER_SEQS, seq_len // MIXER_TILE),
        in_specs=[
            _mixer_specs(CONV_DIM, COL_QKV // CONV_DIM),
            _mixer_specs(MIX_HALF, COL_Z // MIX_HALF),
            _mixer_specs(128, PROMPT_COL_AB // 128),
            state_spec,
            _layer_spec((1, 128), layer),
            _layer_spec((1, 128), layer),
            _layer_spec((1, HEAD_DIM), layer),
        ],
        out_specs=[_mixer_specs(MIX_HALF, 0), state_spec],
        out_shape=[
            jax.ShapeDtypeStruct((batch, seq_len, MIX_HALF), BF16),
            jax.ShapeDtypeStruct((batch, N_HEADS, HEAD_DIM, HEAD_DIM), F32),
        ],
        scratch_shapes=[pltpu.VMEM(_STATE_BLOCK, F32)],
        compiler_params=pltpu.CompilerParams(
            dimension_semantics=("arbitrary", "arbitrary"), vmem_limit_bytes=VMEM_LIMIT),
        name="gdn_prompt",
    )(proj, proj, proj, s0, a_log_rows, dt_bias_rows, norm_a)


def _forget_lower_bound(lbl, layer):
    m = jnp.max(lbl, axis=0, keepdims=True)
    e = jnp.exp(lbl - m)
    sm = e / jnp.sum(e, axis=0, keepdims=True)
    cs = sm[0:1, :]
    for i in range(1, layer + 1):
        cs = cs + sm[i:i + 1, :]
    return cs - sm[0:1, :]


def _midpoint_rows(g, s, row_id):
    n = g.shape[0]
    if s >= 4:
        parts = [jnp.broadcast_to(g[b + s:b + s + 1, :], (2 * s, g.shape[1]))
                 for b in range(0, n, 2 * s)]
        return jnp.concatenate(parts, axis=0)
    nxt = pltpu.roll(g, n - 1, 0)
    if s == 1:
        return jnp.where((row_id & 1) == 1, g, nxt)
    pos = row_id & 3
    return jnp.where(pos == 0, pltpu.roll(g, n - 2, 0),
                     jnp.where(pos == 1, nxt, jnp.where(pos == 2, g, pltpu.roll(g, 1, 0))))


def _hgrn_kernel(p_ref, g_ref, nb_ref, s0_ref, mix_ref, sfin_ref, st_scr, *, tile):
    t = pl.program_id(1)
    pairs = [(b, h) for b in range(p_ref.shape[0]) for h in range(N_HEADS)]

    @pl.when(t == 0)
    def _():
        for p in pairs:
            st_scr[p] = s0_ref[p].T

    n_chunks = tile // HGRN_CHUNK
    chains = [(p, c) for p in pairs for c in range(n_chunks)]

    def rows(c):
        return slice(c * HGRN_CHUNK, (c + 1) * HGRN_CHUNK)

    def cols(block, h):
        return slice(block * MIX_HALF + h * HEAD_DIM, block * MIX_HALF + (h + 1) * HEAD_DIM)

    row_id = lax.broadcasted_iota(jnp.int32, (tile, HEAD_DIM), 0)
    ri = lax.broadcasted_iota(jnp.int32, (HGRN_CHUNK, HGRN_CHUNK), 0)
    ci = lax.broadcasted_iota(jnp.int32, (HGRN_CHUNK, HGRN_CHUNK), 1)
    dist = jnp.bitwise_xor(ri, ci)
    lower = ri > ci

    q = {(b, h): p_ref[b, :, cols(0, h)] for b, h in pairs}
    k = {(b, h): p_ref[b, :, cols(1, h)] for b, h in pairs}
    g = {(b, h): g_ref[b, :, cols(0, h)] for b, h in pairs}

    a = {(p, c): jnp.where(ri == ci, _mm_nt(q[p][rows(c)], k[p][rows(c)]), 0.0) for p, c in chains}
    s = HGRN_CHUNK // 2
    while s >= 1:
        upper = (row_id & s) != 0
        level = lower & (dist >= s) & (dist < 2 * s)
        x = {}
        for p in pairs:
            d = g[p] - _midpoint_rows(g[p], s, row_id)
            x[p] = jnp.where(upper, q[p], k[p]) * jnp.exp(jnp.where(upper, d, -d))
        for p, c in chains:
            xc = x[p][rows(c)]
            a[p, c] = jnp.where(level, _mm_nt(xc, xc), a[p, c])
        s //= 2

    st = {p: st_scr[p] for p in pairs}
    for c in range(n_chunks):
        rs = rows(c)
        for b, h in pairs:
            p = (b, h)
            gc = g[p][rs]
            vc = p_ref[b, rs, cols(2, h)]
            o = _mm(a[p, c], vc) + _mm_nt(q[p][rs] * jnp.exp(gc), st[p])
            g_last = gc[HGRN_CHUNK - 1:HGRN_CHUNK, :]
            st[p] = st[p] * jnp.exp(g_last) + _mm_tn(vc, k[p][rs] * jnp.exp(g_last - gc))
            gate = _sigmoid(p_ref[b, rs, cols(3, h)])
            mix_ref[b, rs, cols(0, h)] = (_rms(o, nb_ref[...]) * gate).astype(mix_ref.dtype)
    for p in pairs:
        st_scr[p] = st[p]

    @pl.when(t == pl.num_programs(1) - 1)
    def _():
        for p in pairs:
            sfin_ref[p] = st_scr[p].T


def _hgrn_call(proj, norm_b, s0, *, layer):
    batch, seq_len, _ = proj.shape
    state_spec = pl.BlockSpec(_STATE_BLOCK, lambda b, t: (b, 0, 0, 0))
    return pl.pallas_call(
        functools.partial(_hgrn_kernel, tile=MIXER_TILE),
        grid=(batch // MIXER_SEQS, seq_len // MIXER_TILE),
        in_specs=[
            _mixer_specs(4 * MIX_HALF, COL_B // (4 * MIX_HALF)),
            _mixer_specs(MIX_HALF, PROMPT_COL_G // MIX_HALF),
            _layer_spec((1, HEAD_DIM), layer),
            state_spec,
        ],
        out_specs=[_mixer_specs(MIX_HALF, 0), state_spec],
        out_shape=[
            jax.ShapeDtypeStruct((batch, seq_len, MIX_HALF), BF16),
            jax.ShapeDtypeStruct((batch, N_HEADS, HEAD_DIM, HEAD_DIM), F32),
        ],
        scratch_shapes=[pltpu.VMEM(_STATE_BLOCK, F32)],
        compiler_params=pltpu.CompilerParams(
            dimension_semantics=("arbitrary", "arbitrary"), vmem_limit_bytes=VMEM_LIMIT),
        name="hgrn_prompt",
    )(proj, proj, norm_b, s0)


def _decode_kernel(*refs, bt, layer):
    (qkv_ref, z_ref, ab_ref, pb_ref, conv_ref, sd_ref, sh_ref,
     cw_ref, alog_ref, dtb_ref, na_ref, nb_ref, lbl_ref) = refs[:13]
    mix_ref, convo_ref, sdo_ref, sho_ref, oa_scr, ob_scr = refs[-6:]
    if layer > 0:
        prev_sd_ref, prev_sh_ref = refs[13:15]
        sdo_ref[0:layer] = prev_sd_ref[...]
        sho_ref[0:layer] = prev_sh_ref[...]
    cw = cw_ref[...]
    u = qkv_ref[...]
    b0 = conv_ref[:, 0:CONV_DIM]
    b1 = conv_ref[:, CONV_DIM:2 * CONV_DIM]
    b2 = conv_ref[:, 2 * CONV_DIM:3 * CONV_DIM]
    acc = b0 * cw[0:1, :]
    acc = acc + b1 * cw[1:2, :]
    acc = acc + b2 * cw[2:3, :]
    acc = acc + u * cw[3:4, :]
    conv = _silu(acc)
    convo_ref[:, 0:CONV_DIM] = b1
    convo_ref[:, CONV_DIM:2 * CONV_DIM] = b2
    convo_ref[:, 2 * CONV_DIM:3 * CONV_DIM] = u

    ab = ab_ref[...]
    eg_all = jnp.exp(-jnp.exp(alog_ref[...]) * _softplus(ab + dtb_ref[...]))
    beta_all = _sigmoid(ab)

    lb = _forget_lower_bound(lbl_ref[...], layer)
    f = lb + (1.0 - lb) * _sigmoid(pb_ref[:, MIX_HALF:2 * MIX_HALF])
    qb = _silu(pb_ref[:, 0:MIX_HALF])

    k_rows, q_rows, v_rows = [], [], []
    for h in range(N_HEADS):
        cs = slice(h * HEAD_DIM, (h + 1) * HEAD_DIM)
        qh = conv[:, cs]
        kh = conv[:, MIX_HALF + h * HEAD_DIM:MIX_HALF + (h + 1) * HEAD_DIM]
        q_rows.append(qh * lax.rsqrt(jnp.sum(qh * qh, axis=-1, keepdims=True) + EPS)
                      * (HEAD_DIM ** -0.5))
        k_rows.append(kh * lax.rsqrt(jnp.sum(kh * kh, axis=-1, keepdims=True) + EPS))
        v_rows.append(conv[:, 2 * MIX_HALF + h * HEAD_DIM:2 * MIX_HALF + (h + 1) * HEAD_DIM])
    f_rows = [f[:, h * HEAD_DIM:(h + 1) * HEAD_DIM] for h in range(N_HEADS)]
    qb_rows = [qb[:, h * HEAD_DIM:(h + 1) * HEAD_DIM] for h in range(N_HEADS)]

    stacked = jnp.concatenate(k_rows + q_rows + f_rows + qb_rows, axis=0)
    cols = stacked.T

    def col(item, h, b):
        i = (item * N_HEADS + h) * bt + b
        return cols[:, i:i + 1]

    for b in range(bt):
        for h in range(N_HEADS):
            cs = slice(h * HEAD_DIM, (h + 1) * HEAD_DIM)
            s = sd_ref[b, h]
            kc = col(0, h, b)
            eg = eg_all[b:b + 1, h:h + 1]
            beta = beta_all[b:b + 1, N_HEADS + h:N_HEADS + h + 1]
            sk = jnp.sum(s * kc, axis=0, keepdims=True)
            v_new = beta * (v_rows[h][b:b + 1, :] - eg * sk)
            s_new = s * eg + kc * v_new
            sdo_ref[layer, b, h] = s_new
            oa_scr[b:b + 1, cs] = jnp.sum(s_new * col(1, h, b), axis=0, keepdims=True)
            s = sh_ref[b, h]
            fc = col(2, h, b)
            vb = pb_ref[b:b + 1, 2 * MIX_HALF + h * HEAD_DIM:2 * MIX_HALF + (h + 1) * HEAD_DIM]
            s_new = s * fc + (1.0 - fc) * vb
            sho_ref[layer, b, h] = s_new
            ob_scr[b:b + 1, cs] = jnp.sum(s_new * col(3, h, b), axis=0, keepdims=True)

    for h in range(N_HEADS):
        cs = slice(h * HEAD_DIM, (h + 1) * HEAD_DIM)
        mix_ref[:, cs] = (_rms(oa_scr[:, cs], na_ref[...]) * _silu(z_ref[:, cs])).astype(mix_ref.dtype)
        gate = _sigmoid(pb_ref[:, 3 * MIX_HALF + h * HEAD_DIM:3 * MIX_HALF + (h + 1) * HEAD_DIM])
        mix_ref[:, MIX_HALF + h * HEAD_DIM:MIX_HALF + (h + 1) * HEAD_DIM] = (
            _rms(ob_scr[:, cs], nb_ref[...]) * gate).astype(mix_ref.dtype)


def _decode_call(proj, conv_state, s_delta, s_hgrn, prev_states, conv_w, a_log_rows, dt_bias_rows,
                 norm_a, norm_b, lb_logits, *, layer):
    batch = proj.shape[0]
    bt = 8
    kern = functools.partial(_decode_kernel, bt=bt, layer=layer)
    state_dims = (bt, N_HEADS, HEAD_DIM, HEAD_DIM)
    old_state_spec = pl.BlockSpec((None,) + state_dims, lambda i: (layer, i, 0, 0, 0))
    prev_state_spec = pl.BlockSpec((layer,) + state_dims, lambda i: (0, i, 0, 0, 0))
    new_state_spec = pl.BlockSpec((layer + 1,) + state_dims, lambda i: (0, i, 0, 0, 0))
    new_state_shape = jax.ShapeDtypeStruct((layer + 1, batch, N_HEADS, HEAD_DIM, HEAD_DIM), F32)
    return pl.pallas_call(
        kern,
        grid=(batch // bt,),
        in_specs=[
            pl.BlockSpec((bt, CONV_DIM), lambda i: (i, COL_QKV // CONV_DIM)),
            pl.BlockSpec((bt, MIX_HALF), lambda i: (i, COL_Z // MIX_HALF)),
            pl.BlockSpec((bt, 128), lambda i: (i, COL_AB // 128)),
            pl.BlockSpec((bt, 4 * MIX_HALF), lambda i: (i, COL_B // (4 * MIX_HALF))),
            pl.BlockSpec((None, bt, (CONV_W - 1) * CONV_DIM), lambda i: (layer, i, 0)),
            old_state_spec,
            old_state_spec,
            _layer_spec((CONV_W, CONV_DIM), layer),
            _layer_spec((1, 128), layer),
            _layer_spec((1, 128), layer),
            _layer_spec((1, HEAD_DIM), layer),
            _layer_spec((1, HEAD_DIM), layer),
            pl.BlockSpec((DEPTH, MIX_HALF), lambda i: (0, 0)),
        ] + [prev_state_spec] * len(prev_states),
        out_specs=[
            pl.BlockSpec((bt, 2 * MIX_HALF), lambda i: (i, 0)),
            pl.BlockSpec((bt, (CONV_W - 1) * CONV_DIM), lambda i: (i, 0)),
            new_state_spec,
            new_state_spec,
        ],
        out_shape=[
            jax.ShapeDtypeStruct((batch, 2 * MIX_HALF), BF16),
            jax.ShapeDtypeStruct((batch, (CONV_W - 1) * CONV_DIM), F32),
            new_state_shape,
            new_state_shape,
        ],
        scratch_shapes=[
            pltpu.VMEM((bt, MIX_HALF), F32),
            pltpu.VMEM((bt, MIX_HALF), F32),
        ],
        compiler_params=pltpu.CompilerParams(
            dimension_semantics=("arbitrary",), vmem_limit_bytes=VMEM_LIMIT),
        name="mixers_decode",
    )(proj, proj, proj, proj, conv_state, s_delta, s_hgrn, conv_w, a_log_rows, dt_bias_rows,
      norm_a, norm_b, lb_logits, *prev_states)


FF_CHUNK = 256


def _out_ffn_kernel(x_ref, ma_ref, mb_ref, g1_ref, sh2_ref, sc2_ref, g2_ref, wo_ref, n2_ref,
                    wg_ref, wu_ref, wd_ref, nf_ref, o_ref, acc_ref, *, final_norm, tiles_per_seq):
    gate1 = _mod_rows(g1_ref, tiles_per_seq)
    shift2 = _mod_rows(sh2_ref, tiles_per_seq)
    scale2 = _mod_rows(sc2_ref, tiles_per_seq)
    gate2 = _mod_rows(g2_ref, tiles_per_seq)
    mix = _mm(ma_ref[...], wo_ref[0:MIX_HALF, :]) + _mm(mb_ref[...], wo_ref[MIX_HALF:, :])
    x1 = x_ref[...] + gate1 * mix
    h = (_rms(x1, n2_ref[...]) * (1.0 + scale2) + shift2).astype(BF16)
    for c in range(D_FF // FF_CHUNK):
        fs = slice(c * FF_CHUNK, (c + 1) * FF_CHUNK)
        gate = jnp.dot(h, wg_ref[:, fs], preferred_element_type=F32)
        up = jnp.dot(h, wu_ref[:, fs], preferred_element_type=F32)
        part = _mm(_silu(gate) * up, wd_ref[fs, :])
        if c == 0:
            acc_ref[...] = part
        else:
            acc_ref[...] += part
    x2 = x1 + gate2 * acc_ref[...]
    if final_norm:
        x2 = _rms(x2, nf_ref[...])
    o_ref[...] = x2


def _out_ffn_call(x, mix_a, mix_b, mix_b_col, mods, w_out, norm2, w_gate, w_up, w_down, norm_f,
                  *, layer, n_prompt, per_token, seq_len, final_norm):
    rows = x.shape[0]
    tm = rows if per_token else 512
    tiles_per_seq = None if per_token else seq_len // tm
    resident = functools.partial(_layer_spec, layer=layer, pipeline_mode=pl.Buffered(1))
    mod = functools.partial(_mod_spec, mods, layer, n_prompt, per_token)
    kern = functools.partial(_out_ffn_kernel, final_norm=final_norm, tiles_per_seq=tiles_per_seq)
    return pl.pallas_call(
        kern,
        grid=(rows // tm,),
        in_specs=[
            pl.BlockSpec((tm, D_MODEL), lambda i: (i, 0)),
            pl.BlockSpec((tm, MIX_HALF), lambda i: (i, 0)),
            pl.BlockSpec((tm, MIX_HALF), lambda i: (i, mix_b_col)),
            mod(MOD_GATE1),
            mod(MOD_SHIFT2),
            mod(MOD_SCALE2),
            mod(MOD_GATE2),
            resident((D_MODEL, D_MODEL)),
            _layer_spec((1, D_MODEL), layer),
            resident((D_MODEL, D_FF)),
            resident((D_MODEL, D_FF)),
            resident((D_FF, D_MODEL)),
            pl.BlockSpec((1, D_MODEL), lambda i: (0, 0)),
        ],
        out_specs=pl.BlockSpec((tm, D_MODEL), lambda i: (i, 0)),
        out_shape=jax.ShapeDtypeStruct((rows, D_MODEL), F32),
        scratch_shapes=[pltpu.VMEM((tm, D_MODEL), F32)],
        compiler_params=pltpu.CompilerParams(
            dimension_semantics=("arbitrary",), vmem_limit_bytes=VMEM_LIMIT),
        name="out_ffn",
    )(x, mix_a, mix_b, mods, mods, mods, mods, w_out, norm2, w_gate, w_up, w_down, norm_f)


def _reorder_in_proj(w):
    qkvz = w[..., 0:COL_B]
    ab = w[..., COL_B:COL_B + 2 * N_HEADS]
    rest = w[..., COL_B + 2 * N_HEADS:]
    pad = jnp.zeros(w.shape[:-1] + (128 - 2 * N_HEADS,), w.dtype)
    return jnp.concatenate([qkvz, rest, ab, pad], axis=-1).astype(BF16)


def _lane_rows(v):
    return jnp.pad(v.astype(F32), ((0, 0), (0, 128 - v.shape[1])))[:, None, :]


def kernel(x_prompt, x_sample, c_prompt, c_sample, state_delta, state_conv, state_hgrn, w_ada, b_ada,
           norm1, w_in, conv_w, a_log, dt_bias, norm_a, norm_b, lb_logits, w_out, norm2, w_gate, w_up,
           w_down, norm_f):
    bp, seq_len, _ = x_prompt.shape
    bs = x_sample.shape[0]

    mods = _ada_call(jnp.concatenate([c_sample, c_prompt], axis=0), w_ada, b_ada)

    xp = x_prompt.reshape(bp * seq_len, D_MODEL)
    xs = x_sample.reshape(bs, D_MODEL)
    zero_buf = jnp.zeros((bp, 8, CONV_DIM), F32)
    zero_state = jnp.zeros((bp, N_HEADS, HEAD_DIM, HEAD_DIM), F32)

    w_in_b = _reorder_in_proj(w_in)
    w_out_b = w_out.astype(BF16)
    w_gate_b = w_gate.astype(BF16)
    w_up_b = w_up.astype(BF16)
    w_down_b = w_down.astype(BF16)
    a_log_rows = _lane_rows(a_log)
    dt_bias_rows = _lane_rows(dt_bias)
    norm1_r = norm1[:, None, :]
    norm2_r = norm2[:, None, :]
    norm_a_r = norm_a[:, None, :]
    norm_b_r = norm_b[:, None, :]
    norm_f_r = norm_f[None, :]
    conv_state = state_conv.reshape(DEPTH, bs, (CONV_W - 1) * CONV_DIM)

    delta_p, conv_p, hgrn_p, conv_s = [], [], [], []
    decode_states = ()
    for l in range(DEPTH):
        last = l == DEPTH - 1
        dense = dict(layer=l, n_prompt=bp)

        proj, cv = _in_proj_call(xp, mods, norm1_r, w_in_b, zero_buf, conv_w, lb_logits,
                                 per_token=False, seq_len=seq_len, **dense)
        proj = proj.reshape(bp, seq_len, PROMPT_COLS)
        mix_a, s_a = _gdn_call(proj, zero_state, a_log_rows, dt_bias_rows, norm_a_r, layer=l)
        mix_b, s_b = _hgrn_call(proj, norm_b_r, zero_state, layer=l)
        mix_a = mix_a.reshape(bp * seq_len, MIX_HALF)
        mix_b = mix_b.reshape(bp * seq_len, MIX_HALF)
        xp = _out_ffn_call(xp, mix_a, mix_b, 0, mods, w_out_b, norm2_r, w_gate_b, w_up_b, w_down_b,
                           norm_f_r, per_token=False, seq_len=seq_len, final_norm=last, **dense)
        delta_p.append(s_a)
        conv_p.append(cv)
        hgrn_p.append(s_b)

        proj = _in_proj_call(xs, mods, norm1_r, w_in_b, None, None, None,
                             per_token=True, seq_len=1, **dense)
        mix, cv, *decode_states = _decode_call(
            proj, conv_state, state_delta, state_hgrn, decode_states,
            conv_w, a_log_rows, dt_bias_rows, norm_a_r, norm_b_r, lb_logits, layer=l)
        xs = _out_ffn_call(xs, mix, mix, 1, mods, w_out_b, norm2_r, w_gate_b, w_up_b, w_down_b,
                           norm_f_r, per_token=True, seq_len=1, final_norm=last, **dense)
        conv_s.append(cv.reshape(bs, CONV_W - 1, CONV_DIM))

    delta_s, hgrn_s = decode_states
    return (xp.reshape(bp, seq_len, D_MODEL), xs.reshape(bs, 1, D_MODEL),
            jnp.stack(delta_p), jnp.stack(conv_p), jnp.stack(hgrn_p),
            delta_s, jnp.stack(conv_s), hgrn_s)
```

```python
import functools

import jax
import jax.numpy as jnp
from jax import lax
from jax.experimental import pallas as pl
from jax.experimental.pallas import tpu as pltpu

F32 = jnp.float32
BF16 = jnp.bfloat16

D_MODEL = 1024
DEPTH = 2
N_HEADS = 4
HEAD_DIM = 128
MIX_HALF = N_HEADS * HEAD_DIM
CONV_W = 4
CONV_DIM = 3 * MIX_HALF
D_FF = 2816
EPS = 1e-6
GDN_CHUNK = 128
GDN_SUB = 16
HGRN_CHUNK = 64

COL_QKV = 0
COL_Z = CONV_DIM
COL_B = COL_Z + MIX_HALF
COL_AB = COL_B + 4 * MIX_HALF
IN_COLS = COL_AB + 128
PROMPT_COL_G = COL_AB
PROMPT_COL_AB = PROMPT_COL_G + MIX_HALF
PROMPT_COLS = PROMPT_COL_AB + 128

MOD_SHIFT1, MOD_SCALE1, MOD_GATE1, MOD_SHIFT2, MOD_SCALE2, MOD_GATE2 = range(6)

VMEM_LIMIT = 56 * 1024 * 1024


def _sigmoid(x):
    return 1.0 / (1.0 + jnp.exp(-x))


def _silu(x):
    return x * _sigmoid(x)


def _softplus(x):
    return jnp.maximum(x, 0.0) + jnp.log1p(jnp.exp(-jnp.abs(x)))


def _mm(a, b):
    return jnp.dot(a.astype(BF16), b.astype(BF16), preferred_element_type=F32)


def _mm_nt(a, b):
    return lax.dot_general(a.astype(BF16), b.astype(BF16), (((1,), (1,)), ((), ())),
                           preferred_element_type=F32)


def _mm_tn(a, b):
    return lax.dot_general(a.astype(BF16), b.astype(BF16), (((0,), (0,)), ((), ())),
                           preferred_element_type=F32)


def _rms(x, w):
    return x * lax.rsqrt(jnp.mean(x * x, axis=-1, keepdims=True) + EPS) * w


def _chunk_cumsum(x, chunk):
    rows = lax.broadcasted_iota(jnp.int32, x.shape, 0) & (chunk - 1)
    s = 1
    while s < chunk:
        x = x + jnp.where(rows >= s, pltpu.roll(x, s, 0), 0.0)
        s *= 2
    return x


def _ada_kernel(c_ref, w_ref, b_ref, o_ref):
    a = _silu(c_ref[...])
    o_ref[...] = _mm(a, w_ref[...]) + b_ref[...]


def _ada_call(c_all, w_ada, b_ada):
    rows = c_all.shape[0]
    tn = 1024
    return pl.pallas_call(
        _ada_kernel,
        grid=(DEPTH, 6 * D_MODEL // tn),
        in_specs=[
            pl.BlockSpec((rows, D_MODEL), lambda l, j: (0, 0)),
            pl.BlockSpec((None, D_MODEL, tn), lambda l, j: (l, 0, j)),
            pl.BlockSpec((None, 1, tn), lambda l, j: (l, 0, j)),
        ],
        out_specs=pl.BlockSpec((None, rows, tn), lambda l, j: (l, 0, j)),
        out_shape=jax.ShapeDtypeStruct((DEPTH, rows, 6 * D_MODEL), F32),
        compiler_params=pltpu.CompilerParams(
            dimension_semantics=("arbitrary", "arbitrary"), vmem_limit_bytes=VMEM_LIMIT),
        name="ada_mod",
    )(c_all, w_ada, b_ada.reshape(DEPTH, 1, 6 * D_MODEL))


def _mod_spec(mods, layer, n_prompt, per_token, col):
    n_decode = mods.shape[1] - n_prompt
    if per_token:
        return pl.BlockSpec((None, n_decode, D_MODEL), lambda i: (layer, 0, col))
    return pl.BlockSpec((None, n_prompt, D_MODEL), lambda i: (layer, n_decode // n_prompt, col))


def _mod_rows(ref, tiles_per_seq):
    if tiles_per_seq is None:
        return ref[...]
    return ref[pl.ds(pl.program_id(0) // tiles_per_seq, 1), :]


def _layer_spec(shape, layer, **kwargs):
    zeros = (0,) * len(shape)
    return pl.BlockSpec((None,) + tuple(shape), lambda *_: (layer,) + zeros, **kwargs)


def _modulated_input(x_ref, scale_ref, shift_ref, nw_ref, tile, tiles_per_seq):
    if tiles_per_seq is None:
        scale, shift = scale_ref[...], shift_ref[...]
    else:
        seq = tile // tiles_per_seq
        scale, shift = scale_ref[pl.ds(seq, 1), :], shift_ref[pl.ds(seq, 1), :]
    return (_rms(x_ref[...], nw_ref[...]) * (1.0 + scale) + shift).astype(BF16)


def _in_proj_decode_kernel(x_ref, scale_ref, shift_ref, nw_ref, w_ref, o_ref):
    h = _modulated_input(x_ref, scale_ref, shift_ref, nw_ref, 0, None)
    o_ref[...] = jnp.dot(h, w_ref[...], preferred_element_type=F32)


def _in_proj_prompt_kernel(x_ref, scale_ref, shift_ref, nw_ref, w_ref, buf_ref, cw_ref, lbl_ref,
                           o_ref, cfin_ref, raw, h_scr, *, tiles_per_seq, layer):
    i = pl.program_id(0)
    tm = x_ref.shape[0]
    tile_in_seq = i % tiles_per_seq
    body = slice(8, 8 + tm)

    n_slabs = CONV_DIM // HEAD_DIM

    def slab_cols(ct):
        return slice(ct * HEAD_DIM, (ct + 1) * HEAD_DIM)

    @pl.when(tile_in_seq == 0)
    def _():
        for ct in range(n_slabs):
            raw[ct, 0:8, :] = buf_ref[:, slab_cols(ct)]

    h_scr[...] = _modulated_input(x_ref, scale_ref, shift_ref, nw_ref, i, tiles_per_seq)
    lb_all = _forget_lower_bound(lbl_ref[...], layer)

    def project(cs):
        return jnp.dot(h_scr[...], w_ref[:, cs], preferred_element_type=F32)

    def conv_slabs(cs):
        pre = project(cs)
        out = []
        for ct in range(cs.start // HEAD_DIM, cs.stop // HEAD_DIM):
            cols = slab_cols(ct)
            raw[ct, body, :] = pre[:, cols.start - cs.start:cols.stop - cs.start]
            acc = raw[ct, 5:5 + tm, :] * cw_ref[0:1, cols]
            for j in range(1, CONV_W):
                acc = acc + raw[ct, 5 + j:5 + j + tm, :] * cw_ref[j:j + 1, cols]
            out.append((cols, _silu(acc)))
        return out

    def l2_normalised(cs, scale):
        for cols, v in conv_slabs(cs):
            o_ref[:, cols] = v * lax.rsqrt(jnp.sum(v * v, axis=-1, keepdims=True) + EPS) * scale

    def post_q(cs):
        l2_normalised(cs, HEAD_DIM ** -0.5)

    def post_k(cs):
        l2_normalised(cs, 1.0)

    def post_v(cs):
        for cols, v in conv_slabs(cs):
            o_ref[:, cols] = v

    def post_copy(cs):
        o_ref[:, cs] = project(cs)

    def post_silu(cs):
        o_ref[:, cs] = _silu(project(cs))

    def post_forget(cs):
        gate_cols = slice(cs.start - COL_B - MIX_HALF, cs.stop - COL_B - MIX_HALF)
        lb = lb_all[:, gate_cols]
        f = lb + (1.0 - lb) * _sigmoid(project(cs))
        o_ref[:, cs] = 1.0 - f
        o_ref[:, PROMPT_COL_G + gate_cols.start:PROMPT_COL_G + gate_cols.stop] = (
            _chunk_cumsum(jnp.log(f), HGRN_CHUNK))

    width = 2 * HEAD_DIM
    pieces = ([post_q] * 2 + [post_k] * 2 + [post_v] * 2 + [post_copy] * 2
              + [post_silu] * 2 + [post_forget] * 2 + [post_copy] * 4)
    for piece, post in enumerate(pieces):
        post(slice(piece * width, (piece + 1) * width))
    o_ref[:, PROMPT_COL_AB:PROMPT_COLS] = project(slice(COL_AB, IN_COLS))

    for ct in range(n_slabs):
        raw[ct, 0:8, :] = raw[ct, tm:tm + 8, :]

    @pl.when(tile_in_seq == tiles_per_seq - 1)
    def _():
        for ct in range(n_slabs):
            cfin_ref[:, slab_cols(ct)] = raw[ct, 8 - (CONV_W - 1):8, :]


def _in_proj_call(x, mods, norm1, w_in, conv_buf, conv_w, lb_logits, *, layer, n_prompt, per_token,
                  seq_len):
    rows = x.shape[0]
    in_specs = [
        None,
        _mod_spec(mods, layer, n_prompt, per_token, MOD_SCALE1),
        _mod_spec(mods, layer, n_prompt, per_token, MOD_SHIFT1),
        _layer_spec((1, D_MODEL), layer),
        _layer_spec((D_MODEL, IN_COLS), layer, pipeline_mode=pl.Buffered(1)),
    ]
    params = pltpu.CompilerParams(dimension_semantics=("arbitrary",), vmem_limit_bytes=VMEM_LIMIT)
    if per_token:
        in_specs[0] = pl.BlockSpec((rows, D_MODEL), lambda i: (0, 0))
        return pl.pallas_call(
            _in_proj_decode_kernel,
            grid=(1,),
            in_specs=in_specs,
            out_specs=pl.BlockSpec((rows, IN_COLS), lambda i: (0, 0)),
            out_shape=jax.ShapeDtypeStruct((rows, IN_COLS), F32),
            compiler_params=params,
            name="in_proj_decode",
        )(x, mods, mods, norm1, w_in)

    tm = 512
    tiles_per_seq = seq_len // tm
    in_specs[0] = pl.BlockSpec((tm, D_MODEL), lambda i: (i, 0))
    in_specs += [
        pl.BlockSpec((None, 8, CONV_DIM), lambda i: (i // tiles_per_seq, 0, 0)),
        _layer_spec((CONV_W, CONV_DIM), layer),
        pl.BlockSpec((DEPTH, MIX_HALF), lambda i: (0, 0)),
    ]
    kern = functools.partial(_in_proj_prompt_kernel, tiles_per_seq=tiles_per_seq, layer=layer)
    return pl.pallas_call(
        kern,
        grid=(rows // tm,),
        in_specs=in_specs,
        out_specs=[
            pl.BlockSpec((tm, PROMPT_COLS), lambda i: (i, 0)),
            pl.BlockSpec((None, CONV_W - 1, CONV_DIM), lambda i: (i // tiles_per_seq, 0, 0)),
        ],
        out_shape=[
            jax.ShapeDtypeStruct((rows, PROMPT_COLS), F32),
            jax.ShapeDtypeStruct((n_prompt, CONV_W - 1, CONV_DIM), F32),
        ],
        scratch_shapes=[
            pltpu.VMEM((CONV_DIM // HEAD_DIM, tm + 8, HEAD_DIM), F32),
            pltpu.VMEM((tm, D_MODEL), BF16),
        ],
        compiler_params=params,
        name="in_proj_prompt",
    )(x, mods, mods, norm1, w_in, conv_buf, conv_w, lb_logits)


def _unit_lower_inverse(mats, sub_mask, eye):
    d = [jnp.where(sub_mask, a, 0.0) for a in mats]
    n = [a - x for a, x in zip(mats, d)]
    d2 = [_mm(x, x) for x in d]
    d4 = [_mm(x, x) for x in d2]
    d8 = [_mm(x, x) for x in d4]
    p = [eye - x + x2 - _mm(x, x2) for x, x2 in zip(d, d2)]
    p = [x + _mm(x, y) for x, y in zip(p, d4)]
    dinv = [x + _mm(x, y) for x, y in zip(p, d8)]
    e = [_mm(x, y) for x, y in zip(dinv, n)]
    e2 = [_mm(x, x) for x in e]
    e4 = [_mm(x, x) for x in e2]
    f = [x2 - x - _mm(x, x2) for x, x2 in zip(e, e2)]
    q = [x + y + _mm(x, y) for x, y in zip(f, e4)]
    return [x + _mm(y, x) for x, y in zip(dinv, q)]


def _gdn_kernel(qkv_ref, z_ref, ab_ref, s0_ref, alog_ref, dtb_ref, na_ref,
                mix_ref, sfin_ref, s_scr, *, tile):
    t = pl.program_id(1)

    @pl.when(t == 0)
    def _():
        s_scr[...] = s0_ref[...]

    ri = lax.broadcasted_iota(jnp.int32, (GDN_CHUNK, GDN_CHUNK), 0)
    ci = lax.broadcasted_iota(jnp.int32, (GDN_CHUNK, GDN_CHUNK), 1)
    m_incl = ri >= ci
    m_strict = ri > ci
    sub_mask = jnp.bitwise_xor(ri, ci) < GDN_SUB
    eye = jnp.where(ri == ci, 1.0, 0.0).astype(F32)
    n_chunks = tile // GDN_CHUNK
    pairs = [(b, h) for b in range(qkv_ref.shape[0]) for h in range(N_HEADS)]
    chains = [(p, c) for p in pairs for c in range(n_chunks)]

    def rows(c):
        return slice(c * GDN_CHUNK, (c + 1) * GDN_CHUNK)

    g_cum_t = {}
    qn, kn, kb, gc, eg, rhs = {}, {}, {}, {}, {}, {}
    for b in range(qkv_ref.shape[0]):
        ab = ab_ref[b]
        g_all = -jnp.exp(alog_ref[...]) * _softplus(ab + dtb_ref[...])
        sig_all = _sigmoid(ab)
        g_cum = _chunk_cumsum(g_all, GDN_CHUNK)
        g_cum_t[b] = g_cum.T
        for h in range(N_HEADS):
            p = (b, h)
            qn[p] = qkv_ref[b, :, h * HEAD_DIM:(h + 1) * HEAD_DIM]
            kn[p] = qkv_ref[b, :, MIX_HALF + h * HEAD_DIM:MIX_HALF + (h + 1) * HEAD_DIM]
            vh = qkv_ref[b, :, 2 * MIX_HALF + h * HEAD_DIM:2 * MIX_HALF + (h + 1) * HEAD_DIM]
            beta = sig_all[:, N_HEADS + h:N_HEADS + h + 1]
            gc[p] = g_cum[:, h:h + 1]
            eg[p] = jnp.exp(gc[p])
            kb[p] = kn[p] * beta
            rhs[p] = jnp.concatenate([vh * beta, kb[p] * eg[p]], axis=1)

    dec = {}
    for p, c in chains:
        gr = g_cum_t[p[0]][p[1]:p[1] + 1, rows(c)]
        dec[p, c] = jnp.exp(jnp.where(m_incl, gc[p][rows(c)] - gr, -jnp.inf))
    st = {(p, c): _mm_nt(jnp.concatenate([kb[p][rows(c)], qn[p][rows(c)]], axis=0), kn[p][rows(c)])
          for p, c in chains}
    a = [st[pc][:GDN_CHUNK] * jnp.where(m_strict, dec[pc], 0.0) for pc in chains]
    qk = {pc: st[pc][GDN_CHUNK:] * dec[pc] for pc in chains}
    tinv = dict(zip(chains, _unit_lower_inverse(a, sub_mask, eye)))
    uw = {(p, c): _mm(tinv[p, c], rhs[p][rows(c)]) for p, c in chains}

    s = {p: s_scr[p] for p in pairs}
    for c in range(n_chunks):
        rs = rows(c)
        r = {p: _mm(jnp.concatenate([uw[p, c][:, HEAD_DIM:], qn[p][rs] * eg[p][rs]], axis=0), s[p])
             for p in pairs}
        v_new = {p: uw[p, c][:, :HEAD_DIM] - r[p][:GDN_CHUNK] for p in pairs}
        for p in pairs:
            g_last = gc[p][(c + 1) * GDN_CHUNK - 1:(c + 1) * GDN_CHUNK, :]
            kd = kn[p][rs] * jnp.exp(g_last - gc[p][rs])
            s[p] = s[p] * jnp.exp(g_last) + _mm_tn(kd, v_new[p])
        for b, h in pairs:
            cs = slice(h * HEAD_DIM, (h + 1) * HEAD_DIM)
            o = r[b, h][GDN_CHUNK:] + _mm(qk[(b, h), c], v_new[b, h])
            mix_ref[b, rs, cs] = (_rms(o, na_ref[...]) * _silu(z_ref[b, rs, cs])).astype(mix_ref.dtype)
    for p in pairs:
        s_scr[p] = s[p]

    @pl.when(t == pl.num_programs(1) - 1)
    def _():
        sfin_ref[...] = s_scr[...]


MIXER_SEQS = 2
MIXER_TILE = 256


def _mixer_specs(cols, col_block):
    return pl.BlockSpec((MIXER_SEQS, MIXER_TILE, cols), lambda b, t: (b, t, col_block))


_STATE_BLOCK = (MIXER_SEQS, N_HEADS, HEAD_DIM, HEAD_DIM)


def _gdn_call(proj, s0, a_log_rows, dt_bias_rows, norm_a, *, layer):
    batch, seq_len, _ = proj.shape
    state_spec = pl.BlockSpec(_STATE_BLOCK, lambda b, t: (b, 0, 0, 0))
    return pl.pallas_call(
        functools.partial(_gdn_kernel, tile=MIXER_TILE),
        grid=(batch // MIXER_SEQS, seq_len // MIXER_TILE),
        in_specs=[
            _mixer_specs(CONV_DIM, COL_QKV // CONV_DIM),
            _mixer_specs(MIX_HALF, COL_Z // MIX_HALF),
            _mixer_specs(128, PROMPT_COL_AB // 128),
            state_spec,
            _layer_spec((1, 128), layer),
            _layer_spec((1, 128), layer),
            _layer_spec((1, HEAD_DIM), layer),
        ],
        out_specs=[_mixer_specs(MIX_HALF, 0), state_spec],
        out_shape=[
            jax.ShapeDtypeStruct((batch, seq_len, MIX_HALF), BF16),
            jax.ShapeDtypeStruct((batch, N_HEADS, HEAD_DIM, HEAD_DIM), F32),
        ],
        scratch_shapes=[pltpu.VMEM(_STATE_BLOCK, F32)],
        compiler_params=pltpu.CompilerParams(
            dimension_semantics=("arbitrary", "arbitrary"), vmem_limit_bytes=VMEM_LIMIT),
        name="gdn_prompt",
    )(proj, proj, proj, s0, a_log_rows, dt_bias_rows, norm_a)


def _forget_lower_bound(lbl, layer):
    m = jnp.max(lbl, axis=0, keepdims=True)
    e = jnp.exp(lbl - m)
    sm = e / jnp.sum(e, axis=0, keepdims=True)
    cs = sm[0:1, :]
    for i in range(1, layer + 1):
        cs = cs + sm[i:i + 1, :]
    return cs - sm[0:1, :]


def _midpoint_rows(g, s, row_id):
    n, width = g.shape
    if s >= 8:
        parts = [jnp.broadcast_to(g[b + s:b + s + 1, :], (2 * s, width))
                 for b in range(0, n, 2 * s)]
        return jnp.concatenate(parts, axis=0)
    tiles = g.reshape(n // 8, 8, width)
    sub = row_id.reshape(n // 8, 8, width) & 7
    mids = range(s, 8, 2 * s)
    out = jnp.broadcast_to(tiles[:, mids[-1]:mids[-1] + 1, :], tiles.shape)
    for m in reversed(mids[:-1]):
        out = jnp.where(sub < m + s, jnp.broadcast_to(tiles[:, m:m + 1, :], tiles.shape), out)
    return out.reshape(n, width)


def _hgrn_kernel(p_ref, g_ref, nb_ref, s0_ref, mix_ref, sfin_ref, st_scr, *, tile):
    t = pl.program_id(1)
    pairs = [(b, h) for b in range(p_ref.shape[0]) for h in range(N_HEADS)]

    @pl.when(t == 0)
    def _():
        for p in pairs:
            st_scr[p] = s0_ref[p].T

    n_chunks = tile // HGRN_CHUNK
    chains = [(p, c) for p in pairs for c in range(n_chunks)]

    def rows(c):
        return slice(c * HGRN_CHUNK, (c + 1) * HGRN_CHUNK)

    def cols(block, h):
        return slice(block * MIX_HALF + h * HEAD_DIM, block * MIX_HALF + (h + 1) * HEAD_DIM)

    row_id = lax.broadcasted_iota(jnp.int32, (tile, HEAD_DIM), 0)
    ri = lax.broadcasted_iota(jnp.int32, (HGRN_CHUNK, HGRN_CHUNK), 0)
    ci = lax.broadcasted_iota(jnp.int32, (HGRN_CHUNK, HGRN_CHUNK), 1)
    dist = jnp.bitwise_xor(ri, ci)
    lower = ri > ci

    q = {(b, h): p_ref[b, :, cols(0, h)] for b, h in pairs}
    k = {(b, h): p_ref[b, :, cols(1, h)] for b, h in pairs}
    g = {(b, h): g_ref[b, :, cols(0, h)] for b, h in pairs}

    a = {(p, c): jnp.where(ri == ci, _mm_nt(q[p][rows(c)], k[p][rows(c)]), 0.0) for p, c in chains}
    s = HGRN_CHUNK // 2
    while s >= 1:
        upper = (row_id & s) != 0
        level = lower & (dist >= s) & (dist < 2 * s)
        x = {}
        for p in pairs:
            d = g[p] - _midpoint_rows(g[p], s, row_id)
            x[p] = jnp.where(upper, q[p], k[p]) * jnp.exp(-jnp.abs(d))
        for p, c in chains:
            xc = x[p][rows(c)]
            a[p, c] = jnp.where(level, _mm_nt(xc, xc), a[p, c])
        s //= 2

    st = {p: st_scr[p] for p in pairs}
    for c in range(n_chunks):
        rs = rows(c)
        for b, h in pairs:
            p = (b, h)
            gc = g[p][rs]
            vc = p_ref[b, rs, cols(2, h)]
            o = _mm(a[p, c], vc) + _mm_nt(q[p][rs] * jnp.exp(gc), st[p])
            g_last = gc[HGRN_CHUNK - 1:HGRN_CHUNK, :]
            st[p] = st[p] * jnp.exp(g_last) + _mm_tn(vc, k[p][rs] * jnp.exp(g_last - gc))
            gate = _sigmoid(p_ref[b, rs, cols(3, h)])
            mix_ref[b, rs, cols(0, h)] = (_rms(o, nb_ref[...]) * gate).astype(mix_ref.dtype)
    for p in pairs:
        st_scr[p] = st[p]

    @pl.when(t == pl.num_programs(1) - 1)
    def _():
        for p in pairs:
            sfin_ref[p] = st_scr[p].T


def _hgrn_call(proj, norm_b, s0, *, layer):
    batch, seq_len, _ = proj.shape
    state_spec = pl.BlockSpec(_STATE_BLOCK, lambda b, t: (b, 0, 0, 0))
    return pl.pallas_call(
        functools.partial(_hgrn_kernel, tile=MIXER_TILE),
        grid=(batch // MIXER_SEQS, seq_len // MIXER_TILE),
        in_specs=[
            _mixer_specs(4 * MIX_HALF, COL_B // (4 * MIX_HALF)),
            _mixer_specs(MIX_HALF, PROMPT_COL_G // MIX_HALF),
            _layer_spec((1, HEAD_DIM), layer),
            state_spec,
        ],
        out_specs=[_mixer_specs(MIX_HALF, 0), state_spec],
        out_shape=[
            jax.ShapeDtypeStruct((batch, seq_len, MIX_HALF), BF16),
            jax.ShapeDtypeStruct((batch, N_HEADS, HEAD_DIM, HEAD_DIM), F32),
        ],
        scratch_shapes=[pltpu.VMEM(_STATE_BLOCK, F32)],
        compiler_params=pltpu.CompilerParams(
            dimension_semantics=("arbitrary", "arbitrary"), vmem_limit_bytes=VMEM_LIMIT),
        name="hgrn_prompt",
    )(proj, proj, norm_b, s0)


def _decode_kernel(*refs, bt, layer):
    (qkv_ref, z_ref, ab_ref, pb_ref, conv_ref, sd_ref, sh_ref,
     cw_ref, alog_ref, dtb_ref, na_ref, nb_ref, lbl_ref) = refs[:13]
    mix_ref, convo_ref, sdo_ref, sho_ref, oa_scr, ob_scr = refs[-6:]
    if layer > 0:
        prev_sd_ref, prev_sh_ref = refs[13:15]
        sdo_ref[0:layer] = prev_sd_ref[...]
        sho_ref[0:layer] = prev_sh_ref[...]
    cw = cw_ref[...]
    u = qkv_ref[...]
    b0 = conv_ref[:, 0:CONV_DIM]
    b1 = conv_ref[:, CONV_DIM:2 * CONV_DIM]
    b2 = conv_ref[:, 2 * CONV_DIM:3 * CONV_DIM]
    acc = b0 * cw[0:1, :]
    acc = acc + b1 * cw[1:2, :]
    acc = acc + b2 * cw[2:3, :]
    acc = acc + u * cw[3:4, :]
    conv = _silu(acc)
    convo_ref[:, 0:CONV_DIM] = b1
    convo_ref[:, CONV_DIM:2 * CONV_DIM] = b2
    convo_ref[:, 2 * CONV_DIM:3 * CONV_DIM] = u

    ab = ab_ref[...]
    eg_all = jnp.exp(-jnp.exp(alog_ref[...]) * _softplus(ab + dtb_ref[...]))
    beta_all = _sigmoid(ab)

    lb = _forget_lower_bound(lbl_ref[...], layer)
    f = lb + (1.0 - lb) * _sigmoid(pb_ref[:, MIX_HALF:2 * MIX_HALF])
    qb = _silu(pb_ref[:, 0:MIX_HALF])

    k_rows, q_rows, v_rows = [], [], []
    for h in range(N_HEADS):
        cs = slice(h * HEAD_DIM, (h + 1) * HEAD_DIM)
        qh = conv[:, cs]
        kh = conv[:, MIX_HALF + h * HEAD_DIM:MIX_HALF + (h + 1) * HEAD_DIM]
        q_rows.append(qh * lax.rsqrt(jnp.sum(qh * qh, axis=-1, keepdims=True) + EPS)
                      * (HEAD_DIM ** -0.5))
        k_rows.append(kh * lax.rsqrt(jnp.sum(kh * kh, axis=-1, keepdims=True) + EPS))
        v_rows.append(conv[:, 2 * MIX_HALF + h * HEAD_DIM:2 * MIX_HALF + (h + 1) * HEAD_DIM])
    f_rows = [f[:, h * HEAD_DIM:(h + 1) * HEAD_DIM] for h in range(N_HEADS)]
    qb_rows = [qb[:, h * HEAD_DIM:(h + 1) * HEAD_DIM] for h in range(N_HEADS)]

    stacked = jnp.concatenate(k_rows + q_rows + f_rows + qb_rows, axis=0)
    cols = stacked.T

    def col(item, h, b):
        i = (item * N_HEADS + h) * bt + b
        return cols[:, i:i + 1]

    for b in range(bt):
        for h in range(N_HEADS):
            cs = slice(h * HEAD_DIM, (h + 1) * HEAD_DIM)
            s = sd_ref[b, h]
            kc = col(0, h, b)
            eg = eg_all[b:b + 1, h:h + 1]
            beta = beta_all[b:b + 1, N_HEADS + h:N_HEADS + h + 1]
            sk = jnp.sum(s * kc, axis=0, keepdims=True)
            v_new = beta * (v_rows[h][b:b + 1, :] - eg * sk)
            s_new = s * eg + kc * v_new
            sdo_ref[layer, b, h] = s_new
            oa_scr[b:b + 1, cs] = jnp.sum(s_new * col(1, h, b), axis=0, keepdims=True)
            s = sh_ref[b, h]
            fc = col(2, h, b)
            vb = pb_ref[b:b + 1, 2 * MIX_HALF + h * HEAD_DIM:2 * MIX_HALF + (h + 1) * HEAD_DIM]
            s_new = s * fc + (1.0 - fc) * vb
            sho_ref[layer, b, h] = s_new
            ob_scr[b:b + 1, cs] = jnp.sum(s_new * col(3, h, b), axis=0, keepdims=True)

    for h in range(N_HEADS):
        cs = slice(h * HEAD_DIM, (h + 1) * HEAD_DIM)
        mix_ref[:, cs] = (_rms(oa_scr[:, cs], na_ref[...]) * _silu(z_ref[:, cs])).astype(mix_ref.dtype)
        gate = _sigmoid(pb_ref[:, 3 * MIX_HALF + h * HEAD_DIM:3 * MIX_HALF + (h + 1) * HEAD_DIM])
        mix_ref[:, MIX_HALF + h * HEAD_DIM:MIX_HALF + (h + 1) * HEAD_DIM] = (
            _rms(ob_scr[:, cs], nb_ref[...]) * gate).astype(mix_ref.dtype)


def _decode_call(proj, conv_state, s_delta, s_hgrn, prev_states, conv_w, a_log_rows, dt_bias_rows,
                 norm_a, norm_b, lb_logits, *, layer):
    batch = proj.shape[0]
    bt = 8
    kern = functools.partial(_decode_kernel, bt=bt, layer=layer)
    state_dims = (bt, N_HEADS, HEAD_DIM, HEAD_DIM)
    old_state_spec = pl.BlockSpec((None,) + state_dims, lambda i: (layer, i, 0, 0, 0))
    prev_state_spec = pl.BlockSpec((layer,) + state_dims, lambda i: (0, i, 0, 0, 0))
    new_state_spec = pl.BlockSpec((layer + 1,) + state_dims, lambda i: (0, i, 0, 0, 0))
    new_state_shape = jax.ShapeDtypeStruct((layer + 1, batch, N_HEADS, HEAD_DIM, HEAD_DIM), F32)
    return pl.pallas_call(
        kern,
        grid=(batch // bt,),
        in_specs=[
            pl.BlockSpec((bt, CONV_DIM), lambda i: (i, COL_QKV // CONV_DIM)),
            pl.BlockSpec((bt, MIX_HALF), lambda i: (i, COL_Z // MIX_HALF)),
            pl.BlockSpec((bt, 128), lambda i: (i, COL_AB // 128)),
            pl.BlockSpec((bt, 4 * MIX_HALF), lambda i: (i, COL_B // (4 * MIX_HALF))),
            pl.BlockSpec((None, bt, (CONV_W - 1) * CONV_DIM), lambda i: (layer, i, 0)),
            old_state_spec,
            old_state_spec,
            _layer_spec((CONV_W, CONV_DIM), layer),
            _layer_spec((1, 128), layer),
            _layer_spec((1, 128), layer),
            _layer_spec((1, HEAD_DIM), layer),
            _layer_spec((1, HEAD_DIM), layer),
            pl.BlockSpec((DEPTH, MIX_HALF), lambda i: (0, 0)),
        ] + [prev_state_spec] * len(prev_states),
        out_specs=[
            pl.BlockSpec((bt, 2 * MIX_HALF), lambda i: (i, 0)),
            pl.BlockSpec((bt, (CONV_W - 1) * CONV_DIM), lambda i: (i, 0)),
            new_state_spec,
            new_state_spec,
        ],
        out_shape=[
            jax.ShapeDtypeStruct((batch, 2 * MIX_HALF), BF16),
            jax.ShapeDtypeStruct((batch, (CONV_W - 1) * CONV_DIM), F32),
            new_state_shape,
            new_state_shape,
        ],
        scratch_shapes=[
            pltpu.VMEM((bt, MIX_HALF), F32),
            pltpu.VMEM((bt, MIX_HALF), F32),
        ],
        compiler_params=pltpu.CompilerParams(
            dimension_semantics=("arbitrary",), vmem_limit_bytes=VMEM_LIMIT),
        name="mixers_decode",
    )(proj, proj, proj, proj, conv_state, s_delta, s_hgrn, conv_w, a_log_rows, dt_bias_rows,
      norm_a, norm_b, lb_logits, *prev_states)


FF_CHUNK = 256


def _out_ffn_kernel(x_ref, ma_ref, mb_ref, g1_ref, sh2_ref, sc2_ref, g2_ref, wo_ref, n2_ref,
                    wg_ref, wu_ref, wd_ref, nf_ref, o_ref, acc_ref, *, final_norm, tiles_per_seq):
    gate1 = _mod_rows(g1_ref, tiles_per_seq)
    shift2 = _mod_rows(sh2_ref, tiles_per_seq)
    scale2 = _mod_rows(sc2_ref, tiles_per_seq)
    gate2 = _mod_rows(g2_ref, tiles_per_seq)
    mix = _mm(ma_ref[...], wo_ref[0:MIX_HALF, :]) + _mm(mb_ref[...], wo_ref[MIX_HALF:, :])
    x1 = x_ref[...] + gate1 * mix
    h = (_rms(x1, n2_ref[...]) * (1.0 + scale2) + shift2).astype(BF16)
    for c in range(D_FF // FF_CHUNK):
        fs = slice(c * FF_CHUNK, (c + 1) * FF_CHUNK)
        gate = jnp.dot(h, wg_ref[:, fs], preferred_element_type=F32)
        up = jnp.dot(h, wu_ref[:, fs], preferred_element_type=F32)
        part = _mm(_silu(gate) * up, wd_ref[fs, :])
        if c == 0:
            acc_ref[...] = part
        else:
            acc_ref[...] += part
    x2 = x1 + gate2 * acc_ref[...]
    if final_norm:
        x2 = _rms(x2, nf_ref[...])
    o_ref[...] = x2


def _out_ffn_call(x, mix_a, mix_b, mix_b_col, mods, w_out, norm2, w_gate, w_up, w_down, norm_f,
                  *, layer, n_prompt, per_token, seq_len, final_norm):
    rows = x.shape[0]
    tm = rows if per_token else 512
    tiles_per_seq = None if per_token else seq_len // tm
    resident = functools.partial(_layer_spec, layer=layer, pipeline_mode=pl.Buffered(1))
    mod = functools.partial(_mod_spec, mods, layer, n_prompt, per_token)
    kern = functools.partial(_out_ffn_kernel, final_norm=final_norm, tiles_per_seq=tiles_per_seq)
    return pl.pallas_call(
        kern,
        grid=(rows // tm,),
        in_specs=[
            pl.BlockSpec((tm, D_MODEL), lambda i: (i, 0)),
            pl.BlockSpec((tm, MIX_HALF), lambda i: (i, 0)),
            pl.BlockSpec((tm, MIX_HALF), lambda i: (i, mix_b_col)),
            mod(MOD_GATE1),
            mod(MOD_SHIFT2),
            mod(MOD_SCALE2),
            mod(MOD_GATE2),
            resident((D_MODEL, D_MODEL)),
            _layer_spec((1, D_MODEL), layer),
            resident((D_MODEL, D_FF)),
            resident((D_MODEL, D_FF)),
            resident((D_FF, D_MODEL)),
            pl.BlockSpec((1, D_MODEL), lambda i: (0, 0)),
        ],
        out_specs=pl.BlockSpec((tm, D_MODEL), lambda i: (i, 0)),
        out_shape=jax.ShapeDtypeStruct((rows, D_MODEL), F32),
        scratch_shapes=[pltpu.VMEM((tm, D_MODEL), F32)],
        compiler_params=pltpu.CompilerParams(
            dimension_semantics=("arbitrary",), vmem_limit_bytes=VMEM_LIMIT),
        name="out_ffn",
    )(x, mix_a, mix_b, mods, mods, mods, mods, w_out, norm2, w_gate, w_up, w_down, norm_f)


def _reorder_in_proj(w):
    qkvz = w[..., 0:COL_B]
    ab = w[..., COL_B:COL_B + 2 * N_HEADS]
    rest = w[..., COL_B + 2 * N_HEADS:]
    pad = jnp.zeros(w.shape[:-1] + (128 - 2 * N_HEADS,), w.dtype)
    return jnp.concatenate([qkvz, rest, ab, pad], axis=-1).astype(BF16)


def _lane_rows(v):
    return jnp.pad(v.astype(F32), ((0, 0), (0, 128 - v.shape[1])))[:, None, :]


def kernel(x_prompt, x_sample, c_prompt, c_sample, state_delta, state_conv, state_hgrn, w_ada, b_ada,
           norm1, w_in, conv_w, a_log, dt_bias, norm_a, norm_b, lb_logits, w_out, norm2, w_gate, w_up,
           w_down, norm_f):
    bp, seq_len, _ = x_prompt.shape
    bs = x_sample.shape[0]

    mods = _ada_call(jnp.concatenate([c_sample, c_prompt], axis=0), w_ada, b_ada)

    xp = x_prompt.reshape(bp * seq_len, D_MODEL)
    xs = x_sample.reshape(bs, D_MODEL)
    zero_buf = jnp.zeros((bp, 8, CONV_DIM), F32)
    zero_state = jnp.zeros((bp, N_HEADS, HEAD_DIM, HEAD_DIM), F32)

    w_in_b = _reorder_in_proj(w_in)
    w_out_b = w_out.astype(BF16)
    w_gate_b = w_gate.astype(BF16)
    w_up_b = w_up.astype(BF16)
    w_down_b = w_down.astype(BF16)
    a_log_rows = _lane_rows(a_log)
    dt_bias_rows = _lane_rows(dt_bias)
    norm1_r = norm1[:, None, :]
    norm2_r = norm2[:, None, :]
    norm_a_r = norm_a[:, None, :]
    norm_b_r = norm_b[:, None, :]
    norm_f_r = norm_f[None, :]
    conv_state = state_conv.reshape(DEPTH, bs, (CONV_W - 1) * CONV_DIM)

    delta_p, conv_p, hgrn_p, conv_s = [], [], [], []
    decode_states = ()
    for l in range(DEPTH):
        last = l == DEPTH - 1
        dense = dict(layer=l, n_prompt=bp)

        proj, cv = _in_proj_call(xp, mods, norm1_r, w_in_b, zero_buf, conv_w, lb_logits,
                                 per_token=False, seq_len=seq_len, **dense)
        proj = proj.reshape(bp, seq_len, PROMPT_COLS)
        mix_a, s_a = _gdn_call(proj, zero_state, a_log_rows, dt_bias_rows, norm_a_r, layer=l)
        mix_b, s_b = _hgrn_call(proj, norm_b_r, zero_state, layer=l)
        mix_a = mix_a.reshape(bp * seq_len, MIX_HALF)
        mix_b = mix_b.reshape(bp * seq_len, MIX_HALF)
        xp = _out_ffn_call(xp, mix_a, mix_b, 0, mods, w_out_b, norm2_r, w_gate_b, w_up_b, w_down_b,
                           norm_f_r, per_token=False, seq_len=seq_len, final_norm=last, **dense)
        delta_p.append(s_a)
        conv_p.append(cv)
        hgrn_p.append(s_b)

        proj = _in_proj_call(xs, mods, norm1_r, w_in_b, None, None, None,
                             per_token=True, seq_len=1, **dense)
        mix, cv, *decode_states = _decode_call(
            proj, conv_state, state_delta, state_hgrn, decode_states,
            conv_w, a_log_rows, dt_bias_rows, norm_a_r, norm_b_r, lb_logits, layer=l)
        xs = _out_ffn_call(xs, mix, mix, 1, mods, w_out_b, norm2_r, w_gate_b, w_up_b, w_down_b,
                           norm_f_r, per_token=True, seq_len=1, final_norm=last, **dense)
        conv_s.append(cv.reshape(bs, CONV_W - 1, CONV_DIM))

    delta_s, hgrn_s = decode_states
    return (xp.reshape(bp, seq_len, D_MODEL), xs.reshape(bs, 1, D_MODEL),
            jnp.stack(delta_p), jnp.stack(conv_p), jnp.stack(hgrn_p),
            delta_s, jnp.stack(conv_s), hgrn_s)
```

```python
import functools

import jax
import jax.numpy as jnp
from jax import lax
from jax.experimental import pallas as pl
from jax.experimental.pallas import tpu as pltpu

F32 = jnp.float32
BF16 = jnp.bfloat16

D_MODEL = 1024
DEPTH = 2
N_HEADS = 4
HEAD_DIM = 128
MIX_HALF = N_HEADS * HEAD_DIM
CONV_W = 4
CONV_DIM = 3 * MIX_HALF
D_FF = 2816
EPS = 1e-6
GDN_CHUNK = 128
GDN_SUB = 16
HGRN_CHUNK = 64

COL_QKV = 0
COL_Z = CONV_DIM
COL_B = COL_Z + MIX_HALF
COL_AB = COL_B + 4 * MIX_HALF
IN_COLS = COL_AB + 128
PROMPT_COL_G = COL_AB
PROMPT_COL_AB = PROMPT_COL_G + MIX_HALF
PROMPT_COLS = PROMPT_COL_AB + 128

MOD_SHIFT1, MOD_SCALE1, MOD_GATE1, MOD_SHIFT2, MOD_SCALE2, MOD_GATE2 = range(6)

VMEM_LIMIT = 56 * 1024 * 1024


def _sigmoid(x):
    return 1.0 / (1.0 + jnp.exp(-x))


def _silu(x):
    return x * _sigmoid(x)


def _softplus(x):
    return jnp.maximum(x, 0.0) + jnp.log1p(jnp.exp(-jnp.abs(x)))


def _mm(a, b):
    return jnp.dot(a.astype(BF16), b.astype(BF16), preferred_element_type=F32)


def _mm_nt(a, b):
    return lax.dot_general(a.astype(BF16), b.astype(BF16), (((1,), (1,)), ((), ())),
                           preferred_element_type=F32)


def _mm_tn(a, b):
    return lax.dot_general(a.astype(BF16), b.astype(BF16), (((0,), (0,)), ((), ())),
                           preferred_element_type=F32)


def _rms(x, w):
    return x * lax.rsqrt(jnp.mean(x * x, axis=-1, keepdims=True) + EPS) * w


def _chunk_cumsum(x, chunk):
    rows = lax.broadcasted_iota(jnp.int32, x.shape, 0) & (chunk - 1)
    s = 1
    while s < chunk:
        x = x + jnp.where(rows >= s, pltpu.roll(x, s, 0), 0.0)
        s *= 2
    return x


def _ada_kernel(c_ref, w_ref, b_ref, o_ref):
    a = _silu(c_ref[...])
    o_ref[...] = _mm(a, w_ref[...]) + b_ref[...]


def _ada_call(c_all, w_ada, b_ada):
    rows = c_all.shape[0]
    tn = 1024
    return pl.pallas_call(
        _ada_kernel,
        grid=(DEPTH, 6 * D_MODEL // tn),
        in_specs=[
            pl.BlockSpec((rows, D_MODEL), lambda l, j: (0, 0)),
            pl.BlockSpec((None, D_MODEL, tn), lambda l, j: (l, 0, j)),
            pl.BlockSpec((None, 1, tn), lambda l, j: (l, 0, j)),
        ],
        out_specs=pl.BlockSpec((None, rows, tn), lambda l, j: (l, 0, j)),
        out_shape=jax.ShapeDtypeStruct((DEPTH, rows, 6 * D_MODEL), F32),
        compiler_params=pltpu.CompilerParams(
            dimension_semantics=("arbitrary", "arbitrary"), vmem_limit_bytes=VMEM_LIMIT),
        name="ada_mod",
    )(c_all, w_ada, b_ada.reshape(DEPTH, 1, 6 * D_MODEL))


def _mod_spec(mods, layer, n_prompt, per_token, col):
    n_decode = mods.shape[1] - n_prompt
    if per_token:
        return pl.BlockSpec((None, n_decode, D_MODEL), lambda i: (layer, 0, col))
    return pl.BlockSpec((None, n_prompt, D_MODEL), lambda i: (layer, n_decode // n_prompt, col))


def _mod_rows(ref, tiles_per_seq):
    if tiles_per_seq is None:
        return ref[...]
    return ref[pl.ds(pl.program_id(0) // tiles_per_seq, 1), :]


def _layer_spec(shape, layer, **kwargs):
    zeros = (0,) * len(shape)
    return pl.BlockSpec((None,) + tuple(shape), lambda *_: (layer,) + zeros, **kwargs)


def _modulated_input(x_ref, scale_ref, shift_ref, nw_ref, tile, tiles_per_seq):
    if tiles_per_seq is None:
        scale, shift = scale_ref[...], shift_ref[...]
    else:
        seq = tile // tiles_per_seq
        scale, shift = scale_ref[pl.ds(seq, 1), :], shift_ref[pl.ds(seq, 1), :]
    return (_rms(x_ref[...], nw_ref[...]) * (1.0 + scale) + shift).astype(BF16)


def _in_proj_decode_kernel(x_ref, scale_ref, shift_ref, nw_ref, w_ref, o_ref):
    h = _modulated_input(x_ref, scale_ref, shift_ref, nw_ref, 0, None)
    o_ref[...] = jnp.dot(h, w_ref[...], preferred_element_type=F32)


def _in_proj_prompt_kernel(x_ref, scale_ref, shift_ref, nw_ref, w_ref, buf_ref, cw_ref, lbl_ref,
                           o_ref, cfin_ref, raw, h_scr, *, tiles_per_seq, layer):
    i = pl.program_id(0)
    tm = x_ref.shape[0]
    tile_in_seq = i % tiles_per_seq
    body = slice(8, 8 + tm)

    n_slabs = CONV_DIM // HEAD_DIM

    def slab_cols(ct):
        return slice(ct * HEAD_DIM, (ct + 1) * HEAD_DIM)

    @pl.when(tile_in_seq == 0)
    def _():
        for ct in range(n_slabs):
            raw[ct, 0:8, :] = buf_ref[:, slab_cols(ct)]

    h_scr[...] = _modulated_input(x_ref, scale_ref, shift_ref, nw_ref, i, tiles_per_seq)
    lb_all = _forget_lower_bound(lbl_ref[...], layer)

    def project(cs):
        return jnp.dot(h_scr[...], w_ref[:, cs], preferred_element_type=F32)

    def conv_slabs(cs):
        pre = project(cs)
        out = []
        for ct in range(cs.start // HEAD_DIM, cs.stop // HEAD_DIM):
            cols = slab_cols(ct)
            raw[ct, body, :] = pre[:, cols.start - cs.start:cols.stop - cs.start]
            acc = raw[ct, 5:5 + tm, :] * cw_ref[0:1, cols]
            for j in range(1, CONV_W):
                acc = acc + raw[ct, 5 + j:5 + j + tm, :] * cw_ref[j:j + 1, cols]
            out.append((cols, _silu(acc)))
        return out

    def l2_normalised(cs, scale):
        for cols, v in conv_slabs(cs):
            o_ref[:, cols] = v * lax.rsqrt(jnp.sum(v * v, axis=-1, keepdims=True) + EPS) * scale

    def post_q(cs):
        l2_normalised(cs, HEAD_DIM ** -0.5)

    def post_k(cs):
        l2_normalised(cs, 1.0)

    def post_v(cs):
        for cols, v in conv_slabs(cs):
            o_ref[:, cols] = v

    def post_copy(cs):
        o_ref[:, cs] = project(cs)

    def post_silu(cs):
        o_ref[:, cs] = _silu(project(cs))

    def post_forget(cs):
        gate_cols = slice(cs.start - COL_B - MIX_HALF, cs.stop - COL_B - MIX_HALF)
        lb = lb_all[:, gate_cols]
        f = lb + (1.0 - lb) * _sigmoid(project(cs))
        o_ref[:, cs] = 1.0 - f
        o_ref[:, PROMPT_COL_G + gate_cols.start:PROMPT_COL_G + gate_cols.stop] = (
            _chunk_cumsum(jnp.log(f), HGRN_CHUNK))

    width = 2 * HEAD_DIM
    pieces = ([post_q] * 2 + [post_k] * 2 + [post_v] * 2 + [post_copy] * 2
              + [post_silu] * 2 + [post_forget] * 2 + [post_copy] * 4)
    for piece, post in enumerate(pieces):
        post(slice(piece * width, (piece + 1) * width))
    o_ref[:, PROMPT_COL_AB:PROMPT_COLS] = project(slice(COL_AB, IN_COLS))

    for ct in range(n_slabs):
        raw[ct, 0:8, :] = raw[ct, tm:tm + 8, :]

    @pl.when(tile_in_seq == tiles_per_seq - 1)
    def _():
        for ct in range(n_slabs):
            cfin_ref[:, slab_cols(ct)] = raw[ct, 8 - (CONV_W - 1):8, :]


def _in_proj_call(x, mods, norm1, w_in, conv_buf, conv_w, lb_logits, *, layer, n_prompt, per_token,
                  seq_len):
    rows = x.shape[0]
    in_specs = [
        None,
        _mod_spec(mods, layer, n_prompt, per_token, MOD_SCALE1),
        _mod_spec(mods, layer, n_prompt, per_token, MOD_SHIFT1),
        _layer_spec((1, D_MODEL), layer),
        _layer_spec((D_MODEL, IN_COLS), layer, pipeline_mode=pl.Buffered(1)),
    ]
    params = pltpu.CompilerParams(dimension_semantics=("arbitrary",), vmem_limit_bytes=VMEM_LIMIT)
    if per_token:
        in_specs[0] = pl.BlockSpec((rows, D_MODEL), lambda i: (0, 0))
        return pl.pallas_call(
            _in_proj_decode_kernel,
            grid=(1,),
            in_specs=in_specs,
            out_specs=pl.BlockSpec((rows, IN_COLS), lambda i: (0, 0)),
            out_shape=jax.ShapeDtypeStruct((rows, IN_COLS), F32),
            compiler_params=params,
            name="in_proj_decode",
        )(x, mods, mods, norm1, w_in)

    tm = 512
    tiles_per_seq = seq_len // tm
    in_specs[0] = pl.BlockSpec((tm, D_MODEL), lambda i: (i, 0))
    in_specs += [
        pl.BlockSpec((None, 8, CONV_DIM), lambda i: (i // tiles_per_seq, 0, 0)),
        _layer_spec((CONV_W, CONV_DIM), layer),
        pl.BlockSpec((DEPTH, MIX_HALF), lambda i: (0, 0)),
    ]
    kern = functools.partial(_in_proj_prompt_kernel, tiles_per_seq=tiles_per_seq, layer=layer)
    return pl.pallas_call(
        kern,
        grid=(rows // tm,),
        in_specs=in_specs,
        out_specs=[
            pl.BlockSpec((tm, PROMPT_COLS), lambda i: (i, 0)),
            pl.BlockSpec((None, CONV_W - 1, CONV_DIM), lambda i: (i // tiles_per_seq, 0, 0)),
        ],
        out_shape=[
            jax.ShapeDtypeStruct((rows, PROMPT_COLS), F32),
            jax.ShapeDtypeStruct((n_prompt, CONV_W - 1, CONV_DIM), F32),
        ],
        scratch_shapes=[
            pltpu.VMEM((CONV_DIM // HEAD_DIM, tm + 8, HEAD_DIM), F32),
            pltpu.VMEM((tm, D_MODEL), BF16),
        ],
        compiler_params=params,
        name="in_proj_prompt",
    )(x, mods, mods, norm1, w_in, conv_buf, conv_w, lb_logits)


def _unit_lower_inverse(mats, sub_mask, eye):
    d = [jnp.where(sub_mask, a, 0.0) for a in mats]
    n = [a - x for a, x in zip(mats, d)]
    d2 = [_mm(x, x) for x in d]
    d4 = [_mm(x, x) for x in d2]
    d8 = [_mm(x, x) for x in d4]
    p = [eye - x + x2 - _mm(x, x2) for x, x2 in zip(d, d2)]
    p = [x + _mm(x, y) for x, y in zip(p, d4)]
    dinv = [x + _mm(x, y) for x, y in zip(p, d8)]
    e = [_mm(x, y) for x, y in zip(dinv, n)]
    e2 = [_mm(x, x) for x in e]
    e4 = [_mm(x, x) for x in e2]
    f = [x2 - x - _mm(x, x2) for x, x2 in zip(e, e2)]
    q = [x + y + _mm(x, y) for x, y in zip(f, e4)]
    return [x + _mm(y, x) for x, y in zip(dinv, q)]


def _gdn_kernel(qkv_ref, z_ref, ab_ref, s0_ref, alog_ref, dtb_ref, na_ref,
                mix_ref, sfin_ref, s_scr, *, tile):
    t = pl.program_id(1)

    @pl.when(t == 0)
    def _():
        s_scr[...] = s0_ref[...]

    ri = lax.broadcasted_iota(jnp.int32, (GDN_CHUNK, GDN_CHUNK), 0)
    ci = lax.broadcasted_iota(jnp.int32, (GDN_CHUNK, GDN_CHUNK), 1)
    m_incl = ri >= ci
    m_strict = ri > ci
    sub_mask = jnp.bitwise_xor(ri, ci) < GDN_SUB
    eye = jnp.where(ri == ci, 1.0, 0.0).astype(F32)
    n_chunks = tile // GDN_CHUNK
    pairs = [(b, h) for b in range(qkv_ref.shape[0]) for h in range(N_HEADS)]
    chains = [(p, c) for p in pairs for c in range(n_chunks)]

    def rows(c):
        return slice(c * GDN_CHUNK, (c + 1) * GDN_CHUNK)

    g_cum_t = {}
    qn, kn, kb, gc, eg, rhs = {}, {}, {}, {}, {}, {}
    for b in range(qkv_ref.shape[0]):
        ab = ab_ref[b]
        g_all = -jnp.exp(alog_ref[...]) * _softplus(ab + dtb_ref[...])
        sig_all = _sigmoid(ab)
        g_cum = _chunk_cumsum(g_all, GDN_CHUNK)
        g_cum_t[b] = g_cum.T
        for h in range(N_HEADS):
            p = (b, h)
            qn[p] = qkv_ref[b, :, h * HEAD_DIM:(h + 1) * HEAD_DIM]
            kn[p] = qkv_ref[b, :, MIX_HALF + h * HEAD_DIM:MIX_HALF + (h + 1) * HEAD_DIM]
            vh = qkv_ref[b, :, 2 * MIX_HALF + h * HEAD_DIM:2 * MIX_HALF + (h + 1) * HEAD_DIM]
            beta = sig_all[:, N_HEADS + h:N_HEADS + h + 1]
            gc[p] = g_cum[:, h:h + 1]
            eg[p] = jnp.exp(gc[p])
            kb[p] = kn[p] * beta
            rhs[p] = jnp.concatenate([vh * beta, kb[p] * eg[p]], axis=1)

    dec = {}
    for p, c in chains:
        gr = g_cum_t[p[0]][p[1]:p[1] + 1, rows(c)]
        dec[p, c] = jnp.exp(jnp.where(m_incl, gc[p][rows(c)] - gr, -jnp.inf))
    st = {(p, c): _mm_nt(jnp.concatenate([kb[p][rows(c)], qn[p][rows(c)]], axis=0), kn[p][rows(c)])
          for p, c in chains}
    a = [st[pc][:GDN_CHUNK] * jnp.where(m_strict, dec[pc], 0.0) for pc in chains]
    qk = {pc: st[pc][GDN_CHUNK:] * dec[pc] for pc in chains}
    tinv = dict(zip(chains, _unit_lower_inverse(a, sub_mask, eye)))
    uw = {(p, c): _mm(tinv[p, c], rhs[p][rows(c)]) for p, c in chains}

    s = {p: s_scr[p] for p in pairs}
    for c in range(n_chunks):
        rs = rows(c)
        r = {p: _mm(jnp.concatenate([uw[p, c][:, HEAD_DIM:], qn[p][rs] * eg[p][rs]], axis=0), s[p])
             for p in pairs}
        v_new = {p: uw[p, c][:, :HEAD_DIM] - r[p][:GDN_CHUNK] for p in pairs}
        for p in pairs:
            g_last = gc[p][(c + 1) * GDN_CHUNK - 1:(c + 1) * GDN_CHUNK, :]
            kd = kn[p][rs] * jnp.exp(g_last - gc[p][rs])
            s[p] = s[p] * jnp.exp(g_last) + _mm_tn(kd, v_new[p])
        for b, h in pairs:
            cs = slice(h * HEAD_DIM, (h + 1) * HEAD_DIM)
            o = r[b, h][GDN_CHUNK:] + _mm(qk[(b, h), c], v_new[b, h])
            mix_ref[b, rs, cs] = (_rms(o, na_ref[...]) * _silu(z_ref[b, rs, cs])).astype(mix_ref.dtype)
    for p in pairs:
        s_scr[p] = s[p]

    @pl.when(t == pl.num_programs(1) - 1)
    def _():
        sfin_ref[...] = s_scr[...]


MIXER_SEQS = 2
MIXER_TILE = 256


def _mixer_specs(cols, col_block):
    return pl.BlockSpec((MIXER_SEQS, MIXER_TILE, cols), lambda b, t: (b, t, col_block))


_STATE_BLOCK = (MIXER_SEQS, N_HEADS, HEAD_DIM, HEAD_DIM)


def _gdn_call(proj, s0, a_log_rows, dt_bias_rows, norm_a, *, layer):
    batch, seq_len, _ = proj.shape
    state_spec = pl.BlockSpec(_STATE_BLOCK, lambda b, t: (b, 0, 0, 0))
    return pl.pallas_call(
        functools.partial(_gdn_kernel, tile=MIXER_TILE),
        grid=(batch // MIXER_SEQS, seq_len // MIXER_TILE),
        in_specs=[
            _mixer_specs(CONV_DIM, COL_QKV // CONV_DIM),
            _mixer_specs(MIX_HALF, COL_Z // MIX_HALF),
            _mixer_specs(128, PROMPT_COL_AB // 128),
            state_spec,
            _layer_spec((1, 128), layer),
            _layer_spec((1, 128), layer),
            _layer_spec((1, HEAD_DIM), layer),
        ],
        out_specs=[_mixer_specs(MIX_HALF, 0), state_spec],
        out_shape=[
            jax.ShapeDtypeStruct((batch, seq_len, MIX_HALF), BF16),
            jax.ShapeDtypeStruct((batch, N_HEADS, HEAD_DIM, HEAD_DIM), F32),
        ],
        scratch_shapes=[pltpu.VMEM(_STATE_BLOCK, F32)],
        compiler_params=pltpu.CompilerParams(
            dimension_semantics=("arbitrary", "arbitrary"), vmem_limit_bytes=VMEM_LIMIT),
        name="gdn_prompt",
    )(proj, proj, proj, s0, a_log_rows, dt_bias_rows, norm_a)


def _forget_lower_bound(lbl, layer):
    m = jnp.max(lbl, axis=0, keepdims=True)
    e = jnp.exp(lbl - m)
    sm = e / jnp.sum(e, axis=0, keepdims=True)
    cs = sm[0:1, :]
    for i in range(1, layer + 1):
        cs = cs + sm[i:i + 1, :]
    return cs - sm[0:1, :]


def _midpoint_rows(g, s, row_id):
    n, width = g.shape
    if s >= 8:
        parts = [jnp.broadcast_to(g[b + s:b + s + 1, :], (2 * s, width))
                 for b in range(0, n, 2 * s)]
        return jnp.concatenate(parts, axis=0)
    tiles = g.reshape(n // 8, 8, width)
    sub = row_id.reshape(n // 8, 8, width) & 7
    mids = range(s, 8, 2 * s)
    out = jnp.broadcast_to(tiles[:, mids[-1]:mids[-1] + 1, :], tiles.shape)
    for m in reversed(mids[:-1]):
        out = jnp.where(sub < m + s, jnp.broadcast_to(tiles[:, m:m + 1, :], tiles.shape), out)
    return out.reshape(n, width)


def _hgrn_kernel(p_ref, g_ref, nb_ref, s0_ref, mix_ref, sfin_ref, st_scr, *, tile):
    t = pl.program_id(1)
    pairs = [(b, h) for b in range(p_ref.shape[0]) for h in range(N_HEADS)]

    @pl.when(t == 0)
    def _():
        for p in pairs:
            st_scr[p] = s0_ref[p].T

    n_chunks = tile // HGRN_CHUNK
    chains = [(p, c) for p in pairs for c in range(n_chunks)]

    def rows(c):
        return slice(c * HGRN_CHUNK, (c + 1) * HGRN_CHUNK)

    def cols(block, h):
        return slice(block * MIX_HALF + h * HEAD_DIM, block * MIX_HALF + (h + 1) * HEAD_DIM)

    row_id = lax.broadcasted_iota(jnp.int32, (tile, HEAD_DIM), 0)
    ri = lax.broadcasted_iota(jnp.int32, (HGRN_CHUNK, HGRN_CHUNK), 0)
    ci = lax.broadcasted_iota(jnp.int32, (HGRN_CHUNK, HGRN_CHUNK), 1)
    dist = jnp.bitwise_xor(ri, ci)
    lower = ri > ci

    q = {(b, h): p_ref[b, :, cols(0, h)] for b, h in pairs}
    k = {(b, h): p_ref[b, :, cols(1, h)] for b, h in pairs}
    g = {(b, h): g_ref[b, :, cols(0, h)] for b, h in pairs}

    a = {(p, c): jnp.where(ri == ci, _mm_nt(q[p][rows(c)], k[p][rows(c)]), 0.0) for p, c in chains}
    s = HGRN_CHUNK // 2
    while s >= 1:
        upper = (row_id & s) != 0
        level = lower & (dist >= s) & (dist < 2 * s)
        x = {}
        for p in pairs:
            d = g[p] - _midpoint_rows(g[p], s, row_id)
            x[p] = jnp.where(upper, q[p], k[p]) * jnp.exp(-jnp.abs(d))
        for p, c in chains:
            xc = x[p][rows(c)]
            a[p, c] = jnp.where(level, _mm_nt(xc, xc), a[p, c])
        s //= 2

    st = {p: st_scr[p] for p in pairs}
    for c in range(n_chunks):
        rs = rows(c)
        for b, h in pairs:
            p = (b, h)
            gc = g[p][rs]
            vc = p_ref[b, rs, cols(2, h)]
            o = _mm(a[p, c], vc) + _mm_nt(q[p][rs] * jnp.exp(gc), st[p])
            g_last = gc[HGRN_CHUNK - 1:HGRN_CHUNK, :]
            st[p] = st[p] * jnp.exp(g_last) + _mm_tn(vc, k[p][rs] * jnp.exp(g_last - gc))
            gate = _sigmoid(p_ref[b, rs, cols(3, h)])
            mix_ref[b, rs, cols(0, h)] = (_rms(o, nb_ref[...]) * gate).astype(mix_ref.dtype)
    for p in pairs:
        st_scr[p] = st[p]

    @pl.when(t == pl.num_programs(1) - 1)
    def _():
        for p in pairs:
            sfin_ref[p] = st_scr[p].T


def _hgrn_call(proj, norm_b, s0, *, layer):
    batch, seq_len, _ = proj.shape
    state_spec = pl.BlockSpec(_STATE_BLOCK, lambda b, t: (b, 0, 0, 0))
    return pl.pallas_call(
        functools.partial(_hgrn_kernel, tile=MIXER_TILE),
        grid=(batch // MIXER_SEQS, seq_len // MIXER_TILE),
        in_specs=[
            _mixer_specs(4 * MIX_HALF, COL_B // (4 * MIX_HALF)),
            _mixer_specs(MIX_HALF, PROMPT_COL_G // MIX_HALF),
            _layer_spec((1, HEAD_DIM), layer),
            state_spec,
        ],
        out_specs=[_mixer_specs(MIX_HALF, 0), state_spec],
        out_shape=[
            jax.ShapeDtypeStruct((batch, seq_len, MIX_HALF), BF16),
            jax.ShapeDtypeStruct((batch, N_HEADS, HEAD_DIM, HEAD_DIM), F32),
        ],
        scratch_shapes=[pltpu.VMEM(_STATE_BLOCK, F32)],
        compiler_params=pltpu.CompilerParams(
            dimension_semantics=("arbitrary", "arbitrary"), vmem_limit_bytes=VMEM_LIMIT),
        name="hgrn_prompt",
    )(proj, proj, norm_b, s0)


def _decode_kernel(*refs, bt, layer):
    (qkv_ref, z_ref, ab_ref, pb_ref, conv_ref, sd_ref, sh_ref,
     cw_ref, alog_ref, dtb_ref, na_ref, nb_ref, lbl_ref) = refs[:13]
    mix_ref, convo_ref, sdo_ref, sho_ref, oa_scr, ob_scr = refs[-6:]
    if layer > 0:
        prev_sd_ref, prev_sh_ref = refs[13:15]
        sdo_ref[0:layer] = prev_sd_ref[...]
        sho_ref[0:layer] = prev_sh_ref[...]
    cw = cw_ref[...]
    u = qkv_ref[...]
    b0 = conv_ref[:, 0:CONV_DIM]
    b1 = conv_ref[:, CONV_DIM:2 * CONV_DIM]
    b2 = conv_ref[:, 2 * CONV_DIM:3 * CONV_DIM]
    acc = b0 * cw[0:1, :]
    acc = acc + b1 * cw[1:2, :]
    acc = acc + b2 * cw[2:3, :]
    acc = acc + u * cw[3:4, :]
    conv = _silu(acc)
    convo_ref[:, 0:CONV_DIM] = b1
    convo_ref[:, CONV_DIM:2 * CONV_DIM] = b2
    convo_ref[:, 2 * CONV_DIM:3 * CONV_DIM] = u

    ab = ab_ref[...]
    eg_all = jnp.exp(-jnp.exp(alog_ref[...]) * _softplus(ab + dtb_ref[...]))
    beta_all = _sigmoid(ab)

    lb = _forget_lower_bound(lbl_ref[...], layer)
    f = lb + (1.0 - lb) * _sigmoid(pb_ref[:, MIX_HALF:2 * MIX_HALF])
    qb = _silu(pb_ref[:, 0:MIX_HALF])

    k_rows, q_rows, v_rows = [], [], []
    for h in range(N_HEADS):
        cs = slice(h * HEAD_DIM, (h + 1) * HEAD_DIM)
        qh = conv[:, cs]
        kh = conv[:, MIX_HALF + h * HEAD_DIM:MIX_HALF + (h + 1) * HEAD_DIM]
        q_rows.append(qh * lax.rsqrt(jnp.sum(qh * qh, axis=-1, keepdims=True) + EPS)
                      * (HEAD_DIM ** -0.5))
        k_rows.append(kh * lax.rsqrt(jnp.sum(kh * kh, axis=-1, keepdims=True) + EPS))
        v_rows.append(conv[:, 2 * MIX_HALF + h * HEAD_DIM:2 * MIX_HALF + (h + 1) * HEAD_DIM])
    f_rows = [f[:, h * HEAD_DIM:(h + 1) * HEAD_DIM] for h in range(N_HEADS)]
    qb_rows = [qb[:, h * HEAD_DIM:(h + 1) * HEAD_DIM] for h in range(N_HEADS)]
    qk_a = [jnp.sum(q_rows[h] * k_rows[h], axis=-1, keepdims=True) for h in range(N_HEADS)]
    qk_b = [jnp.sum(qb_rows[h] * (1.0 - f_rows[h]), axis=-1, keepdims=True) for h in range(N_HEADS)]

    f_cols = jnp.concatenate(f_rows + [jnp.zeros(((16 - N_HEADS) * bt, HEAD_DIM), F32)], axis=0).T

    row = lax.broadcasted_iota(jnp.int32, (8, HEAD_DIM), 0)

    def rows8(first, second=None):
        out = jnp.where(row == 0, first, 0.0)
        return out if second is None else jnp.where(row == 1, second, out)

    for b in range(bt):
        bs = slice(b, b + 1)
        for h in range(N_HEADS):
            cs = slice(h * HEAD_DIM, (h + 1) * HEAD_DIM)
            s = sd_ref[b, h]
            kq = rows8(k_rows[h][bs, :], q_rows[h][bs, :])
            eg = eg_all[bs, h:h + 1]
            beta = beta_all[bs, N_HEADS + h:N_HEADS + h + 1]
            r = _mm(kq, s)
            v_new = beta * (v_rows[h][bs, :] - eg * r[0:1, :])
            sdo_ref[layer, b, h] = s * eg + _mm_tn(kq, rows8(v_new))
            oa_scr[bs, cs] = eg * r[1:2, :] + qk_a[h][bs, :] * v_new
            s = sh_ref[b, h]
            fc = f_cols[:, h * bt + b:h * bt + b + 1]
            vb = pb_ref[bs, 2 * MIX_HALF + h * HEAD_DIM:2 * MIX_HALF + (h + 1) * HEAD_DIM]
            r = _mm(rows8(qb_rows[h][bs, :] * f_rows[h][bs, :]), s)
            sho_ref[layer, b, h] = s * fc + _mm_tn(rows8(1.0 - f_rows[h][bs, :]), rows8(vb))
            ob_scr[bs, cs] = r[0:1, :] + qk_b[h][bs, :] * vb

    for h in range(N_HEADS):
        cs = slice(h * HEAD_DIM, (h + 1) * HEAD_DIM)
        mix_ref[:, cs] = (_rms(oa_scr[:, cs], na_ref[...]) * _silu(z_ref[:, cs])).astype(mix_ref.dtype)
        gate = _sigmoid(pb_ref[:, 3 * MIX_HALF + h * HEAD_DIM:3 * MIX_HALF + (h + 1) * HEAD_DIM])
        mix_ref[:, MIX_HALF + h * HEAD_DIM:MIX_HALF + (h + 1) * HEAD_DIM] = (
            _rms(ob_scr[:, cs], nb_ref[...]) * gate).astype(mix_ref.dtype)


def _decode_call(proj, conv_state, s_delta, s_hgrn, prev_states, conv_w, a_log_rows, dt_bias_rows,
                 norm_a, norm_b, lb_logits, *, layer):
    batch = proj.shape[0]
    bt = 8
    kern = functools.partial(_decode_kernel, bt=bt, layer=layer)
    state_dims = (bt, N_HEADS, HEAD_DIM, HEAD_DIM)
    old_state_spec = pl.BlockSpec((None,) + state_dims, lambda i: (layer, i, 0, 0, 0))
    prev_state_spec = pl.BlockSpec((layer,) + state_dims, lambda i: (0, i, 0, 0, 0))
    new_state_spec = pl.BlockSpec((layer + 1,) + state_dims, lambda i: (0, i, 0, 0, 0))
    new_state_shape = jax.ShapeDtypeStruct((layer + 1, batch, N_HEADS, HEAD_DIM, HEAD_DIM), F32)
    return pl.pallas_call(
        kern,
        grid=(batch // bt,),
        in_specs=[
            pl.BlockSpec((bt, CONV_DIM), lambda i: (i, COL_QKV // CONV_DIM)),
            pl.BlockSpec((bt, MIX_HALF), lambda i: (i, COL_Z // MIX_HALF)),
            pl.BlockSpec((bt, 128), lambda i: (i, COL_AB // 128)),
            pl.BlockSpec((bt, 4 * MIX_HALF), lambda i: (i, COL_B // (4 * MIX_HALF))),
            pl.BlockSpec((None, bt, (CONV_W - 1) * CONV_DIM), lambda i: (layer, i, 0)),
            old_state_spec,
            old_state_spec,
            _layer_spec((CONV_W, CONV_DIM), layer),
            _layer_spec((1, 128), layer),
            _layer_spec((1, 128), layer),
            _layer_spec((1, HEAD_DIM), layer),
            _layer_spec((1, HEAD_DIM), layer),
            pl.BlockSpec((DEPTH, MIX_HALF), lambda i: (0, 0)),
        ] + [prev_state_spec] * len(prev_states),
        out_specs=[
            pl.BlockSpec((bt, 2 * MIX_HALF), lambda i: (i, 0)),
            pl.BlockSpec((bt, (CONV_W - 1) * CONV_DIM), lambda i: (i, 0)),
            new_state_spec,
            new_state_spec,
        ],
        out_shape=[
            jax.ShapeDtypeStruct((batch, 2 * MIX_HALF), BF16),
            jax.ShapeDtypeStruct((batch, (CONV_W - 1) * CONV_DIM), F32),
            new_state_shape,
            new_state_shape,
        ],
        scratch_shapes=[
            pltpu.VMEM((bt, MIX_HALF), F32),
            pltpu.VMEM((bt, MIX_HALF), F32),
        ],
        compiler_params=pltpu.CompilerParams(
            dimension_semantics=("arbitrary",), vmem_limit_bytes=VMEM_LIMIT),
        name="mixers_decode",
    )(proj, proj, proj, proj, conv_state, s_delta, s_hgrn, conv_w, a_log_rows, dt_bias_rows,
      norm_a, norm_b, lb_logits, *prev_states)


FF_CHUNK = 256


def _out_ffn_kernel(x_ref, ma_ref, mb_ref, g1_ref, sh2_ref, sc2_ref, g2_ref, wo_ref, n2_ref,
                    wg_ref, wu_ref, wd_ref, nf_ref, o_ref, acc_ref, *, final_norm, tiles_per_seq):
    gate1 = _mod_rows(g1_ref, tiles_per_seq)
    shift2 = _mod_rows(sh2_ref, tiles_per_seq)
    scale2 = _mod_rows(sc2_ref, tiles_per_seq)
    gate2 = _mod_rows(g2_ref, tiles_per_seq)
    mix = _mm(ma_ref[...], wo_ref[0:MIX_HALF, :]) + _mm(mb_ref[...], wo_ref[MIX_HALF:, :])
    x1 = x_ref[...] + gate1 * mix
    h = (_rms(x1, n2_ref[...]) * (1.0 + scale2) + shift2).astype(BF16)
    for c in range(D_FF // FF_CHUNK):
        fs = slice(c * FF_CHUNK, (c + 1) * FF_CHUNK)
        gate = jnp.dot(h, wg_ref[:, fs], preferred_element_type=F32)
        up = jnp.dot(h, wu_ref[:, fs], preferred_element_type=F32)
        part = _mm(_silu(gate) * up, wd_ref[fs, :])
        if c == 0:
            acc_ref[...] = part
        else:
            acc_ref[...] += part
    x2 = x1 + gate2 * acc_ref[...]
    if final_norm:
        x2 = _rms(x2, nf_ref[...])
    o_ref[...] = x2


def _out_ffn_call(x, mix_a, mix_b, mix_b_col, mods, w_out, norm2, w_gate, w_up, w_down, norm_f,
                  *, layer, n_prompt, per_token, seq_len, final_norm):
    rows = x.shape[0]
    tm = rows if per_token else 512
    tiles_per_seq = None if per_token else seq_len // tm
    resident = functools.partial(_layer_spec, layer=layer, pipeline_mode=pl.Buffered(1))
    mod = functools.partial(_mod_spec, mods, layer, n_prompt, per_token)
    kern = functools.partial(_out_ffn_kernel, final_norm=final_norm, tiles_per_seq=tiles_per_seq)
    return pl.pallas_call(
        kern,
        grid=(rows // tm,),
        in_specs=[
            pl.BlockSpec((tm, D_MODEL), lambda i: (i, 0)),
            pl.BlockSpec((tm, MIX_HALF), lambda i: (i, 0)),
            pl.BlockSpec((tm, MIX_HALF), lambda i: (i, mix_b_col)),
            mod(MOD_GATE1),
            mod(MOD_SHIFT2),
            mod(MOD_SCALE2),
            mod(MOD_GATE2),
            resident((D_MODEL, D_MODEL)),
            _layer_spec((1, D_MODEL), layer),
            resident((D_MODEL, D_FF)),
            resident((D_MODEL, D_FF)),
            resident((D_FF, D_MODEL)),
            pl.BlockSpec((1, D_MODEL), lambda i: (0, 0)),
        ],
        out_specs=pl.BlockSpec((tm, D_MODEL), lambda i: (i, 0)),
        out_shape=jax.ShapeDtypeStruct((rows, D_MODEL), F32),
        scratch_shapes=[pltpu.VMEM((tm, D_MODEL), F32)],
        compiler_params=pltpu.CompilerParams(
            dimension_semantics=("arbitrary",), vmem_limit_bytes=VMEM_LIMIT),
        name="out_ffn",
    )(x, mix_a, mix_b, mods, mods, mods, mods, w_out, norm2, w_gate, w_up, w_down, norm_f)


def _reorder_in_proj_kernel(w_ref, o_ref):
    n_ab = 2 * N_HEADS
    o_ref[:, 0:COL_B] = w_ref[:, 0:COL_B].astype(BF16)
    o_ref[:, COL_B:COL_AB] = w_ref[:, COL_B + n_ab:COL_AB + n_ab].astype(BF16)
    lane = lax.broadcasted_iota(jnp.int32, (w_ref.shape[0], 128), 1)
    o_ref[:, COL_AB:IN_COLS] = jnp.where(lane < n_ab, w_ref[:, COL_B:COL_B + 128], 0.0).astype(BF16)


def _reorder_in_proj(w):
    rows = 256
    return pl.pallas_call(
        _reorder_in_proj_kernel,
        grid=(DEPTH, D_MODEL // rows),
        in_specs=[pl.BlockSpec((None, rows, w.shape[-1]), lambda l, i: (l, i, 0))],
        out_specs=pl.BlockSpec((None, rows, IN_COLS), lambda l, i: (l, i, 0)),
        out_shape=jax.ShapeDtypeStruct((DEPTH, D_MODEL, IN_COLS), BF16),
        compiler_params=pltpu.CompilerParams(
            dimension_semantics=("arbitrary", "arbitrary"), vmem_limit_bytes=VMEM_LIMIT),
        name="reorder_in_proj",
    )(w)


def _lane_rows(v):
    return jnp.pad(v.astype(F32), ((0, 0), (0, 128 - v.shape[1])))[:, None, :]


def kernel(x_prompt, x_sample, c_prompt, c_sample, state_delta, state_conv, state_hgrn, w_ada, b_ada,
           norm1, w_in, conv_w, a_log, dt_bias, norm_a, norm_b, lb_logits, w_out, norm2, w_gate, w_up,
           w_down, norm_f):
    bp, seq_len, _ = x_prompt.shape
    bs = x_sample.shape[0]

    mods = _ada_call(jnp.concatenate([c_sample, c_prompt], axis=0), w_ada, b_ada)

    xp = x_prompt.reshape(bp * seq_len, D_MODEL)
    xs = x_sample.reshape(bs, D_MODEL)
    zero_buf = jnp.zeros((bp, 8, CONV_DIM), F32)
    zero_state = jnp.zeros((bp, N_HEADS, HEAD_DIM, HEAD_DIM), F32)

    w_in_b = _reorder_in_proj(w_in)
    w_out_b = w_out.astype(BF16)
    w_gate_b = w_gate.astype(BF16)
    w_up_b = w_up.astype(BF16)
    w_down_b = w_down.astype(BF16)
    a_log_rows = _lane_rows(a_log)
    dt_bias_rows = _lane_rows(dt_bias)
    norm1_r = norm1[:, None, :]
    norm2_r = norm2[:, None, :]
    norm_a_r = norm_a[:, None, :]
    norm_b_r = norm_b[:, None, :]
    norm_f_r = norm_f[None, :]
    conv_state = state_conv.reshape(DEPTH, bs, (CONV_W - 1) * CONV_DIM)

    delta_p, conv_p, hgrn_p, conv_s = [], [], [], []
    decode_states = ()
    for l in range(DEPTH):
        last = l == DEPTH - 1
        dense = dict(layer=l, n_prompt=bp)

        proj, cv = _in_proj_call(xp, mods, norm1_r, w_in_b, zero_buf, conv_w, lb_logits,
                                 per_token=False, seq_len=seq_len, **dense)
        proj = proj.reshape(bp, seq_len, PROMPT_COLS)
        mix_a, s_a = _gdn_call(proj, zero_state, a_log_rows, dt_bias_rows, norm_a_r, layer=l)
        mix_b, s_b = _hgrn_call(proj, norm_b_r, zero_state, layer=l)
        mix_a = mix_a.reshape(bp * seq_len, MIX_HALF)
        mix_b = mix_b.reshape(bp * seq_len, MIX_HALF)
        xp = _out_ffn_call(xp, mix_a, mix_b, 0, mods, w_out_b, norm2_r, w_gate_b, w_up_b, w_down_b,
                           norm_f_r, per_token=False, seq_len=seq_len, final_norm=last, **dense)
        delta_p.append(s_a)
        conv_p.append(cv)
        hgrn_p.append(s_b)

        proj = _in_proj_call(xs, mods, norm1_r, w_in_b, None, None, None,
                             per_token=True, seq_len=1, **dense)
        mix, cv, *decode_states = _decode_call(
            proj, conv_state, state_delta, state_hgrn, decode_states,
            conv_w, a_log_rows, dt_bias_rows, norm_a_r, norm_b_r, lb_logits, layer=l)
        xs = _out_ffn_call(xs, mix, mix, 1, mods, w_out_b, norm2_r, w_gate_b, w_up_b, w_down_b,
                           norm_f_r, per_token=True, seq_len=1, final_norm=last, **dense)
        conv_s.append(cv.reshape(bs, CONV_W - 1, CONV_DIM))

    delta_s, hgrn_s = decode_states
    return (xp.reshape(bp, seq_len, D_MODEL), xs.reshape(bs, 1, D_MODEL),
            jnp.stack(delta_p), jnp.stack(conv_p), jnp.stack(hgrn_p),
            delta_s, jnp.stack(conv_s), hgrn_s)
```

```python
import functools

import jax
import jax.numpy as jnp
from jax import lax
from jax.experimental import pallas as pl
from jax.experimental.pallas import tpu as pltpu

F32 = jnp.float32
BF16 = jnp.bfloat16

D_MODEL = 1024
DEPTH = 2
N_HEADS = 4
HEAD_DIM = 128
MIX_HALF = N_HEADS * HEAD_DIM
CONV_W = 4
CONV_DIM = 3 * MIX_HALF
D_FF = 2816
EPS = 1e-6
GDN_CHUNK = 128
GDN_SUB = 16
HGRN_CHUNK = 64

COL_QKV = 0
COL_Z = CONV_DIM
COL_B = COL_Z + MIX_HALF
COL_AB = COL_B + 4 * MIX_HALF
IN_COLS = COL_AB + 128
PROMPT_COL_G = COL_AB
PROMPT_COL_AB = PROMPT_COL_G + MIX_HALF
PROMPT_COLS = PROMPT_COL_AB + 128

MOD_SHIFT1, MOD_SCALE1, MOD_GATE1, MOD_SHIFT2, MOD_SCALE2, MOD_GATE2 = range(6)

VMEM_LIMIT = 56 * 1024 * 1024


def _sigmoid(x):
    return 1.0 / (1.0 + jnp.exp(-x))


def _silu(x):
    return x * _sigmoid(x)


def _softplus(x):
    return jnp.maximum(x, 0.0) + jnp.log1p(jnp.exp(-jnp.abs(x)))


def _mm(a, b):
    return jnp.dot(a.astype(BF16), b.astype(BF16), preferred_element_type=F32)


def _mm_nt(a, b):
    return lax.dot_general(a.astype(BF16), b.astype(BF16), (((1,), (1,)), ((), ())),
                           preferred_element_type=F32)


def _mm_tn(a, b):
    return lax.dot_general(a.astype(BF16), b.astype(BF16), (((0,), (0,)), ((), ())),
                           preferred_element_type=F32)


def _rms(x, w):
    return x * lax.rsqrt(jnp.mean(x * x, axis=-1, keepdims=True) + EPS) * w


def _chunk_cumsum(x, chunk):
    rows = lax.broadcasted_iota(jnp.int32, x.shape, 0) & (chunk - 1)
    s = 1
    while s < chunk:
        x = x + jnp.where(rows >= s, pltpu.roll(x, s, 0), 0.0)
        s *= 2
    return x


def _ada_kernel(c_ref, w_ref, b_ref, o_ref):
    a = _silu(c_ref[...])
    o_ref[...] = _mm(a, w_ref[...]) + b_ref[...]


def _ada_call(c_all, w_ada, b_ada):
    rows = c_all.shape[0]
    tn = 1024
    return pl.pallas_call(
        _ada_kernel,
        grid=(DEPTH, 6 * D_MODEL // tn),
        in_specs=[
            pl.BlockSpec((rows, D_MODEL), lambda l, j: (0, 0)),
            pl.BlockSpec((None, D_MODEL, tn), lambda l, j: (l, 0, j)),
            pl.BlockSpec((None, 1, tn), lambda l, j: (l, 0, j)),
        ],
        out_specs=pl.BlockSpec((None, rows, tn), lambda l, j: (l, 0, j)),
        out_shape=jax.ShapeDtypeStruct((DEPTH, rows, 6 * D_MODEL), F32),
        compiler_params=pltpu.CompilerParams(
            dimension_semantics=("arbitrary", "arbitrary"), vmem_limit_bytes=VMEM_LIMIT),
        name="ada_mod",
    )(c_all, w_ada, b_ada.reshape(DEPTH, 1, 6 * D_MODEL))


def _mod_spec(mods, layer, n_prompt, per_token, col):
    n_decode = mods.shape[1] - n_prompt
    if per_token:
        return pl.BlockSpec((None, n_decode, D_MODEL), lambda i: (layer, 0, col))
    return pl.BlockSpec((None, n_prompt, D_MODEL), lambda i: (layer, n_decode // n_prompt, col))


def _mod_rows(ref, tiles_per_seq):
    if tiles_per_seq is None:
        return ref[...]
    return ref[pl.ds(pl.program_id(0) // tiles_per_seq, 1), :]


def _layer_spec(shape, layer, **kwargs):
    zeros = (0,) * len(shape)
    return pl.BlockSpec((None,) + tuple(shape), lambda *_: (layer,) + zeros, **kwargs)


def _modulated_input(x_ref, scale_ref, shift_ref, nw_ref, tile, tiles_per_seq):
    if tiles_per_seq is None:
        scale, shift = scale_ref[...], shift_ref[...]
    else:
        seq = tile // tiles_per_seq
        scale, shift = scale_ref[pl.ds(seq, 1), :], shift_ref[pl.ds(seq, 1), :]
    return (_rms(x_ref[...], nw_ref[...]) * (1.0 + scale) + shift).astype(BF16)


def _in_proj_decode_kernel(x_ref, scale_ref, shift_ref, nw_ref, w_ref, o_ref):
    h = _modulated_input(x_ref, scale_ref, shift_ref, nw_ref, 0, None)
    o_ref[...] = jnp.dot(h, w_ref[...], preferred_element_type=F32)


def _in_proj_prompt_kernel(x_ref, scale_ref, shift_ref, nw_ref, w_ref, buf_ref, cw_ref, lbl_ref,
                           o_ref, cfin_ref, raw, h_scr, *, tiles_per_seq, layer):
    i = pl.program_id(0)
    tm = x_ref.shape[0]
    tile_in_seq = i % tiles_per_seq
    body = slice(8, 8 + tm)

    n_slabs = CONV_DIM // HEAD_DIM

    def slab_cols(ct):
        return slice(ct * HEAD_DIM, (ct + 1) * HEAD_DIM)

    @pl.when(tile_in_seq == 0)
    def _():
        for ct in range(n_slabs):
            raw[ct, 0:8, :] = buf_ref[:, slab_cols(ct)]

    h_scr[...] = _modulated_input(x_ref, scale_ref, shift_ref, nw_ref, i, tiles_per_seq)
    lb_all = _forget_lower_bound(lbl_ref[...], layer)

    def project(cs):
        return jnp.dot(h_scr[...], w_ref[:, cs], preferred_element_type=F32)

    def conv_slabs(cs):
        pre = project(cs)
        out = []
        for ct in range(cs.start // HEAD_DIM, cs.stop // HEAD_DIM):
            cols = slab_cols(ct)
            raw[ct, body, :] = pre[:, cols.start - cs.start:cols.stop - cs.start]
            acc = raw[ct, 5:5 + tm, :] * cw_ref[0:1, cols]
            for j in range(1, CONV_W):
                acc = acc + raw[ct, 5 + j:5 + j + tm, :] * cw_ref[j:j + 1, cols]
            out.append((cols, _silu(acc)))
        return out

    def l2_normalised(cs, scale):
        for cols, v in conv_slabs(cs):
            o_ref[:, cols] = v * lax.rsqrt(jnp.sum(v * v, axis=-1, keepdims=True) + EPS) * scale

    def post_q(cs):
        l2_normalised(cs, HEAD_DIM ** -0.5)

    def post_k(cs):
        l2_normalised(cs, 1.0)

    def post_v(cs):
        for cols, v in conv_slabs(cs):
            o_ref[:, cols] = v

    def post_copy(cs):
        o_ref[:, cs] = project(cs)

    def post_silu(cs):
        o_ref[:, cs] = _silu(project(cs))

    def post_forget(cs):
        gate_cols = slice(cs.start - COL_B - MIX_HALF, cs.stop - COL_B - MIX_HALF)
        lb = lb_all[:, gate_cols]
        f = lb + (1.0 - lb) * _sigmoid(project(cs))
        o_ref[:, cs] = 1.0 - f
        o_ref[:, PROMPT_COL_G + gate_cols.start:PROMPT_COL_G + gate_cols.stop] = (
            _chunk_cumsum(jnp.log(f), HGRN_CHUNK))

    width = 2 * HEAD_DIM
    pieces = ([post_q] * 2 + [post_k] * 2 + [post_v] * 2 + [post_copy] * 2
              + [post_silu] * 2 + [post_forget] * 2 + [post_copy] * 4)
    for piece, post in enumerate(pieces):
        post(slice(piece * width, (piece + 1) * width))
    o_ref[:, PROMPT_COL_AB:PROMPT_COLS] = project(slice(COL_AB, IN_COLS))

    for ct in range(n_slabs):
        raw[ct, 0:8, :] = raw[ct, tm:tm + 8, :]

    @pl.when(tile_in_seq == tiles_per_seq - 1)
    def _():
        for ct in range(n_slabs):
            cfin_ref[:, slab_cols(ct)] = raw[ct, 8 - (CONV_W - 1):8, :]


def _in_proj_call(x, mods, norm1, w_in, conv_buf, conv_w, lb_logits, *, layer, n_prompt, per_token,
                  seq_len):
    rows = x.shape[0]
    in_specs = [
        None,
        _mod_spec(mods, layer, n_prompt, per_token, MOD_SCALE1),
        _mod_spec(mods, layer, n_prompt, per_token, MOD_SHIFT1),
        _layer_spec((1, D_MODEL), layer),
        _layer_spec((D_MODEL, IN_COLS), layer, pipeline_mode=pl.Buffered(1)),
    ]
    params = pltpu.CompilerParams(dimension_semantics=("arbitrary",), vmem_limit_bytes=VMEM_LIMIT)
    if per_token:
        in_specs[0] = pl.BlockSpec((rows, D_MODEL), lambda i: (0, 0))
        return pl.pallas_call(
            _in_proj_decode_kernel,
            grid=(1,),
            in_specs=in_specs,
            out_specs=pl.BlockSpec((rows, IN_COLS), lambda i: (0, 0)),
            out_shape=jax.ShapeDtypeStruct((rows, IN_COLS), F32),
            compiler_params=params,
            name="in_proj_decode",
        )(x, mods, mods, norm1, w_in)

    tm = 512
    tiles_per_seq = seq_len // tm
    in_specs[0] = pl.BlockSpec((tm, D_MODEL), lambda i: (i, 0))
    in_specs += [
        pl.BlockSpec((None, 8, CONV_DIM), lambda i: (i // tiles_per_seq, 0, 0)),
        _layer_spec((CONV_W, CONV_DIM), layer),
        pl.BlockSpec((DEPTH, MIX_HALF), lambda i: (0, 0)),
    ]
    kern = functools.partial(_in_proj_prompt_kernel, tiles_per_seq=tiles_per_seq, layer=layer)
    return pl.pallas_call(
        kern,
        grid=(rows // tm,),
        in_specs=in_specs,
        out_specs=[
            pl.BlockSpec((tm, PROMPT_COLS), lambda i: (i, 0)),
            pl.BlockSpec((None, CONV_W - 1, CONV_DIM), lambda i: (i // tiles_per_seq, 0, 0)),
        ],
        out_shape=[
            jax.ShapeDtypeStruct((rows, PROMPT_COLS), F32),
            jax.ShapeDtypeStruct((n_prompt, CONV_W - 1, CONV_DIM), F32),
        ],
        scratch_shapes=[
            pltpu.VMEM((CONV_DIM // HEAD_DIM, tm + 8, HEAD_DIM), F32),
            pltpu.VMEM((tm, D_MODEL), BF16),
        ],
        compiler_params=params,
        name="in_proj_prompt",
    )(x, mods, mods, norm1, w_in, conv_buf, conv_w, lb_logits)


def _unit_lower_inverse(mats, sub_mask, eye):
    d = [jnp.where(sub_mask, a, 0.0) for a in mats]
    n = [a - x for a, x in zip(mats, d)]
    d2 = [_mm(x, x) for x in d]
    d4 = [_mm(x, x) for x in d2]
    d8 = [_mm(x, x) for x in d4]
    p = [eye - x + x2 - _mm(x, x2) for x, x2 in zip(d, d2)]
    p = [x + _mm(x, y) for x, y in zip(p, d4)]
    dinv = [x + _mm(x, y) for x, y in zip(p, d8)]
    e = [_mm(x, y) for x, y in zip(dinv, n)]
    e2 = [_mm(x, x) for x in e]
    e4 = [_mm(x, x) for x in e2]
    f = [x2 - x - _mm(x, x2) for x, x2 in zip(e, e2)]
    q = [x + y + _mm(x, y) for x, y in zip(f, e4)]
    return [x + _mm(y, x) for x, y in zip(dinv, q)]


def _gdn_kernel(qkv_ref, z_ref, ab_ref, s0_ref, alog_ref, dtb_ref, na_ref,
                mix_ref, sfin_ref, s_scr, *, tile):
    t = pl.program_id(1)

    @pl.when(t == 0)
    def _():
        s_scr[...] = s0_ref[...]

    ri = lax.broadcasted_iota(jnp.int32, (GDN_CHUNK, GDN_CHUNK), 0)
    ci = lax.broadcasted_iota(jnp.int32, (GDN_CHUNK, GDN_CHUNK), 1)
    m_incl = ri >= ci
    m_strict = ri > ci
    sub_mask = jnp.bitwise_xor(ri, ci) < GDN_SUB
    eye = jnp.where(ri == ci, 1.0, 0.0).astype(F32)
    n_chunks = tile // GDN_CHUNK
    pairs = [(b, h) for b in range(qkv_ref.shape[0]) for h in range(N_HEADS)]
    chains = [(p, c) for p in pairs for c in range(n_chunks)]

    def rows(c):
        return slice(c * GDN_CHUNK, (c + 1) * GDN_CHUNK)

    g_cum_t = {}
    qn, kn, kb, gc, eg, rhs = {}, {}, {}, {}, {}, {}
    for b in range(qkv_ref.shape[0]):
        ab = ab_ref[b]
        g_all = -jnp.exp(alog_ref[...]) * _softplus(ab + dtb_ref[...])
        sig_all = _sigmoid(ab)
        g_cum = _chunk_cumsum(g_all, GDN_CHUNK)
        g_cum_t[b] = g_cum.T
        for h in range(N_HEADS):
            p = (b, h)
            qn[p] = qkv_ref[b, :, h * HEAD_DIM:(h + 1) * HEAD_DIM]
            kn[p] = qkv_ref[b, :, MIX_HALF + h * HEAD_DIM:MIX_HALF + (h + 1) * HEAD_DIM]
            vh = qkv_ref[b, :, 2 * MIX_HALF + h * HEAD_DIM:2 * MIX_HALF + (h + 1) * HEAD_DIM]
            beta = sig_all[:, N_HEADS + h:N_HEADS + h + 1]
            gc[p] = g_cum[:, h:h + 1]
            eg[p] = jnp.exp(gc[p])
            kb[p] = kn[p] * beta
            rhs[p] = jnp.concatenate([vh * beta, kb[p] * eg[p]], axis=1)

    dec = {}
    for p, c in chains:
        gr = g_cum_t[p[0]][p[1]:p[1] + 1, rows(c)]
        dec[p, c] = jnp.exp(jnp.where(m_incl, gc[p][rows(c)] - gr, -jnp.inf))
    st = {(p, c): _mm_nt(jnp.concatenate([kb[p][rows(c)], qn[p][rows(c)]], axis=0), kn[p][rows(c)])
          for p, c in chains}
    a = [st[pc][:GDN_CHUNK] * jnp.where(m_strict, dec[pc], 0.0) for pc in chains]
    qk = {pc: st[pc][GDN_CHUNK:] * dec[pc] for pc in chains}
    tinv = dict(zip(chains, _unit_lower_inverse(a, sub_mask, eye)))
    uw = {(p, c): _mm(tinv[p, c], rhs[p][rows(c)]) for p, c in chains}

    s = {p: s_scr[p] for p in pairs}
    for c in range(n_chunks):
        rs = rows(c)
        r = {p: _mm(jnp.concatenate([uw[p, c][:, HEAD_DIM:], qn[p][rs] * eg[p][rs]], axis=0), s[p])
             for p in pairs}
        v_new = {p: uw[p, c][:, :HEAD_DIM] - r[p][:GDN_CHUNK] for p in pairs}
        for p in pairs:
            g_last = gc[p][(c + 1) * GDN_CHUNK - 1:(c + 1) * GDN_CHUNK, :]
            kd = kn[p][rs] * jnp.exp(g_last - gc[p][rs])
            s[p] = s[p] * jnp.exp(g_last) + _mm_tn(kd, v_new[p])
        for b, h in pairs:
            cs = slice(h * HEAD_DIM, (h + 1) * HEAD_DIM)
            o = r[b, h][GDN_CHUNK:] + _mm(qk[(b, h), c], v_new[b, h])
            mix_ref[b, rs, cs] = (_rms(o, na_ref[...]) * _silu(z_ref[b, rs, cs])).astype(mix_ref.dtype)
    for p in pairs:
        s_scr[p] = s[p]

    @pl.when(t == pl.num_programs(1) - 1)
    def _():
        sfin_ref[...] = s_scr[...]


MIXER_SEQS = 2
MIXER_TILE = 256


def _mixer_specs(cols, col_block):
    return pl.BlockSpec((MIXER_SEQS, MIXER_TILE, cols), lambda b, t: (b, t, col_block))


_STATE_BLOCK = (MIXER_SEQS, N_HEADS, HEAD_DIM, HEAD_DIM)


def _gdn_call(proj, s0, a_log_rows, dt_bias_rows, norm_a, *, layer):
    batch, seq_len, _ = proj.shape
    state_spec = pl.BlockSpec(_STATE_BLOCK, lambda b, t: (b, 0, 0, 0))
    return pl.pallas_call(
        functools.partial(_gdn_kernel, tile=MIXER_TILE),
        grid=(batch // MIXER_SEQS, seq_len // MIXER_TILE),
        in_specs=[
            _mixer_specs(CONV_DIM, COL_QKV // CONV_DIM),
            _mixer_specs(MIX_HALF, COL_Z // MIX_HALF),
            _mixer_specs(128, PROMPT_COL_AB // 128),
            state_spec,
            _layer_spec((1, 128), layer),
            _layer_spec((1, 128), layer),
            _layer_spec((1, HEAD_DIM), layer),
        ],
        out_specs=[_mixer_specs(MIX_HALF, 0), state_spec],
        out_shape=[
            jax.ShapeDtypeStruct((batch, seq_len, MIX_HALF), BF16),
            jax.ShapeDtypeStruct((batch, N_HEADS, HEAD_DIM, HEAD_DIM), F32),
        ],
        scratch_shapes=[pltpu.VMEM(_STATE_BLOCK, F32)],
        compiler_params=pltpu.CompilerParams(
            dimension_semantics=("arbitrary", "arbitrary"), vmem_limit_bytes=VMEM_LIMIT),
        name="gdn_prompt",
    )(proj, proj, proj, s0, a_log_rows, dt_bias_rows, norm_a)


def _forget_lower_bound(lbl, layer):
    m = jnp.max(lbl, axis=0, keepdims=True)
    e = jnp.exp(lbl - m)
    sm = e / jnp.sum(e, axis=0, keepdims=True)
    cs = sm[0:1, :]
    for i in range(1, layer + 1):
        cs = cs + sm[i:i + 1, :]
    return cs - sm[0:1, :]


def _midpoint_rows(g, s, row_id):
    n, width = g.shape
    if s >= 8:
        parts = [jnp.broadcast_to(g[b + s:b + s + 1, :], (2 * s, width))
                 for b in range(0, n, 2 * s)]
        return jnp.concatenate(parts, axis=0)
    tiles = g.reshape(n // 8, 8, width)
    sub = row_id.reshape(n // 8, 8, width) & 7
    mids = range(s, 8, 2 * s)
    out = jnp.broadcast_to(tiles[:, mids[-1]:mids[-1] + 1, :], tiles.shape)
    for m in reversed(mids[:-1]):
        out = jnp.where(sub < m + s, jnp.broadcast_to(tiles[:, m:m + 1, :], tiles.shape), out)
    return out.reshape(n, width)


def _hgrn_kernel(p_ref, g_ref, nb_ref, s0_ref, mix_ref, sfin_ref, st_scr, *, tile):
    t = pl.program_id(1)
    pairs = [(b, h) for b in range(p_ref.shape[0]) for h in range(N_HEADS)]

    @pl.when(t == 0)
    def _():
        for p in pairs:
            st_scr[p] = s0_ref[p].T

    n_chunks = tile // HGRN_CHUNK
    chains = [(p, c) for p in pairs for c in range(n_chunks)]

    def rows(c):
        return slice(c * HGRN_CHUNK, (c + 1) * HGRN_CHUNK)

    def cols(block, h):
        return slice(block * MIX_HALF + h * HEAD_DIM, block * MIX_HALF + (h + 1) * HEAD_DIM)

    row_id = lax.broadcasted_iota(jnp.int32, (tile, HEAD_DIM), 0)
    ri = lax.broadcasted_iota(jnp.int32, (HGRN_CHUNK, HGRN_CHUNK), 0)
    ci = lax.broadcasted_iota(jnp.int32, (HGRN_CHUNK, HGRN_CHUNK), 1)
    dist = jnp.bitwise_xor(ri, ci)
    lower = ri > ci

    q = {(b, h): p_ref[b, :, cols(0, h)] for b, h in pairs}
    k = {(b, h): p_ref[b, :, cols(1, h)] for b, h in pairs}
    g = {(b, h): g_ref[b, :, cols(0, h)] for b, h in pairs}

    a = {(p, c): jnp.where(ri == ci, _mm_nt(q[p][rows(c)], k[p][rows(c)]), 0.0) for p, c in chains}
    s = HGRN_CHUNK // 2
    while s >= 1:
        upper = (row_id & s) != 0
        level = lower & (dist >= s) & (dist < 2 * s)
        x = {}
        for p in pairs:
            d = g[p] - _midpoint_rows(g[p], s, row_id)
            x[p] = jnp.where(upper, q[p], k[p]) * jnp.exp(-jnp.abs(d))
        for p, c in chains:
            xc = x[p][rows(c)]
            a[p, c] = jnp.where(level, _mm_nt(xc, xc), a[p, c])
        s //= 2

    st = {p: st_scr[p] for p in pairs}
    for c in range(n_chunks):
        rs = rows(c)
        for b, h in pairs:
            p = (b, h)
            gc = g[p][rs]
            vc = p_ref[b, rs, cols(2, h)]
            o = _mm(a[p, c], vc) + _mm_nt(q[p][rs] * jnp.exp(gc), st[p])
            g_last = gc[HGRN_CHUNK - 1:HGRN_CHUNK, :]
            st[p] = st[p] * jnp.exp(g_last) + _mm_tn(vc, k[p][rs] * jnp.exp(g_last - gc))
            gate = _sigmoid(p_ref[b, rs, cols(3, h)])
            mix_ref[b, rs, cols(0, h)] = (_rms(o, nb_ref[...]) * gate).astype(mix_ref.dtype)
    for p in pairs:
        st_scr[p] = st[p]

    @pl.when(t == pl.num_programs(1) - 1)
    def _():
        for p in pairs:
            sfin_ref[p] = st_scr[p].T


def _hgrn_call(proj, norm_b, s0, *, layer):
    batch, seq_len, _ = proj.shape
    state_spec = pl.BlockSpec(_STATE_BLOCK, lambda b, t: (b, 0, 0, 0))
    return pl.pallas_call(
        functools.partial(_hgrn_kernel, tile=MIXER_TILE),
        grid=(batch // MIXER_SEQS, seq_len // MIXER_TILE),
        in_specs=[
            _mixer_specs(4 * MIX_HALF, COL_B // (4 * MIX_HALF)),
            _mixer_specs(MIX_HALF, PROMPT_COL_G // MIX_HALF),
            _layer_spec((1, HEAD_DIM), layer),
            state_spec,
        ],
        out_specs=[_mixer_specs(MIX_HALF, 0), state_spec],
        out_shape=[
            jax.ShapeDtypeStruct((batch, seq_len, MIX_HALF), BF16),
            jax.ShapeDtypeStruct((batch, N_HEADS, HEAD_DIM, HEAD_DIM), F32),
        ],
        scratch_shapes=[pltpu.VMEM(_STATE_BLOCK, F32)],
        compiler_params=pltpu.CompilerParams(
            dimension_semantics=("arbitrary", "arbitrary"), vmem_limit_bytes=VMEM_LIMIT),
        name="hgrn_prompt",
    )(proj, proj, norm_b, s0)


def _decode_kernel(*refs, bt, layer):
    (qkv_ref, z_ref, ab_ref, pb_ref, conv_ref, sd_ref, sh_ref,
     cw_ref, alog_ref, dtb_ref, na_ref, nb_ref, lbl_ref) = refs[:13]
    mix_ref, convo_ref, sdo_ref, sho_ref, oa_scr, ob_scr = refs[-6:]
    if layer > 0:
        prev_sd_ref, prev_sh_ref = refs[13:15]
        sdo_ref[0:layer] = prev_sd_ref[...]
        sho_ref[0:layer] = prev_sh_ref[...]
    cw = cw_ref[...]
    u = qkv_ref[...]
    b0 = conv_ref[:, 0:CONV_DIM]
    b1 = conv_ref[:, CONV_DIM:2 * CONV_DIM]
    b2 = conv_ref[:, 2 * CONV_DIM:3 * CONV_DIM]
    acc = b0 * cw[0:1, :]
    acc = acc + b1 * cw[1:2, :]
    acc = acc + b2 * cw[2:3, :]
    acc = acc + u * cw[3:4, :]
    conv = _silu(acc)
    convo_ref[:, 0:CONV_DIM] = b1
    convo_ref[:, CONV_DIM:2 * CONV_DIM] = b2
    convo_ref[:, 2 * CONV_DIM:3 * CONV_DIM] = u

    ab = ab_ref[...]
    eg_all = jnp.exp(-jnp.exp(alog_ref[...]) * _softplus(ab + dtb_ref[...]))
    beta_all = _sigmoid(ab)

    lb = _forget_lower_bound(lbl_ref[...], layer)
    f = lb + (1.0 - lb) * _sigmoid(pb_ref[:, MIX_HALF:2 * MIX_HALF])
    qb = _silu(pb_ref[:, 0:MIX_HALF])

    k_rows, q_rows, v_rows = [], [], []
    for h in range(N_HEADS):
        cs = slice(h * HEAD_DIM, (h + 1) * HEAD_DIM)
        qh = conv[:, cs]
        kh = conv[:, MIX_HALF + h * HEAD_DIM:MIX_HALF + (h + 1) * HEAD_DIM]
        q_rows.append(qh * lax.rsqrt(jnp.sum(qh * qh, axis=-1, keepdims=True) + EPS)
                      * (HEAD_DIM ** -0.5))
        k_rows.append(kh * lax.rsqrt(jnp.sum(kh * kh, axis=-1, keepdims=True) + EPS))
        v_rows.append(conv[:, 2 * MIX_HALF + h * HEAD_DIM:2 * MIX_HALF + (h + 1) * HEAD_DIM])
    f_rows = [f[:, h * HEAD_DIM:(h + 1) * HEAD_DIM] for h in range(N_HEADS)]
    qb_rows = [qb[:, h * HEAD_DIM:(h + 1) * HEAD_DIM] for h in range(N_HEADS)]
    qk_a = [jnp.sum(q_rows[h] * k_rows[h], axis=-1, keepdims=True) for h in range(N_HEADS)]
    qk_b = [jnp.sum(qb_rows[h] * (1.0 - f_rows[h]), axis=-1, keepdims=True) for h in range(N_HEADS)]

    f_cols = jnp.concatenate(f_rows + [jnp.zeros(((16 - N_HEADS) * bt, HEAD_DIM), F32)], axis=0).T

    row = lax.broadcasted_iota(jnp.int32, (8, HEAD_DIM), 0)

    def rows8(first, second=None):
        out = jnp.where(row == 0, first, 0.0)
        return out if second is None else jnp.where(row == 1, second, out)

    for b in range(bt):
        bs = slice(b, b + 1)
        for h in range(N_HEADS):
            cs = slice(h * HEAD_DIM, (h + 1) * HEAD_DIM)
            s = sd_ref[b, h]
            kq = rows8(k_rows[h][bs, :], q_rows[h][bs, :])
            eg = eg_all[bs, h:h + 1]
            beta = beta_all[bs, N_HEADS + h:N_HEADS + h + 1]
            r = _mm(kq, s)
            v_new = beta * (v_rows[h][bs, :] - eg * r[0:1, :])
            sdo_ref[layer, b, h] = s * eg + _mm_tn(kq, rows8(v_new))
            oa_scr[bs, cs] = eg * r[1:2, :] + qk_a[h][bs, :] * v_new
            s = sh_ref[b, h]
            fc = f_cols[:, h * bt + b:h * bt + b + 1]
            vb = pb_ref[bs, 2 * MIX_HALF + h * HEAD_DIM:2 * MIX_HALF + (h + 1) * HEAD_DIM]
            r = _mm(rows8(qb_rows[h][bs, :] * f_rows[h][bs, :]), s)
            sho_ref[layer, b, h] = s * fc + _mm_tn(rows8(1.0 - f_rows[h][bs, :]), rows8(vb))
            ob_scr[bs, cs] = r[0:1, :] + qk_b[h][bs, :] * vb

    for h in range(N_HEADS):
        cs = slice(h * HEAD_DIM, (h + 1) * HEAD_DIM)
        mix_ref[:, cs] = (_rms(oa_scr[:, cs], na_ref[...]) * _silu(z_ref[:, cs])).astype(mix_ref.dtype)
        gate = _sigmoid(pb_ref[:, 3 * MIX_HALF + h * HEAD_DIM:3 * MIX_HALF + (h + 1) * HEAD_DIM])
        mix_ref[:, MIX_HALF + h * HEAD_DIM:MIX_HALF + (h + 1) * HEAD_DIM] = (
            _rms(ob_scr[:, cs], nb_ref[...]) * gate).astype(mix_ref.dtype)


def _decode_call(proj, conv_state, s_delta, s_hgrn, prev_states, conv_w, a_log_rows, dt_bias_rows,
                 norm_a, norm_b, lb_logits, *, layer):
    batch = proj.shape[0]
    bt = 8
    kern = functools.partial(_decode_kernel, bt=bt, layer=layer)
    state_dims = (bt, N_HEADS, HEAD_DIM, HEAD_DIM)
    old_state_spec = pl.BlockSpec((None,) + state_dims, lambda i: (layer, i, 0, 0, 0))
    prev_state_spec = pl.BlockSpec((layer,) + state_dims, lambda i: (0, i, 0, 0, 0))
    new_state_spec = pl.BlockSpec((layer + 1,) + state_dims, lambda i: (0, i, 0, 0, 0))
    new_state_shape = jax.ShapeDtypeStruct((layer + 1, batch, N_HEADS, HEAD_DIM, HEAD_DIM), F32)
    return pl.pallas_call(
        kern,
        grid=(batch // bt,),
        in_specs=[
            pl.BlockSpec((bt, CONV_DIM), lambda i: (i, COL_QKV // CONV_DIM)),
            pl.BlockSpec((bt, MIX_HALF), lambda i: (i, COL_Z // MIX_HALF)),
            pl.BlockSpec((bt, 128), lambda i: (i, COL_AB // 128)),
            pl.BlockSpec((bt, 4 * MIX_HALF), lambda i: (i, COL_B // (4 * MIX_HALF))),
            pl.BlockSpec((None, bt, (CONV_W - 1) * CONV_DIM), lambda i: (layer, i, 0)),
            old_state_spec,
            old_state_spec,
            _layer_spec((CONV_W, CONV_DIM), layer),
            _layer_spec((1, 128), layer),
            _layer_spec((1, 128), layer),
            _layer_spec((1, HEAD_DIM), layer),
            _layer_spec((1, HEAD_DIM), layer),
            pl.BlockSpec((DEPTH, MIX_HALF), lambda i: (0, 0)),
        ] + [prev_state_spec] * len(prev_states),
        out_specs=[
            pl.BlockSpec((bt, 2 * MIX_HALF), lambda i: (i, 0)),
            pl.BlockSpec((bt, (CONV_W - 1) * CONV_DIM), lambda i: (i, 0)),
            new_state_spec,
            new_state_spec,
        ],
        out_shape=[
            jax.ShapeDtypeStruct((batch, 2 * MIX_HALF), BF16),
            jax.ShapeDtypeStruct((batch, (CONV_W - 1) * CONV_DIM), F32),
            new_state_shape,
            new_state_shape,
        ],
        scratch_shapes=[
            pltpu.VMEM((bt, MIX_HALF), F32),
            pltpu.VMEM((bt, MIX_HALF), F32),
        ],
        compiler_params=pltpu.CompilerParams(
            dimension_semantics=("arbitrary",), vmem_limit_bytes=VMEM_LIMIT),
        name="mixers_decode",
    )(proj, proj, proj, proj, conv_state, s_delta, s_hgrn, conv_w, a_log_rows, dt_bias_rows,
      norm_a, norm_b, lb_logits, *prev_states)


FF_CHUNK = 256


def _out_ffn_kernel(x_ref, ma_ref, mb_ref, g1_ref, sh2_ref, sc2_ref, g2_ref, wo_ref, n2_ref,
                    wg_ref, wu_ref, wd_ref, nf_ref, o_ref, acc_ref, *, final_norm, tiles_per_seq):
    gate1 = _mod_rows(g1_ref, tiles_per_seq)
    shift2 = _mod_rows(sh2_ref, tiles_per_seq)
    scale2 = _mod_rows(sc2_ref, tiles_per_seq)
    gate2 = _mod_rows(g2_ref, tiles_per_seq)
    mix = _mm(ma_ref[...], wo_ref[0:MIX_HALF, :]) + _mm(mb_ref[...], wo_ref[MIX_HALF:, :])
    x1 = x_ref[...] + gate1 * mix
    h = (_rms(x1, n2_ref[...]) * (1.0 + scale2) + shift2).astype(BF16)
    for c in range(D_FF // FF_CHUNK):
        fs = slice(c * FF_CHUNK, (c + 1) * FF_CHUNK)
        gate = jnp.dot(h, wg_ref[:, fs], preferred_element_type=F32)
        up = jnp.dot(h, wu_ref[:, fs], preferred_element_type=F32)
        part = _mm(_silu(gate) * up, wd_ref[fs, :])
        if c == 0:
            acc_ref[...] = part
        else:
            acc_ref[...] += part
    x2 = x1 + gate2 * acc_ref[...]
    if final_norm:
        x2 = _rms(x2, nf_ref[...])
    o_ref[...] = x2


def _out_ffn_call(x, mix_a, mix_b, mix_b_col, mods, w_out, norm2, w_gate, w_up, w_down, norm_f,
                  *, layer, n_prompt, per_token, seq_len, final_norm):
    rows = x.shape[0]
    tm = rows if per_token else 512
    tiles_per_seq = None if per_token else seq_len // tm
    resident = functools.partial(_layer_spec, layer=layer, pipeline_mode=pl.Buffered(1))
    mod = functools.partial(_mod_spec, mods, layer, n_prompt, per_token)
    kern = functools.partial(_out_ffn_kernel, final_norm=final_norm, tiles_per_seq=tiles_per_seq)
    return pl.pallas_call(
        kern,
        grid=(rows // tm,),
        in_specs=[
            pl.BlockSpec((tm, D_MODEL), lambda i: (i, 0)),
            pl.BlockSpec((tm, MIX_HALF), lambda i: (i, 0)),
            pl.BlockSpec((tm, MIX_HALF), lambda i: (i, mix_b_col)),
            mod(MOD_GATE1),
            mod(MOD_SHIFT2),
            mod(MOD_SCALE2),
            mod(MOD_GATE2),
            resident((D_MODEL, D_MODEL)),
            _layer_spec((1, D_MODEL), layer),
            resident((D_MODEL, D_FF)),
            resident((D_MODEL, D_FF)),
            resident((D_FF, D_MODEL)),
            pl.BlockSpec((1, D_MODEL), lambda i: (0, 0)),
        ],
        out_specs=pl.BlockSpec((tm, D_MODEL), lambda i: (i, 0)),
        out_shape=jax.ShapeDtypeStruct((rows, D_MODEL), F32),
        scratch_shapes=[pltpu.VMEM((tm, D_MODEL), F32)],
        compiler_params=pltpu.CompilerParams(
            dimension_semantics=("arbitrary",), vmem_limit_bytes=VMEM_LIMIT),
        name="out_ffn",
    )(x, mix_a, mix_b, mods, mods, mods, mods, w_out, norm2, w_gate, w_up, w_down, norm_f)


N_AB = 2 * N_HEADS
REORDER_COLS = 256
AB_BLOCK = COL_AB // REORDER_COLS


def _reorder_in_proj_kernel(wt_ref, o_ref):
    w = wt_ref[0]
    row = lax.broadcasted_iota(jnp.int32, w.shape, 0)
    w = jnp.where((pl.program_id(1) == AB_BLOCK) & (row >= N_AB), 0.0, w)
    o_ref[...] = w.T.astype(BF16)


def _reorder_in_proj(w):
    wt = jnp.swapaxes(w, 1, 2)

    def src_row(l, j):
        is_ab = j // AB_BLOCK
        is_b = j // (COL_B // REORDER_COLS) - 2 * is_ab
        start = j * REORDER_COLS + N_AB * is_b - (COL_AB - COL_B) * is_ab
        return (l, pl.multiple_of(start, 8), 0)

    return pl.pallas_call(
        _reorder_in_proj_kernel,
        grid=(DEPTH, pl.cdiv(IN_COLS, REORDER_COLS)),
        in_specs=[pl.BlockSpec((pl.Element(1), pl.Element(REORDER_COLS), pl.Element(D_MODEL)),
                               src_row)],
        out_specs=pl.BlockSpec((None, D_MODEL, REORDER_COLS), lambda l, j: (l, 0, j)),
        out_shape=jax.ShapeDtypeStruct((DEPTH, D_MODEL, IN_COLS), BF16),
        compiler_params=pltpu.CompilerParams(
            dimension_semantics=("arbitrary", "arbitrary"), vmem_limit_bytes=VMEM_LIMIT),
        name="reorder_in_proj",
    )(wt)


def _lane_rows(v):
    return jnp.pad(v.astype(F32), ((0, 0), (0, 128 - v.shape[1])))[:, None, :]


def kernel(x_prompt, x_sample, c_prompt, c_sample, state_delta, state_conv, state_hgrn, w_ada, b_ada,
           norm1, w_in, conv_w, a_log, dt_bias, norm_a, norm_b, lb_logits, w_out, norm2, w_gate, w_up,
           w_down, norm_f):
    bp, seq_len, _ = x_prompt.shape
    bs = x_sample.shape[0]

    mods = _ada_call(jnp.concatenate([c_sample, c_prompt], axis=0), w_ada, b_ada)

    xp = x_prompt.reshape(bp * seq_len, D_MODEL)
    xs = x_sample.reshape(bs, D_MODEL)
    zero_buf = jnp.zeros((bp, 8, CONV_DIM), F32)
    zero_state = jnp.zeros((bp, N_HEADS, HEAD_DIM, HEAD_DIM), F32)

    w_in_b = _reorder_in_proj(w_in)
    w_out_b = w_out.astype(BF16)
    w_gate_b = w_gate.astype(BF16)
    w_up_b = w_up.astype(BF16)
    w_down_b = w_down.astype(BF16)
    a_log_rows = _lane_rows(a_log)
    dt_bias_rows = _lane_rows(dt_bias)
    norm1_r = norm1[:, None, :]
    norm2_r = norm2[:, None, :]
    norm_a_r = norm_a[:, None, :]
    norm_b_r = norm_b[:, None, :]
    norm_f_r = norm_f[None, :]
    conv_state = state_conv.reshape(DEPTH, bs, (CONV_W - 1) * CONV_DIM)

    delta_p, conv_p, hgrn_p, conv_s = [], [], [], []
    decode_states = ()
    for l in range(DEPTH):
        last = l == DEPTH - 1
        dense = dict(layer=l, n_prompt=bp)

        proj, cv = _in_proj_call(xp, mods, norm1_r, w_in_b, zero_buf, conv_w, lb_logits,
                                 per_token=False, seq_len=seq_len, **dense)
        proj = proj.reshape(bp, seq_len, PROMPT_COLS)
        mix_a, s_a = _gdn_call(proj, zero_state, a_log_rows, dt_bias_rows, norm_a_r, layer=l)
        mix_b, s_b = _hgrn_call(proj, norm_b_r, zero_state, layer=l)
        mix_a = mix_a.reshape(bp * seq_len, MIX_HALF)
        mix_b = mix_b.reshape(bp * seq_len, MIX_HALF)
        xp = _out_ffn_call(xp, mix_a, mix_b, 0, mods, w_out_b, norm2_r, w_gate_b, w_up_b, w_down_b,
                           norm_f_r, per_token=False, seq_len=seq_len, final_norm=last, **dense)
        delta_p.append(s_a)
        conv_p.append(cv)
        hgrn_p.append(s_b)

        proj = _in_proj_call(xs, mods, norm1_r, w_in_b, None, None, None,
                             per_token=True, seq_len=1, **dense)
        mix, cv, *decode_states = _decode_call(
            proj, conv_state, state_delta, state_hgrn, decode_states,
            conv_w, a_log_rows, dt_bias_rows, norm_a_r, norm_b_r, lb_logits, layer=l)
        xs = _out_ffn_call(xs, mix, mix, 1, mods, w_out_b, norm2_r, w_gate_b, w_up_b, w_down_b,
                           norm_f_r, per_token=True, seq_len=1, final_norm=last, **dense)
        conv_s.append(cv.reshape(bs, CONV_W - 1, CONV_DIM))

    delta_s, hgrn_s = decode_states
    return (xp.reshape(bp, seq_len, D_MODEL), xs.reshape(bs, 1, D_MODEL),
            jnp.stack(delta_p), jnp.stack(conv_p), jnp.stack(hgrn_p),
            delta_s, jnp.stack(conv_s), hgrn_s)
```

```python
import functools

import jax
import jax.numpy as jnp
from jax import lax
from jax.experimental import pallas as pl
from jax.experimental.pallas import tpu as pltpu

F32 = jnp.float32
BF16 = jnp.bfloat16

D_MODEL = 1024
DEPTH = 2
N_HEADS = 4
HEAD_DIM = 128
MIX_HALF = N_HEADS * HEAD_DIM
CONV_W = 4
CONV_DIM = 3 * MIX_HALF
D_FF = 2816
EPS = 1e-6
GDN_CHUNK = 128
GDN_SUB = 16
HGRN_CHUNK = 64

COL_QKV = 0
COL_Z = CONV_DIM
COL_B = COL_Z + MIX_HALF
COL_AB = COL_B + 4 * MIX_HALF
IN_COLS = COL_AB + 128
PROMPT_COL_G = COL_AB
PROMPT_COL_AB = PROMPT_COL_G + MIX_HALF
PROMPT_COLS = PROMPT_COL_AB + 128

MOD_SHIFT1, MOD_SCALE1, MOD_GATE1, MOD_SHIFT2, MOD_SCALE2, MOD_GATE2 = range(6)

VMEM_LIMIT = 56 * 1024 * 1024


def _sigmoid(x):
    return 1.0 / (1.0 + jnp.exp(-x))


def _silu(x):
    return x * _sigmoid(x)


def _softplus(x):
    return jnp.maximum(x, 0.0) + jnp.log1p(jnp.exp(-jnp.abs(x)))


def _mm(a, b):
    return jnp.dot(a.astype(BF16), b.astype(BF16), preferred_element_type=F32)


def _mm_nt(a, b):
    return lax.dot_general(a.astype(BF16), b.astype(BF16), (((1,), (1,)), ((), ())),
                           preferred_element_type=F32)


def _mm_tn(a, b):
    return lax.dot_general(a.astype(BF16), b.astype(BF16), (((0,), (0,)), ((), ())),
                           preferred_element_type=F32)


def _rms(x, w):
    return x * lax.rsqrt(jnp.mean(x * x, axis=-1, keepdims=True) + EPS) * w


def _chunk_cumsum(x, chunk):
    rows = lax.broadcasted_iota(jnp.int32, x.shape, 0) & (chunk - 1)
    s = 1
    while s < chunk:
        x = x + jnp.where(rows >= s, pltpu.roll(x, s, 0), 0.0)
        s *= 2
    return x


def _ada_kernel(c_ref, w_ref, b_ref, o_ref):
    a = _silu(c_ref[...])
    o_ref[...] = _mm(a, w_ref[...]) + b_ref[...]


def _ada_call(c_all, w_ada, b_ada):
    rows = c_all.shape[0]
    tn = 1024
    return pl.pallas_call(
        _ada_kernel,
        grid=(DEPTH, 6 * D_MODEL // tn),
        in_specs=[
            pl.BlockSpec((rows, D_MODEL), lambda l, j: (0, 0)),
            pl.BlockSpec((None, D_MODEL, tn), lambda l, j: (l, 0, j)),
            pl.BlockSpec((None, 1, tn), lambda l, j: (l, 0, j)),
        ],
        out_specs=pl.BlockSpec((None, rows, tn), lambda l, j: (l, 0, j)),
        out_shape=jax.ShapeDtypeStruct((DEPTH, rows, 6 * D_MODEL), F32),
        compiler_params=pltpu.CompilerParams(
            dimension_semantics=("arbitrary", "arbitrary"), vmem_limit_bytes=VMEM_LIMIT),
        name="ada_mod",
    )(c_all, w_ada, b_ada.reshape(DEPTH, 1, 6 * D_MODEL))


def _mod_spec(mods, layer, n_prompt, per_token, col):
    n_decode = mods.shape[1] - n_prompt
    if per_token:
        return pl.BlockSpec((None, n_decode, D_MODEL), lambda i: (layer, 0, col))
    return pl.BlockSpec((None, n_prompt, D_MODEL), lambda i: (layer, n_decode // n_prompt, col))


def _mod_rows(ref, tiles_per_seq):
    if tiles_per_seq is None:
        return ref[...]
    return ref[pl.ds(pl.program_id(0) // tiles_per_seq, 1), :]


def _layer_spec(shape, layer, **kwargs):
    zeros = (0,) * len(shape)
    return pl.BlockSpec((None,) + tuple(shape), lambda *_: (layer,) + zeros, **kwargs)


def _modulated_input(x_ref, scale_ref, shift_ref, nw_ref, tile, tiles_per_seq):
    if tiles_per_seq is None:
        scale, shift = scale_ref[...], shift_ref[...]
    else:
        seq = tile // tiles_per_seq
        scale, shift = scale_ref[pl.ds(seq, 1), :], shift_ref[pl.ds(seq, 1), :]
    return (_rms(x_ref[...], nw_ref[...]) * (1.0 + scale) + shift).astype(BF16)


def _in_proj_decode_kernel(x_ref, scale_ref, shift_ref, nw_ref, w_ref, o_ref):
    h = _modulated_input(x_ref, scale_ref, shift_ref, nw_ref, 0, None)
    o_ref[...] = jnp.dot(h, w_ref[...], preferred_element_type=F32)


def _in_proj_prompt_kernel(x_ref, scale_ref, shift_ref, nw_ref, w_ref, buf_ref, cw_ref, lbl_ref,
                           o_ref, cfin_ref, raw, h_scr, *, tiles_per_seq, layer):
    i = pl.program_id(0)
    tm = x_ref.shape[0]
    tile_in_seq = i % tiles_per_seq
    body = slice(8, 8 + tm)

    n_slabs = CONV_DIM // HEAD_DIM

    def slab_cols(ct):
        return slice(ct * HEAD_DIM, (ct + 1) * HEAD_DIM)

    @pl.when(tile_in_seq == 0)
    def _():
        for ct in range(n_slabs):
            raw[ct, 0:8, :] = buf_ref[:, slab_cols(ct)]

    h_scr[...] = _modulated_input(x_ref, scale_ref, shift_ref, nw_ref, i, tiles_per_seq)
    lb_all = _forget_lower_bound(lbl_ref[...], layer)

    def project(cs):
        return jnp.dot(h_scr[...], w_ref[:, cs], preferred_element_type=F32)

    def conv_slabs(cs):
        pre = project(cs)
        out = []
        for ct in range(cs.start // HEAD_DIM, cs.stop // HEAD_DIM):
            cols = slab_cols(ct)
            raw[ct, body, :] = pre[:, cols.start - cs.start:cols.stop - cs.start]
            acc = raw[ct, 5:5 + tm, :] * cw_ref[0:1, cols]
            for j in range(1, CONV_W):
                acc = acc + raw[ct, 5 + j:5 + j + tm, :] * cw_ref[j:j + 1, cols]
            out.append((cols, _silu(acc)))
        return out

    def l2_normalised(cs, scale):
        for cols, v in conv_slabs(cs):
            o_ref[:, cols] = v * lax.rsqrt(jnp.sum(v * v, axis=-1, keepdims=True) + EPS) * scale

    def post_q(cs):
        l2_normalised(cs, HEAD_DIM ** -0.5)

    def post_k(cs):
        l2_normalised(cs, 1.0)

    def post_v(cs):
        for cols, v in conv_slabs(cs):
            o_ref[:, cols] = v

    def post_copy(cs):
        o_ref[:, cs] = project(cs)

    def post_silu(cs):
        o_ref[:, cs] = _silu(project(cs))

    def post_forget(cs):
        gate_cols = slice(cs.start - COL_B - MIX_HALF, cs.stop - COL_B - MIX_HALF)
        lb = lb_all[:, gate_cols]
        f = lb + (1.0 - lb) * _sigmoid(project(cs))
        o_ref[:, cs] = 1.0 - f
        o_ref[:, PROMPT_COL_G + gate_cols.start:PROMPT_COL_G + gate_cols.stop] = (
            _chunk_cumsum(jnp.log(f), HGRN_CHUNK))

    width = 2 * HEAD_DIM
    pieces = ([post_q] * 2 + [post_k] * 2 + [post_v] * 2 + [post_copy] * 2
              + [post_silu] * 2 + [post_forget] * 2 + [post_copy] * 4)
    order = [0, 6, 1, 7, 2, 12, 3, 13, 4, 14, 5, 15, 8, 9, 10, 11]
    for piece in order:
        pieces[piece](slice(piece * width, (piece + 1) * width))
    o_ref[:, PROMPT_COL_AB:PROMPT_COLS] = project(slice(COL_AB, IN_COLS))

    for ct in range(n_slabs):
        raw[ct, 0:8, :] = raw[ct, tm:tm + 8, :]

    @pl.when(tile_in_seq == tiles_per_seq - 1)
    def _():
        for ct in range(n_slabs):
            cfin_ref[:, slab_cols(ct)] = raw[ct, 8 - (CONV_W - 1):8, :]


def _in_proj_call(x, mods, norm1, w_in, conv_buf, conv_w, lb_logits, *, layer, n_prompt, per_token,
                  seq_len):
    rows = x.shape[0]
    in_specs = [
        None,
        _mod_spec(mods, layer, n_prompt, per_token, MOD_SCALE1),
        _mod_spec(mods, layer, n_prompt, per_token, MOD_SHIFT1),
        _layer_spec((1, D_MODEL), layer),
        _layer_spec((D_MODEL, IN_COLS), layer, pipeline_mode=pl.Buffered(1)),
    ]
    params = pltpu.CompilerParams(dimension_semantics=("arbitrary",), vmem_limit_bytes=VMEM_LIMIT)
    if per_token:
        in_specs[0] = pl.BlockSpec((rows, D_MODEL), lambda i: (0, 0))
        return pl.pallas_call(
            _in_proj_decode_kernel,
            grid=(1,),
            in_specs=in_specs,
            out_specs=pl.BlockSpec((rows, IN_COLS), lambda i: (0, 0)),
            out_shape=jax.ShapeDtypeStruct((rows, IN_COLS), F32),
            compiler_params=params,
            name="in_proj_decode",
        )(x, mods, mods, norm1, w_in)

    tm = 512
    tiles_per_seq = seq_len // tm
    in_specs[0] = pl.BlockSpec((tm, D_MODEL), lambda i: (i, 0))
    in_specs += [
        pl.BlockSpec((None, 8, CONV_DIM), lambda i: (i // tiles_per_seq, 0, 0)),
        _layer_spec((CONV_W, CONV_DIM), layer),
        pl.BlockSpec((DEPTH, MIX_HALF), lambda i: (0, 0)),
    ]
    kern = functools.partial(_in_proj_prompt_kernel, tiles_per_seq=tiles_per_seq, layer=layer)
    return pl.pallas_call(
        kern,
        grid=(rows // tm,),
        in_specs=in_specs,
        out_specs=[
            pl.BlockSpec((tm, PROMPT_COLS), lambda i: (i, 0)),
            pl.BlockSpec((None, CONV_W - 1, CONV_DIM), lambda i: (i // tiles_per_seq, 0, 0)),
        ],
        out_shape=[
            jax.ShapeDtypeStruct((rows, PROMPT_COLS), F32),
            jax.ShapeDtypeStruct((n_prompt, CONV_W - 1, CONV_DIM), F32),
        ],
        scratch_shapes=[
            pltpu.VMEM((CONV_DIM // HEAD_DIM, tm + 8, HEAD_DIM), F32),
            pltpu.VMEM((tm, D_MODEL), BF16),
        ],
        compiler_params=params,
        name="in_proj_prompt",
    )(x, mods, mods, norm1, w_in, conv_buf, conv_w, lb_logits)


def _unit_lower_inverse(mats, sub_mask, eye):
    d = [jnp.where(sub_mask, a, 0.0) for a in mats]
    n = [a - x for a, x in zip(mats, d)]
    d2 = [_mm(x, x) for x in d]
    d4 = [_mm(x, x) for x in d2]
    d8 = [_mm(x, x) for x in d4]
    p = [eye - x + x2 - _mm(x, x2) for x, x2 in zip(d, d2)]
    p = [x + _mm(x, y) for x, y in zip(p, d4)]
    dinv = [x + _mm(x, y) for x, y in zip(p, d8)]
    e = [_mm(x, y) for x, y in zip(dinv, n)]
    e2 = [_mm(x, x) for x in e]
    e4 = [_mm(x, x) for x in e2]
    f = [x2 - x - _mm(x, x2) for x, x2 in zip(e, e2)]
    q = [x + y + _mm(x, y) for x, y in zip(f, e4)]
    return [x + _mm(y, x) for x, y in zip(dinv, q)]


def _gdn_kernel(qkv_ref, z_ref, ab_ref, s0_ref, alog_ref, dtb_ref, na_ref,
                mix_ref, sfin_ref, s_scr, *, tile):
    t = pl.program_id(1)

    @pl.when(t == 0)
    def _():
        s_scr[...] = s0_ref[...]

    ri = lax.broadcasted_iota(jnp.int32, (GDN_CHUNK, GDN_CHUNK), 0)
    ci = lax.broadcasted_iota(jnp.int32, (GDN_CHUNK, GDN_CHUNK), 1)
    m_incl = ri >= ci
    m_strict = ri > ci
    sub_mask = jnp.bitwise_xor(ri, ci) < GDN_SUB
    eye = jnp.where(ri == ci, 1.0, 0.0).astype(F32)
    n_chunks = tile // GDN_CHUNK
    pairs = [(b, h) for b in range(qkv_ref.shape[0]) for h in range(N_HEADS)]
    chains = [(p, c) for p in pairs for c in range(n_chunks)]

    def rows(c):
        return slice(c * GDN_CHUNK, (c + 1) * GDN_CHUNK)

    g_cum_t = {}
    qn, kn, kb, gc, eg, rhs = {}, {}, {}, {}, {}, {}
    for b in range(qkv_ref.shape[0]):
        ab = ab_ref[b]
        g_all = -jnp.exp(alog_ref[...]) * _softplus(ab + dtb_ref[...])
        sig_all = _sigmoid(ab)
        g_cum = _chunk_cumsum(g_all, GDN_CHUNK)
        g_cum_t[b] = g_cum.T
        for h in range(N_HEADS):
            p = (b, h)
            qn[p] = qkv_ref[b, :, h * HEAD_DIM:(h + 1) * HEAD_DIM]
            kn[p] = qkv_ref[b, :, MIX_HALF + h * HEAD_DIM:MIX_HALF + (h + 1) * HEAD_DIM]
            vh = qkv_ref[b, :, 2 * MIX_HALF + h * HEAD_DIM:2 * MIX_HALF + (h + 1) * HEAD_DIM]
            beta = sig_all[:, N_HEADS + h:N_HEADS + h + 1]
            gc[p] = g_cum[:, h:h + 1]
            eg[p] = jnp.exp(gc[p])
            kb[p] = kn[p] * beta
            rhs[p] = jnp.concatenate([vh * beta, kb[p] * eg[p]], axis=1)

    dec = {}
    for p, c in chains:
        gr = g_cum_t[p[0]][p[1]:p[1] + 1, rows(c)]
        dec[p, c] = jnp.exp(jnp.where(m_incl, gc[p][rows(c)] - gr, -jnp.inf))
    st = {(p, c): _mm_nt(jnp.concatenate([kb[p][rows(c)], qn[p][rows(c)]], axis=0), kn[p][rows(c)])
          for p, c in chains}
    a = [st[pc][:GDN_CHUNK] * jnp.where(m_strict, dec[pc], 0.0) for pc in chains]
    qk = {pc: st[pc][GDN_CHUNK:] * dec[pc] for pc in chains}
    tinv = dict(zip(chains, _unit_lower_inverse(a, sub_mask, eye)))
    uw = {(p, c): _mm(tinv[p, c], rhs[p][rows(c)]) for p, c in chains}

    s = {p: s_scr[p] for p in pairs}
    for c in range(n_chunks):
        rs = rows(c)
        r = {p: _mm(jnp.concatenate([uw[p, c][:, HEAD_DIM:], qn[p][rs] * eg[p][rs]], axis=0), s[p])
             for p in pairs}
        v_new = {p: uw[p, c][:, :HEAD_DIM] - r[p][:GDN_CHUNK] for p in pairs}
        for p in pairs:
            g_last = gc[p][(c + 1) * GDN_CHUNK - 1:(c + 1) * GDN_CHUNK, :]
            kd = kn[p][rs] * jnp.exp(g_last - gc[p][rs])
            s[p] = s[p] * jnp.exp(g_last) + _mm_tn(kd, v_new[p])
        for b, h in pairs:
            cs = slice(h * HEAD_DIM, (h + 1) * HEAD_DIM)
            o = r[b, h][GDN_CHUNK:] + _mm(qk[(b, h), c], v_new[b, h])
            mix_ref[b, rs, cs] = (_rms(o, na_ref[...]) * _silu(z_ref[b, rs, cs])).astype(mix_ref.dtype)
    for p in pairs:
        s_scr[p] = s[p]

    @pl.when(t == pl.num_programs(1) - 1)
    def _():
        sfin_ref[...] = s_scr[...]


MIXER_SEQS = 2
MIXER_TILE = 256


def _mixer_specs(cols, col_block):
    return pl.BlockSpec((MIXER_SEQS, MIXER_TILE, cols), lambda b, t: (b, t, col_block))


_STATE_BLOCK = (MIXER_SEQS, N_HEADS, HEAD_DIM, HEAD_DIM)


def _gdn_call(proj, s0, a_log_rows, dt_bias_rows, norm_a, *, layer):
    batch, seq_len, _ = proj.shape
    state_spec = pl.BlockSpec(_STATE_BLOCK, lambda b, t: (b, 0, 0, 0))
    return pl.pallas_call(
        functools.partial(_gdn_kernel, tile=MIXER_TILE),
        grid=(batch // MIXER_SEQS, seq_len // MIXER_TILE),
        in_specs=[
            _mixer_specs(CONV_DIM, COL_QKV // CONV_DIM),
            _mixer_specs(MIX_HALF, COL_Z // MIX_HALF),
            _mixer_specs(128, PROMPT_COL_AB // 128),
            state_spec,
            _layer_spec((1, 128), layer),
            _layer_spec((1, 128), layer),
            _layer_spec((1, HEAD_DIM), layer),
        ],
        out_specs=[_mixer_specs(MIX_HALF, 0), state_spec],
        out_shape=[
            jax.ShapeDtypeStruct((batch, seq_len, MIX_HALF), BF16),
            jax.ShapeDtypeStruct((batch, N_HEADS, HEAD_DIM, HEAD_DIM), F32),
        ],
        scratch_shapes=[pltpu.VMEM(_STATE_BLOCK, F32)],
        compiler_params=pltpu.CompilerParams(
            dimension_semantics=("arbitrary", "arbitrary"), vmem_limit_bytes=VMEM_LIMIT),
        name="gdn_prompt",
    )(proj, proj, proj, s0, a_log_rows, dt_bias_rows, norm_a)


def _forget_lower_bound(lbl, layer):
    m = jnp.max(lbl, axis=0, keepdims=True)
    e = jnp.exp(lbl - m)
    sm = e / jnp.sum(e, axis=0, keepdims=True)
    cs = sm[0:1, :]
    for i in range(1, layer + 1):
        cs = cs + sm[i:i + 1, :]
    return cs - sm[0:1, :]


def _midpoint_rows(g, s, row_id):
    n, width = g.shape
    if s >= 8:
        parts = [jnp.broadcast_to(g[b + s:b + s + 1, :], (2 * s, width))
                 for b in range(0, n, 2 * s)]
        return jnp.concatenate(parts, axis=0)
    tiles = g.reshape(n // 8, 8, width)
    sub = row_id.reshape(n // 8, 8, width) & 7
    mids = range(s, 8, 2 * s)
    out = jnp.broadcast_to(tiles[:, mids[-1]:mids[-1] + 1, :], tiles.shape)
    for m in reversed(mids[:-1]):
        out = jnp.where(sub < m + s, jnp.broadcast_to(tiles[:, m:m + 1, :], tiles.shape), out)
    return out.reshape(n, width)


def _hgrn_kernel(p_ref, g_ref, nb_ref, s0_ref, mix_ref, sfin_ref, st_scr, *, tile):
    t = pl.program_id(1)
    pairs = [(b, h) for b in range(p_ref.shape[0]) for h in range(N_HEADS)]

    @pl.when(t == 0)
    def _():
        for p in pairs:
            st_scr[p] = s0_ref[p].T

    n_chunks = tile // HGRN_CHUNK
    chains = [(p, c) for p in pairs for c in range(n_chunks)]

    def rows(c):
        return slice(c * HGRN_CHUNK, (c + 1) * HGRN_CHUNK)

    def cols(block, h):
        return slice(block * MIX_HALF + h * HEAD_DIM, block * MIX_HALF + (h + 1) * HEAD_DIM)

    row_id = lax.broadcasted_iota(jnp.int32, (tile, HEAD_DIM), 0)
    ri = lax.broadcasted_iota(jnp.int32, (HGRN_CHUNK, HGRN_CHUNK), 0)
    ci = lax.broadcasted_iota(jnp.int32, (HGRN_CHUNK, HGRN_CHUNK), 1)
    dist = jnp.bitwise_xor(ri, ci)
    lower = ri > ci

    q = {(b, h): p_ref[b, :, cols(0, h)] for b, h in pairs}
    k = {(b, h): p_ref[b, :, cols(1, h)] for b, h in pairs}
    g = {(b, h): g_ref[b, :, cols(0, h)] for b, h in pairs}

    a = {(p, c): jnp.where(ri == ci, _mm_nt(q[p][rows(c)], k[p][rows(c)]), 0.0) for p, c in chains}
    s = HGRN_CHUNK // 2
    while s >= 1:
        upper = (row_id & s) != 0
        level = lower & (dist >= s) & (dist < 2 * s)
        x = {}
        for p in pairs:
            d = g[p] - _midpoint_rows(g[p], s, row_id)
            x[p] = jnp.where(upper, q[p], k[p]) * jnp.exp(-jnp.abs(d))
        for p, c in chains:
            xc = x[p][rows(c)]
            a[p, c] = jnp.where(level, _mm_nt(xc, xc), a[p, c])
        s //= 2

    st = {p: st_scr[p] for p in pairs}
    for c in range(n_chunks):
        rs = rows(c)
        for b, h in pairs:
            p = (b, h)
            gc = g[p][rs]
            vc = p_ref[b, rs, cols(2, h)]
            o = _mm(a[p, c], vc) + _mm_nt(q[p][rs] * jnp.exp(gc), st[p])
            g_last = gc[HGRN_CHUNK - 1:HGRN_CHUNK, :]
            st[p] = st[p] * jnp.exp(g_last) + _mm_tn(vc, k[p][rs] * jnp.exp(g_last - gc))
            gate = _sigmoid(p_ref[b, rs, cols(3, h)])
            mix_ref[b, rs, cols(0, h)] = (_rms(o, nb_ref[...]) * gate).astype(mix_ref.dtype)
    for p in pairs:
        st_scr[p] = st[p]

    @pl.when(t == pl.num_programs(1) - 1)
    def _():
        for p in pairs:
            sfin_ref[p] = st_scr[p].T


def _hgrn_call(proj, norm_b, s0, *, layer):
    batch, seq_len, _ = proj.shape
    state_spec = pl.BlockSpec(_STATE_BLOCK, lambda b, t: (b, 0, 0, 0))
    return pl.pallas_call(
        functools.partial(_hgrn_kernel, tile=MIXER_TILE),
        grid=(batch // MIXER_SEQS, seq_len // MIXER_TILE),
        in_specs=[
            _mixer_specs(4 * MIX_HALF, COL_B // (4 * MIX_HALF)),
            _mixer_specs(MIX_HALF, PROMPT_COL_G // MIX_HALF),
            _layer_spec((1, HEAD_DIM), layer),
            state_spec,
        ],
        out_specs=[_mixer_specs(MIX_HALF, 0), state_spec],
        out_shape=[
            jax.ShapeDtypeStruct((batch, seq_len, MIX_HALF), BF16),
            jax.ShapeDtypeStruct((batch, N_HEADS, HEAD_DIM, HEAD_DIM), F32),
        ],
        scratch_shapes=[pltpu.VMEM(_STATE_BLOCK, F32)],
        compiler_params=pltpu.CompilerParams(
            dimension_semantics=("arbitrary", "arbitrary"), vmem_limit_bytes=VMEM_LIMIT),
        name="hgrn_prompt",
    )(proj, proj, norm_b, s0)


def _decode_kernel(*refs, bt, layer):
    (qkv_ref, z_ref, ab_ref, pb_ref, conv_ref, sd_ref, sh_ref,
     cw_ref, alog_ref, dtb_ref, na_ref, nb_ref, lbl_ref) = refs[:13]
    mix_ref, convo_ref, sdo_ref, sho_ref, oa_scr, ob_scr = refs[-6:]
    if layer > 0:
        prev_sd_ref, prev_sh_ref = refs[13:15]
        sdo_ref[0:layer] = prev_sd_ref[...]
        sho_ref[0:layer] = prev_sh_ref[...]
    cw = cw_ref[...]
    u = qkv_ref[...]
    b0 = conv_ref[:, 0:CONV_DIM]
    b1 = conv_ref[:, CONV_DIM:2 * CONV_DIM]
    b2 = conv_ref[:, 2 * CONV_DIM:3 * CONV_DIM]
    acc = b0 * cw[0:1, :]
    acc = acc + b1 * cw[1:2, :]
    acc = acc + b2 * cw[2:3, :]
    acc = acc + u * cw[3:4, :]
    conv = _silu(acc)
    convo_ref[:, 0:CONV_DIM] = b1
    convo_ref[:, CONV_DIM:2 * CONV_DIM] = b2
    convo_ref[:, 2 * CONV_DIM:3 * CONV_DIM] = u

    ab = ab_ref[...]
    eg_all = jnp.exp(-jnp.exp(alog_ref[...]) * _softplus(ab + dtb_ref[...]))
    beta_all = _sigmoid(ab)

    lb = _forget_lower_bound(lbl_ref[...], layer)
    f = lb + (1.0 - lb) * _sigmoid(pb_ref[:, MIX_HALF:2 * MIX_HALF])
    qb = _silu(pb_ref[:, 0:MIX_HALF])

    k_rows, q_rows, v_rows = [], [], []
    for h in range(N_HEADS):
        cs = slice(h * HEAD_DIM, (h + 1) * HEAD_DIM)
        qh = conv[:, cs]
        kh = conv[:, MIX_HALF + h * HEAD_DIM:MIX_HALF + (h + 1) * HEAD_DIM]
        q_rows.append(qh * lax.rsqrt(jnp.sum(qh * qh, axis=-1, keepdims=True) + EPS)
                      * (HEAD_DIM ** -0.5))
        k_rows.append(kh * lax.rsqrt(jnp.sum(kh * kh, axis=-1, keepdims=True) + EPS))
        v_rows.append(conv[:, 2 * MIX_HALF + h * HEAD_DIM:2 * MIX_HALF + (h + 1) * HEAD_DIM])
    f_rows = [f[:, h * HEAD_DIM:(h + 1) * HEAD_DIM] for h in range(N_HEADS)]
    qb_rows = [qb[:, h * HEAD_DIM:(h + 1) * HEAD_DIM] for h in range(N_HEADS)]
    qk_a = [jnp.sum(q_rows[h] * k_rows[h], axis=-1, keepdims=True) for h in range(N_HEADS)]
    qk_b = [jnp.sum(qb_rows[h] * (1.0 - f_rows[h]), axis=-1, keepdims=True) for h in range(N_HEADS)]

    f_cols = jnp.concatenate(f_rows + [jnp.zeros(((16 - N_HEADS) * bt, HEAD_DIM), F32)], axis=0).T

    row = lax.broadcasted_iota(jnp.int32, (8, HEAD_DIM), 0)

    def rows8(first, second=None):
        out = jnp.where(row == 0, first, 0.0)
        return out if second is None else jnp.where(row == 1, second, out)

    for b in range(bt):
        bs = slice(b, b + 1)
        for h in range(N_HEADS):
            cs = slice(h * HEAD_DIM, (h + 1) * HEAD_DIM)
            s = sd_ref[b, h]
            kq = rows8(k_rows[h][bs, :], q_rows[h][bs, :])
            eg = eg_all[bs, h:h + 1]
            beta = beta_all[bs, N_HEADS + h:N_HEADS + h + 1]
            r = _mm(kq, s)
            v_new = beta * (v_rows[h][bs, :] - eg * r[0:1, :])
            sdo_ref[layer, b, h] = s * eg + _mm_tn(kq, rows8(v_new))
            oa_scr[bs, cs] = eg * r[1:2, :] + qk_a[h][bs, :] * v_new
            s = sh_ref[b, h]
            fc = f_cols[:, h * bt + b:h * bt + b + 1]
            vb = pb_ref[bs, 2 * MIX_HALF + h * HEAD_DIM:2 * MIX_HALF + (h + 1) * HEAD_DIM]
            r = _mm(rows8(qb_rows[h][bs, :] * f_rows[h][bs, :]), s)
            sho_ref[layer, b, h] = s * fc + _mm_tn(rows8(1.0 - f_rows[h][bs, :]), rows8(vb))
            ob_scr[bs, cs] = r[0:1, :] + qk_b[h][bs, :] * vb

    for h in range(N_HEADS):
        cs = slice(h * HEAD_DIM, (h + 1) * HEAD_DIM)
        mix_ref[:, cs] = (_rms(oa_scr[:, cs], na_ref[...]) * _silu(z_ref[:, cs])).astype(mix_ref.dtype)
        gate = _sigmoid(pb_ref[:, 3 * MIX_HALF + h * HEAD_DIM:3 * MIX_HALF + (h + 1) * HEAD_DIM])
        mix_ref[:, MIX_HALF + h * HEAD_DIM:MIX_HALF + (h + 1) * HEAD_DIM] = (
            _rms(ob_scr[:, cs], nb_ref[...]) * gate).astype(mix_ref.dtype)


def _decode_call(proj, conv_state, s_delta, s_hgrn, prev_states, conv_w, a_log_rows, dt_bias_rows,
                 norm_a, norm_b, lb_logits, *, layer):
    batch = proj.shape[0]
    bt = 8
    kern = functools.partial(_decode_kernel, bt=bt, layer=layer)
    state_dims = (bt, N_HEADS, HEAD_DIM, HEAD_DIM)
    old_state_spec = pl.BlockSpec((None,) + state_dims, lambda i: (layer, i, 0, 0, 0))
    prev_state_spec = pl.BlockSpec((layer,) + state_dims, lambda i: (0, i, 0, 0, 0))
    new_state_spec = pl.BlockSpec((layer + 1,) + state_dims, lambda i: (0, i, 0, 0, 0))
    new_state_shape = jax.ShapeDtypeStruct((layer + 1, batch, N_HEADS, HEAD_DIM, HEAD_DIM), F32)
    return pl.pallas_call(
        kern,
        grid=(batch // bt,),
        in_specs=[
            pl.BlockSpec((bt, CONV_DIM), lambda i: (i, COL_QKV // CONV_DIM)),
            pl.BlockSpec((bt, MIX_HALF), lambda i: (i, COL_Z // MIX_HALF)),
            pl.BlockSpec((bt, 128), lambda i: (i, COL_AB // 128)),
            pl.BlockSpec((bt, 4 * MIX_HALF), lambda i: (i, COL_B // (4 * MIX_HALF))),
            pl.BlockSpec((None, bt, (CONV_W - 1) * CONV_DIM), lambda i: (layer, i, 0)),
            old_state_spec,
            old_state_spec,
            _layer_spec((CONV_W, CONV_DIM), layer),
            _layer_spec((1, 128), layer),
            _layer_spec((1, 128), layer),
            _layer_spec((1, HEAD_DIM), layer),
            _layer_spec((1, HEAD_DIM), layer),
            pl.BlockSpec((DEPTH, MIX_HALF), lambda i: (0, 0)),
        ] + [prev_state_spec] * len(prev_states),
        out_specs=[
            pl.BlockSpec((bt, 2 * MIX_HALF), lambda i: (i, 0)),
            pl.BlockSpec((bt, (CONV_W - 1) * CONV_DIM), lambda i: (i, 0)),
            new_state_spec,
            new_state_spec,
        ],
        out_shape=[
            jax.ShapeDtypeStruct((batch, 2 * MIX_HALF), BF16),
            jax.ShapeDtypeStruct((batch, (CONV_W - 1) * CONV_DIM), F32),
            new_state_shape,
            new_state_shape,
        ],
        scratch_shapes=[
            pltpu.VMEM((bt, MIX_HALF), F32),
            pltpu.VMEM((bt, MIX_HALF), F32),
        ],
        compiler_params=pltpu.CompilerParams(
            dimension_semantics=("arbitrary",), vmem_limit_bytes=VMEM_LIMIT),
        name="mixers_decode",
    )(proj, proj, proj, proj, conv_state, s_delta, s_hgrn, conv_w, a_log_rows, dt_bias_rows,
      norm_a, norm_b, lb_logits, *prev_states)


FF_CHUNK = 256


def _out_ffn_kernel(x_ref, ma_ref, mb_ref, g1_ref, sh2_ref, sc2_ref, g2_ref, wo_ref, n2_ref,
                    wg_ref, wu_ref, wd_ref, nf_ref, o_ref, acc_ref, *, final_norm, tiles_per_seq):
    gate1 = _mod_rows(g1_ref, tiles_per_seq)
    shift2 = _mod_rows(sh2_ref, tiles_per_seq)
    scale2 = _mod_rows(sc2_ref, tiles_per_seq)
    gate2 = _mod_rows(g2_ref, tiles_per_seq)
    mix = _mm(ma_ref[...], wo_ref[0:MIX_HALF, :]) + _mm(mb_ref[...], wo_ref[MIX_HALF:, :])
    x1 = x_ref[...] + gate1 * mix
    h = (_rms(x1, n2_ref[...]) * (1.0 + scale2) + shift2).astype(BF16)
    for c in range(D_FF // FF_CHUNK):
        fs = slice(c * FF_CHUNK, (c + 1) * FF_CHUNK)
        gate = jnp.dot(h, wg_ref[:, fs], preferred_element_type=F32)
        up = jnp.dot(h, wu_ref[:, fs], preferred_element_type=F32)
        part = _mm(_silu(gate) * up, wd_ref[fs, :])
        if c == 0:
            acc_ref[...] = part
        else:
            acc_ref[...] += part
    x2 = x1 + gate2 * acc_ref[...]
    if final_norm:
        x2 = _rms(x2, nf_ref[...])
    o_ref[...] = x2


def _out_ffn_call(x, mix_a, mix_b, mix_b_col, mods, w_out, norm2, w_gate, w_up, w_down, norm_f,
                  *, layer, n_prompt, per_token, seq_len, final_norm):
    rows = x.shape[0]
    tm = rows if per_token else 512
    tiles_per_seq = None if per_token else seq_len // tm
    resident = functools.partial(_layer_spec, layer=layer, pipeline_mode=pl.Buffered(1))
    mod = functools.partial(_mod_spec, mods, layer, n_prompt, per_token)
    kern = functools.partial(_out_ffn_kernel, final_norm=final_norm, tiles_per_seq=tiles_per_seq)
    return pl.pallas_call(
        kern,
        grid=(rows // tm,),
        in_specs=[
            pl.BlockSpec((tm, D_MODEL), lambda i: (i, 0)),
            pl.BlockSpec((tm, MIX_HALF), lambda i: (i, 0)),
            pl.BlockSpec((tm, MIX_HALF), lambda i: (i, mix_b_col)),
            mod(MOD_GATE1),
            mod(MOD_SHIFT2),
            mod(MOD_SCALE2),
            mod(MOD_GATE2),
            resident((D_MODEL, D_MODEL)),
            _layer_spec((1, D_MODEL), layer),
            resident((D_MODEL, D_FF)),
            resident((D_MODEL, D_FF)),
            resident((D_FF, D_MODEL)),
            pl.BlockSpec((1, D_MODEL), lambda i: (0, 0)),
        ],
        out_specs=pl.BlockSpec((tm, D_MODEL), lambda i: (i, 0)),
        out_shape=jax.ShapeDtypeStruct((rows, D_MODEL), F32),
        scratch_shapes=[pltpu.VMEM((tm, D_MODEL), F32)],
        compiler_params=pltpu.CompilerParams(
            dimension_semantics=("arbitrary",), vmem_limit_bytes=VMEM_LIMIT),
        name="out_ffn",
    )(x, mix_a, mix_b, mods, mods, mods, mods, w_out, norm2, w_gate, w_up, w_down, norm_f)


N_AB = 2 * N_HEADS
REORDER_COLS = 512
AB_BLOCK = COL_AB // REORDER_COLS


def _reorder_in_proj_kernel(wt_ref, o_ref):
    w = wt_ref[0]
    row = lax.broadcasted_iota(jnp.int32, w.shape, 0)
    w = jnp.where((pl.program_id(1) == AB_BLOCK) & (row >= N_AB), 0.0, w)
    o_ref[...] = w.T.astype(BF16)


def _reorder_in_proj(w):
    wt = jnp.swapaxes(w, 1, 2)

    def src_row(l, j):
        is_ab = j // AB_BLOCK
        is_b = j // (COL_B // REORDER_COLS) - 2 * is_ab
        start = j * REORDER_COLS + N_AB * is_b - (COL_AB - COL_B) * is_ab
        return (l, pl.multiple_of(start, 8), 0)

    return pl.pallas_call(
        _reorder_in_proj_kernel,
        grid=(DEPTH, pl.cdiv(IN_COLS, REORDER_COLS)),
        in_specs=[pl.BlockSpec((pl.Element(1), pl.Element(REORDER_COLS), pl.Element(D_MODEL)),
                               src_row)],
        out_specs=pl.BlockSpec((None, D_MODEL, REORDER_COLS), lambda l, j: (l, 0, j)),
        out_shape=jax.ShapeDtypeStruct((DEPTH, D_MODEL, IN_COLS), BF16),
        compiler_params=pltpu.CompilerParams(
            dimension_semantics=("arbitrary", "arbitrary"), vmem_limit_bytes=VMEM_LIMIT),
        name="reorder_in_proj",
    )(wt)


def _lane_rows(v):
    return jnp.pad(v.astype(F32), ((0, 0), (0, 128 - v.shape[1])))[:, None, :]


def kernel(x_prompt, x_sample, c_prompt, c_sample, state_delta, state_conv, state_hgrn, w_ada, b_ada,
           norm1, w_in, conv_w, a_log, dt_bias, norm_a, norm_b, lb_logits, w_out, norm2, w_gate, w_up,
           w_down, norm_f):
    bp, seq_len, _ = x_prompt.shape
    bs = x_sample.shape[0]

    mods = _ada_call(jnp.concatenate([c_sample, c_prompt], axis=0), w_ada, b_ada)

    xp = x_prompt.reshape(bp * seq_len, D_MODEL)
    xs = x_sample.reshape(bs, D_MODEL)
    zero_buf = jnp.zeros((bp, 8, CONV_DIM), F32)
    zero_state = jnp.zeros((bp, N_HEADS, HEAD_DIM, HEAD_DIM), F32)

    w_in_b = _reorder_in_proj(w_in)
    w_out_b = w_out.astype(BF16)
    w_gate_b = w_gate.astype(BF16)
    w_up_b = w_up.astype(BF16)
    w_down_b = w_down.astype(BF16)
    a_log_rows = _lane_rows(a_log)
    dt_bias_rows = _lane_rows(dt_bias)
    norm1_r = norm1[:, None, :]
    norm2_r = norm2[:, None, :]
    norm_a_r = norm_a[:, None, :]
    norm_b_r = norm_b[:, None, :]
    norm_f_r = norm_f[None, :]
    conv_state = state_conv.reshape(DEPTH, bs, (CONV_W - 1) * CONV_DIM)

    delta_p, conv_p, hgrn_p, conv_s = [], [], [], []
    decode_states = ()
    for l in range(DEPTH):
        last = l == DEPTH - 1
        dense = dict(layer=l, n_prompt=bp)

        proj, cv = _in_proj_call(xp, mods, norm1_r, w_in_b, zero_buf, conv_w, lb_logits,
                                 per_token=False, seq_len=seq_len, **dense)
        proj = proj.reshape(bp, seq_len, PROMPT_COLS)
        mix_a, s_a = _gdn_call(proj, zero_state, a_log_rows, dt_bias_rows, norm_a_r, layer=l)
        mix_b, s_b = _hgrn_call(proj, norm_b_r, zero_state, layer=l)
        mix_a = mix_a.reshape(bp * seq_len, MIX_HALF)
        mix_b = mix_b.reshape(bp * seq_len, MIX_HALF)
        xp = _out_ffn_call(xp, mix_a, mix_b, 0, mods, w_out_b, norm2_r, w_gate_b, w_up_b, w_down_b,
                           norm_f_r, per_token=False, seq_len=seq_len, final_norm=last, **dense)
        delta_p.append(s_a)
        conv_p.append(cv)
        hgrn_p.append(s_b)

        proj = _in_proj_call(xs, mods, norm1_r, w_in_b, None, None, None,
                             per_token=True, seq_len=1, **dense)
        mix, cv, *decode_states = _decode_call(
            proj, conv_state, state_delta, state_hgrn, decode_states,
            conv_w, a_log_rows, dt_bias_rows, norm_a_r, norm_b_r, lb_logits, layer=l)
        xs = _out_ffn_call(xs, mix, mix, 1, mods, w_out_b, norm2_r, w_gate_b, w_up_b, w_down_b,
                           norm_f_r, per_token=True, seq_len=1, final_norm=last, **dense)
        conv_s.append(cv.reshape(bs, CONV_W - 1, CONV_DIM))

    delta_s, hgrn_s = decode_states
    return (xp.reshape(bp, seq_len, D_MODEL), xs.reshape(bs, 1, D_MODEL),
            jnp.stack(delta_p), jnp.stack(conv_p), jnp.stack(hgrn_p),
            delta_s, jnp.stack(conv_s), hgrn_s)
```

```python
import functools

import jax
import jax.numpy as jnp
from jax import lax
from jax.experimental import pallas as pl
from jax.experimental.pallas import tpu as pltpu

F32 = jnp.float32
BF16 = jnp.bfloat16

D_MODEL = 1024
DEPTH = 2
N_HEADS = 4
HEAD_DIM = 128
MIX_HALF = N_HEADS * HEAD_DIM
CONV_W = 4
CONV_DIM = 3 * MIX_HALF
D_FF = 2816
EPS = 1e-6
GDN_CHUNK = 128
GDN_SUB = 16
HGRN_CHUNK = 64

COL_QKV = 0
COL_Z = CONV_DIM
COL_B = COL_Z + MIX_HALF
COL_AB = COL_B + 4 * MIX_HALF
IN_COLS = COL_AB + 128
PROMPT_COL_G = COL_AB
PROMPT_COL_AB = PROMPT_COL_G + MIX_HALF
PROMPT_COLS = PROMPT_COL_AB + 128

MOD_SHIFT1, MOD_SCALE1, MOD_GATE1, MOD_SHIFT2, MOD_SCALE2, MOD_GATE2 = range(6)

VMEM_LIMIT = 56 * 1024 * 1024


def _sigmoid(x):
    return 1.0 / (1.0 + jnp.exp(-x))


def _silu(x):
    return x * _sigmoid(x)


def _softplus(x):
    return jnp.maximum(x, 0.0) + jnp.log1p(jnp.exp(-jnp.abs(x)))


def _mm(a, b):
    return jnp.dot(a.astype(BF16), b.astype(BF16), preferred_element_type=F32)


def _mm_nt(a, b):
    return lax.dot_general(a.astype(BF16), b.astype(BF16), (((1,), (1,)), ((), ())),
                           preferred_element_type=F32)


def _mm_tn(a, b):
    return lax.dot_general(a.astype(BF16), b.astype(BF16), (((0,), (0,)), ((), ())),
                           preferred_element_type=F32)


def _rms(x, w):
    return x * lax.rsqrt(jnp.mean(x * x, axis=-1, keepdims=True) + EPS) * w


def _chunk_cumsum(x, chunk):
    rows = lax.broadcasted_iota(jnp.int32, x.shape, 0) & (chunk - 1)
    s = 1
    while s < chunk:
        x = x + jnp.where(rows >= s, pltpu.roll(x, s, 0), 0.0)
        s *= 2
    return x


def _ada_kernel(c_ref, w_ref, b_ref, o_ref):
    a = _silu(c_ref[...])
    o_ref[...] = _mm(a, w_ref[...]) + b_ref[...]


def _ada_call(c_all, w_ada, b_ada):
    rows = c_all.shape[0]
    tn = 1024
    return pl.pallas_call(
        _ada_kernel,
        grid=(DEPTH, 6 * D_MODEL // tn),
        in_specs=[
            pl.BlockSpec((rows, D_MODEL), lambda l, j: (0, 0)),
            pl.BlockSpec((None, D_MODEL, tn), lambda l, j: (l, 0, j)),
            pl.BlockSpec((None, 1, tn), lambda l, j: (l, 0, j)),
        ],
        out_specs=pl.BlockSpec((None, rows, tn), lambda l, j: (l, 0, j)),
        out_shape=jax.ShapeDtypeStruct((DEPTH, rows, 6 * D_MODEL), F32),
        compiler_params=pltpu.CompilerParams(
            dimension_semantics=("arbitrary", "arbitrary"), vmem_limit_bytes=VMEM_LIMIT),
        name="ada_mod",
    )(c_all, w_ada, b_ada.reshape(DEPTH, 1, 6 * D_MODEL))


def _mod_spec(mods, layer, n_prompt, per_token, col):
    n_decode = mods.shape[1] - n_prompt
    if per_token:
        return pl.BlockSpec((None, n_decode, D_MODEL), lambda i: (layer, 0, col))
    return pl.BlockSpec((None, n_prompt, D_MODEL), lambda i: (layer, n_decode // n_prompt, col))


def _mod_rows(ref, tiles_per_seq):
    if tiles_per_seq is None:
        return ref[...]
    return ref[pl.ds(pl.program_id(0) // tiles_per_seq, 1), :]


def _layer_spec(shape, layer, **kwargs):
    zeros = (0,) * len(shape)
    return pl.BlockSpec((None,) + tuple(shape), lambda *_: (layer,) + zeros, **kwargs)


def _modulated_input(x_ref, scale_ref, shift_ref, nw_ref, tile, tiles_per_seq):
    if tiles_per_seq is None:
        scale, shift = scale_ref[...], shift_ref[...]
    else:
        seq = tile // tiles_per_seq
        scale, shift = scale_ref[pl.ds(seq, 1), :], shift_ref[pl.ds(seq, 1), :]
    return (_rms(x_ref[...], nw_ref[...]) * (1.0 + scale) + shift).astype(BF16)


def _in_proj_decode_kernel(x_ref, scale_ref, shift_ref, nw_ref, w_ref, o_ref):
    h = _modulated_input(x_ref, scale_ref, shift_ref, nw_ref, 0, None)
    o_ref[...] = jnp.dot(h, w_ref[...], preferred_element_type=F32)


def _in_proj_prompt_kernel(x_ref, scale_ref, shift_ref, nw_ref, w_ref, buf_ref, cw_ref, lbl_ref,
                           o_ref, cfin_ref, raw, h_scr, *, tiles_per_seq, layer):
    i = pl.program_id(0)
    tm = x_ref.shape[0]
    tile_in_seq = i % tiles_per_seq
    body = slice(8, 8 + tm)

    n_slabs = CONV_DIM // HEAD_DIM

    def slab_cols(ct):
        return slice(ct * HEAD_DIM, (ct + 1) * HEAD_DIM)

    @pl.when(tile_in_seq == 0)
    def _():
        for ct in range(n_slabs):
            raw[ct, 0:8, :] = buf_ref[:, slab_cols(ct)]

    h_scr[...] = _modulated_input(x_ref, scale_ref, shift_ref, nw_ref, i, tiles_per_seq)
    lb_all = _forget_lower_bound(lbl_ref[...], layer)

    def project(cs):
        return jnp.dot(h_scr[...], w_ref[:, cs], preferred_element_type=F32)

    def conv_slabs(cs):
        pre = project(cs)
        out = []
        for ct in range(cs.start // HEAD_DIM, cs.stop // HEAD_DIM):
            cols = slab_cols(ct)
            raw[ct, body, :] = pre[:, cols.start - cs.start:cols.stop - cs.start]
            acc = raw[ct, 5:5 + tm, :] * cw_ref[0:1, cols]
            for j in range(1, CONV_W):
                acc = acc + raw[ct, 5 + j:5 + j + tm, :] * cw_ref[j:j + 1, cols]
            out.append((cols, _silu(acc)))
        return out

    def l2_normalised(cs, scale):
        for cols, v in conv_slabs(cs):
            o_ref[:, cols] = v * lax.rsqrt(jnp.sum(v * v, axis=-1, keepdims=True) + EPS) * scale

    def post_q(cs):
        l2_normalised(cs, HEAD_DIM ** -0.5)

    def post_k(cs):
        l2_normalised(cs, 1.0)

    def post_v(cs):
        for cols, v in conv_slabs(cs):
            o_ref[:, cols] = v

    def post_copy(cs):
        o_ref[:, cs] = project(cs)

    def post_silu(cs):
        o_ref[:, cs] = _silu(project(cs))

    def post_forget(cs):
        gate_cols = slice(cs.start - COL_B - MIX_HALF, cs.stop - COL_B - MIX_HALF)
        lb = lb_all[:, gate_cols]
        f = lb + (1.0 - lb) * _sigmoid(project(cs))
        o_ref[:, cs] = 1.0 - f
        o_ref[:, PROMPT_COL_G + gate_cols.start:PROMPT_COL_G + gate_cols.stop] = (
            _chunk_cumsum(jnp.log(f), HGRN_CHUNK))

    width = 2 * HEAD_DIM
    pieces = ([post_q] * 2 + [post_k] * 2 + [post_v] * 2 + [post_copy] * 2
              + [post_silu] * 2 + [post_forget] * 2 + [post_copy] * 4)
    order = [0, 6, 1, 7, 2, 12, 3, 13, 4, 14, 5, 15, 8, 9, 10, 11]
    for piece in order:
        pieces[piece](slice(piece * width, (piece + 1) * width))
    o_ref[:, PROMPT_COL_AB:PROMPT_COLS] = project(slice(COL_AB, IN_COLS))

    for ct in range(n_slabs):
        raw[ct, 0:8, :] = raw[ct, tm:tm + 8, :]

    @pl.when(tile_in_seq == tiles_per_seq - 1)
    def _():
        for ct in range(n_slabs):
            cfin_ref[:, slab_cols(ct)] = raw[ct, 8 - (CONV_W - 1):8, :]


def _in_proj_call(x, mods, norm1, w_in, conv_buf, conv_w, lb_logits, *, layer, n_prompt, per_token,
                  seq_len):
    rows = x.shape[0]
    in_specs = [
        None,
        _mod_spec(mods, layer, n_prompt, per_token, MOD_SCALE1),
        _mod_spec(mods, layer, n_prompt, per_token, MOD_SHIFT1),
        _layer_spec((1, D_MODEL), layer),
        _layer_spec((D_MODEL, IN_COLS), layer, pipeline_mode=pl.Buffered(1)),
    ]
    params = pltpu.CompilerParams(dimension_semantics=("arbitrary",), vmem_limit_bytes=VMEM_LIMIT)
    if per_token:
        in_specs[0] = pl.BlockSpec((rows, D_MODEL), lambda i: (0, 0))
        return pl.pallas_call(
            _in_proj_decode_kernel,
            grid=(1,),
            in_specs=in_specs,
            out_specs=pl.BlockSpec((rows, IN_COLS), lambda i: (0, 0)),
            out_shape=jax.ShapeDtypeStruct((rows, IN_COLS), F32),
            compiler_params=params,
            name="in_proj_decode",
        )(x, mods, mods, norm1, w_in)

    tm = 512
    tiles_per_seq = seq_len // tm
    in_specs[0] = pl.BlockSpec((tm, D_MODEL), lambda i: (i, 0))
    in_specs += [
        pl.BlockSpec((None, 8, CONV_DIM), lambda i: (i // tiles_per_seq, 0, 0)),
        _layer_spec((CONV_W, CONV_DIM), layer),
        pl.BlockSpec((DEPTH, MIX_HALF), lambda i: (0, 0)),
    ]
    kern = functools.partial(_in_proj_prompt_kernel, tiles_per_seq=tiles_per_seq, layer=layer)
    return pl.pallas_call(
        kern,
        grid=(rows // tm,),
        in_specs=in_specs,
        out_specs=[
            pl.BlockSpec((tm, PROMPT_COLS), lambda i: (i, 0)),
            pl.BlockSpec((None, CONV_W - 1, CONV_DIM), lambda i: (i // tiles_per_seq, 0, 0)),
        ],
        out_shape=[
            jax.ShapeDtypeStruct((rows, PROMPT_COLS), F32),
            jax.ShapeDtypeStruct((n_prompt, CONV_W - 1, CONV_DIM), F32),
        ],
        scratch_shapes=[
            pltpu.VMEM((CONV_DIM // HEAD_DIM, tm + 8, HEAD_DIM), F32),
            pltpu.VMEM((tm, D_MODEL), BF16),
        ],
        compiler_params=params,
        name="in_proj_prompt",
    )(x, mods, mods, norm1, w_in, conv_buf, conv_w, lb_logits)


def _unit_lower_inverse(mats, sub_mask, eye):
    d = [jnp.where(sub_mask, a, 0.0) for a in mats]
    n = [a - x for a, x in zip(mats, d)]
    d2 = [_mm(x, x) for x in d]
    d4 = [_mm(x, x) for x in d2]
    d8 = [_mm(x, x) for x in d4]
    p = [eye - x + x2 - _mm(x, x2) for x, x2 in zip(d, d2)]
    p = [x + _mm(x, y) for x, y in zip(p, d4)]
    dinv = [x + _mm(x, y) for x, y in zip(p, d8)]
    e = [_mm(x, y) for x, y in zip(dinv, n)]
    e2 = [_mm(x, x) for x in e]
    e4 = [_mm(x, x) for x in e2]
    f = [x2 - x - _mm(x, x2) for x, x2 in zip(e, e2)]
    q = [x + y + _mm(x, y) for x, y in zip(f, e4)]
    return [x + _mm(y, x) for x, y in zip(dinv, q)]


def _gdn_kernel(qkv_ref, z_ref, ab_ref, s0_ref, alog_ref, dtb_ref, na_ref,
                mix_ref, sfin_ref, s_scr, *, tile):
    t = pl.program_id(1)

    @pl.when(t == 0)
    def _():
        s_scr[...] = s0_ref[...]

    ri = lax.broadcasted_iota(jnp.int32, (GDN_CHUNK, GDN_CHUNK), 0)
    ci = lax.broadcasted_iota(jnp.int32, (GDN_CHUNK, GDN_CHUNK), 1)
    m_incl = ri >= ci
    m_strict = ri > ci
    sub_mask = jnp.bitwise_xor(ri, ci) < GDN_SUB
    eye = jnp.where(ri == ci, 1.0, 0.0).astype(F32)
    n_chunks = tile // GDN_CHUNK
    pairs = [(b, h) for b in range(qkv_ref.shape[0]) for h in range(N_HEADS)]
    chains = [(p, c) for p in pairs for c in range(n_chunks)]

    def rows(c):
        return slice(c * GDN_CHUNK, (c + 1) * GDN_CHUNK)

    g_cum_t = {}
    qn, kn, kb, gc, eg, rhs = {}, {}, {}, {}, {}, {}
    for b in range(qkv_ref.shape[0]):
        ab = ab_ref[b]
        g_all = -jnp.exp(alog_ref[...]) * _softplus(ab + dtb_ref[...])
        sig_all = _sigmoid(ab)
        g_cum = _chunk_cumsum(g_all, GDN_CHUNK)
        g_cum_t[b] = g_cum.T
        for h in range(N_HEADS):
            p = (b, h)
            qn[p] = qkv_ref[b, :, h * HEAD_DIM:(h + 1) * HEAD_DIM]
            kn[p] = qkv_ref[b, :, MIX_HALF + h * HEAD_DIM:MIX_HALF + (h + 1) * HEAD_DIM]
            vh = qkv_ref[b, :, 2 * MIX_HALF + h * HEAD_DIM:2 * MIX_HALF + (h + 1) * HEAD_DIM]
            beta = sig_all[:, N_HEADS + h:N_HEADS + h + 1]
            gc[p] = g_cum[:, h:h + 1]
            eg[p] = jnp.exp(gc[p])
            kb[p] = kn[p] * beta
            rhs[p] = jnp.concatenate([vh * beta, kb[p] * eg[p]], axis=1)

    dec = {}
    for p, c in chains:
        gr = g_cum_t[p[0]][p[1]:p[1] + 1, rows(c)]
        dec[p, c] = jnp.exp(jnp.where(m_incl, gc[p][rows(c)] - gr, -jnp.inf))
    st = {(p, c): _mm_nt(jnp.concatenate([kb[p][rows(c)], qn[p][rows(c)]], axis=0), kn[p][rows(c)])
          for p, c in chains}
    a = [st[pc][:GDN_CHUNK] * jnp.where(m_strict, dec[pc], 0.0) for pc in chains]
    qk = {pc: st[pc][GDN_CHUNK:] * dec[pc] for pc in chains}
    tinv = dict(zip(chains, _unit_lower_inverse(a, sub_mask, eye)))
    uw = {(p, c): _mm(tinv[p, c], rhs[p][rows(c)]) for p, c in chains}

    s = {p: s_scr[p] for p in pairs}
    for c in range(n_chunks):
        rs = rows(c)
        r = {p: _mm(jnp.concatenate([uw[p, c][:, HEAD_DIM:], qn[p][rs] * eg[p][rs]], axis=0), s[p])
             for p in pairs}
        v_new = {p: uw[p, c][:, :HEAD_DIM] - r[p][:GDN_CHUNK] for p in pairs}
        for p in pairs:
            g_last = gc[p][(c + 1) * GDN_CHUNK - 1:(c + 1) * GDN_CHUNK, :]
            kd = kn[p][rs] * jnp.exp(g_last - gc[p][rs])
            s[p] = s[p] * jnp.exp(g_last) + _mm_tn(kd, v_new[p])
        for b, h in pairs:
            cs = slice(h * HEAD_DIM, (h + 1) * HEAD_DIM)
            o = r[b, h][GDN_CHUNK:] + _mm(qk[(b, h), c], v_new[b, h])
            mix_ref[b, rs, cs] = (_rms(o, na_ref[...]) * _silu(z_ref[b, rs, cs])).astype(mix_ref.dtype)
    for p in pairs:
        s_scr[p] = s[p]

    @pl.when(t == pl.num_programs(1) - 1)
    def _():
        sfin_ref[...] = s_scr[...]


MIXER_SEQS = 2
MIXER_TILE = 256


def _mixer_specs(cols, col_block):
    return pl.BlockSpec((MIXER_SEQS, MIXER_TILE, cols), lambda b, t: (b, t, col_block))


_STATE_BLOCK = (MIXER_SEQS, N_HEADS, HEAD_DIM, HEAD_DIM)


def _gdn_call(proj, s0, a_log_rows, dt_bias_rows, norm_a, *, layer):
    batch, seq_len, _ = proj.shape
    state_spec = pl.BlockSpec(_STATE_BLOCK, lambda b, t: (b, 0, 0, 0))
    return pl.pallas_call(
        functools.partial(_gdn_kernel, tile=MIXER_TILE),
        grid=(batch // MIXER_SEQS, seq_len // MIXER_TILE),
        in_specs=[
            _mixer_specs(CONV_DIM, COL_QKV // CONV_DIM),
            _mixer_specs(MIX_HALF, COL_Z // MIX_HALF),
            _mixer_specs(128, PROMPT_COL_AB // 128),
            state_spec,
            _layer_spec((1, 128), layer),
            _layer_spec((1, 128), layer),
            _layer_spec((1, HEAD_DIM), layer),
        ],
        out_specs=[_mixer_specs(MIX_HALF, 0), state_spec],
        out_shape=[
            jax.ShapeDtypeStruct((batch, seq_len, MIX_HALF), BF16),
            jax.ShapeDtypeStruct((batch, N_HEADS, HEAD_DIM, HEAD_DIM), F32),
        ],
        scratch_shapes=[pltpu.VMEM(_STATE_BLOCK, F32)],
        compiler_params=pltpu.CompilerParams(
            dimension_semantics=("arbitrary", "arbitrary"), vmem_limit_bytes=VMEM_LIMIT),
        name="gdn_prompt",
    )(proj, proj, proj, s0, a_log_rows, dt_bias_rows, norm_a)


def _forget_lower_bound(lbl, layer):
    m = jnp.max(lbl, axis=0, keepdims=True)
    e = jnp.exp(lbl - m)
    sm = e / jnp.sum(e, axis=0, keepdims=True)
    cs = sm[0:1, :]
    for i in range(1, layer + 1):
        cs = cs + sm[i:i + 1, :]
    return cs - sm[0:1, :]


def _midpoint_rows(g, s, row_id):
    n, width = g.shape
    if s >= 8:
        parts = [jnp.broadcast_to(g[b + s:b + s + 1, :], (2 * s, width))
                 for b in range(0, n, 2 * s)]
        return jnp.concatenate(parts, axis=0)
    tiles = g.reshape(n // 8, 8, width)
    sub = row_id.reshape(n // 8, 8, width) & 7
    mids = range(s, 8, 2 * s)
    out = jnp.broadcast_to(tiles[:, mids[-1]:mids[-1] + 1, :], tiles.shape)
    for m in reversed(mids[:-1]):
        out = jnp.where(sub < m + s, jnp.broadcast_to(tiles[:, m:m + 1, :], tiles.shape), out)
    return out.reshape(n, width)


def _hgrn_kernel(p_ref, g_ref, nb_ref, s0_ref, mix_ref, sfin_ref, st_scr, *, tile):
    t = pl.program_id(1)
    pairs = [(b, h) for b in range(p_ref.shape[0]) for h in range(N_HEADS)]

    @pl.when(t == 0)
    def _():
        for p in pairs:
            st_scr[p] = s0_ref[p].T

    n_chunks = tile // HGRN_CHUNK
    chains = [(p, c) for p in pairs for c in range(n_chunks)]

    def rows(c):
        return slice(c * HGRN_CHUNK, (c + 1) * HGRN_CHUNK)

    def cols(block, h):
        return slice(block * MIX_HALF + h * HEAD_DIM, block * MIX_HALF + (h + 1) * HEAD_DIM)

    row_id = lax.broadcasted_iota(jnp.int32, (tile, HEAD_DIM), 0)
    ri = lax.broadcasted_iota(jnp.int32, (HGRN_CHUNK, HGRN_CHUNK), 0)
    ci = lax.broadcasted_iota(jnp.int32, (HGRN_CHUNK, HGRN_CHUNK), 1)
    dist = jnp.bitwise_xor(ri, ci)
    lower = ri > ci

    q = {(b, h): p_ref[b, :, cols(0, h)] for b, h in pairs}
    k = {(b, h): p_ref[b, :, cols(1, h)] for b, h in pairs}
    g = {(b, h): g_ref[b, :, cols(0, h)] for b, h in pairs}

    chunk_row = row_id[:HGRN_CHUNK]
    levels = [HGRN_CHUNK >> (i + 1) for i in range(HGRN_CHUNK.bit_length() - 1)]
    upper = {s: (chunk_row & s) != 0 for s in levels}
    level_mask = {s: lower & (dist >= s) & (dist < 2 * s) for s in levels}
    a = {}
    for p, c in chains:
        qc, kc, gc = q[p][rows(c)], k[p][rows(c)], g[p][rows(c)]
        acc = jnp.where(ri == ci, _mm_nt(qc, kc), 0.0)
        for s in levels:
            d = gc - _midpoint_rows(gc, s, chunk_row)
            xc = jnp.where(upper[s], qc, kc) * jnp.exp(-jnp.abs(d))
            acc = jnp.where(level_mask[s], _mm_nt(xc, xc), acc)
        a[p, c] = acc

    st = {p: st_scr[p] for p in pairs}
    for c in range(n_chunks):
        rs = rows(c)
        for b, h in pairs:
            p = (b, h)
            gc = g[p][rs]
            vc = p_ref[b, rs, cols(2, h)]
            o = _mm(a[p, c], vc) + _mm_nt(q[p][rs] * jnp.exp(gc), st[p])
            g_last = gc[HGRN_CHUNK - 1:HGRN_CHUNK, :]
            st[p] = st[p] * jnp.exp(g_last) + _mm_tn(vc, k[p][rs] * jnp.exp(g_last - gc))
            gate = _sigmoid(p_ref[b, rs, cols(3, h)])
            mix_ref[b, rs, cols(0, h)] = (_rms(o, nb_ref[...]) * gate).astype(mix_ref.dtype)
    for p in pairs:
        st_scr[p] = st[p]

    @pl.when(t == pl.num_programs(1) - 1)
    def _():
        for p in pairs:
            sfin_ref[p] = st_scr[p].T


def _hgrn_call(proj, norm_b, s0, *, layer):
    batch, seq_len, _ = proj.shape
    state_spec = pl.BlockSpec(_STATE_BLOCK, lambda b, t: (b, 0, 0, 0))
    return pl.pallas_call(
        functools.partial(_hgrn_kernel, tile=MIXER_TILE),
        grid=(batch // MIXER_SEQS, seq_len // MIXER_TILE),
        in_specs=[
            _mixer_specs(4 * MIX_HALF, COL_B // (4 * MIX_HALF)),
            _mixer_specs(MIX_HALF, PROMPT_COL_G // MIX_HALF),
            _layer_spec((1, HEAD_DIM), layer),
            state_spec,
        ],
        out_specs=[_mixer_specs(MIX_HALF, 0), state_spec],
        out_shape=[
            jax.ShapeDtypeStruct((batch, seq_len, MIX_HALF), BF16),
            jax.ShapeDtypeStruct((batch, N_HEADS, HEAD_DIM, HEAD_DIM), F32),
        ],
        scratch_shapes=[pltpu.VMEM(_STATE_BLOCK, F32)],
        compiler_params=pltpu.CompilerParams(
            dimension_semantics=("arbitrary", "arbitrary"), vmem_limit_bytes=VMEM_LIMIT),
        name="hgrn_prompt",
    )(proj, proj, norm_b, s0)


def _decode_kernel(*refs, bt, layer):
    (qkv_ref, z_ref, ab_ref, pb_ref, conv_ref, sd_ref, sh_ref,
     cw_ref, alog_ref, dtb_ref, na_ref, nb_ref, lbl_ref) = refs[:13]
    mix_ref, convo_ref, sdo_ref, sho_ref, oa_scr, ob_scr = refs[-6:]
    if layer > 0:
        prev_sd_ref, prev_sh_ref = refs[13:15]
        sdo_ref[0:layer] = prev_sd_ref[...]
        sho_ref[0:layer] = prev_sh_ref[...]
    cw = cw_ref[...]
    u = qkv_ref[...]
    b0 = conv_ref[:, 0:CONV_DIM]
    b1 = conv_ref[:, CONV_DIM:2 * CONV_DIM]
    b2 = conv_ref[:, 2 * CONV_DIM:3 * CONV_DIM]
    acc = b0 * cw[0:1, :]
    acc = acc + b1 * cw[1:2, :]
    acc = acc + b2 * cw[2:3, :]
    acc = acc + u * cw[3:4, :]
    conv = _silu(acc)
    convo_ref[:, 0:CONV_DIM] = b1
    convo_ref[:, CONV_DIM:2 * CONV_DIM] = b2
    convo_ref[:, 2 * CONV_DIM:3 * CONV_DIM] = u

    ab = ab_ref[...]
    eg_all = jnp.exp(-jnp.exp(alog_ref[...]) * _softplus(ab + dtb_ref[...]))
    beta_all = _sigmoid(ab)

    lb = _forget_lower_bound(lbl_ref[...], layer)
    f = lb + (1.0 - lb) * _sigmoid(pb_ref[:, MIX_HALF:2 * MIX_HALF])
    qb = _silu(pb_ref[:, 0:MIX_HALF])

    k_rows, q_rows, v_rows = [], [], []
    for h in range(N_HEADS):
        cs = slice(h * HEAD_DIM, (h + 1) * HEAD_DIM)
        qh = conv[:, cs]
        kh = conv[:, MIX_HALF + h * HEAD_DIM:MIX_HALF + (h + 1) * HEAD_DIM]
        q_rows.append(qh * lax.rsqrt(jnp.sum(qh * qh, axis=-1, keepdims=True) + EPS)
                      * (HEAD_DIM ** -0.5))
        k_rows.append(kh * lax.rsqrt(jnp.sum(kh * kh, axis=-1, keepdims=True) + EPS))
        v_rows.append(conv[:, 2 * MIX_HALF + h * HEAD_DIM:2 * MIX_HALF + (h + 1) * HEAD_DIM])
    f_rows = [f[:, h * HEAD_DIM:(h + 1) * HEAD_DIM] for h in range(N_HEADS)]
    qb_rows = [qb[:, h * HEAD_DIM:(h + 1) * HEAD_DIM] for h in range(N_HEADS)]
    qk_a = [jnp.sum(q_rows[h] * k_rows[h], axis=-1, keepdims=True) for h in range(N_HEADS)]
    qk_b = [jnp.sum(qb_rows[h] * (1.0 - f_rows[h]), axis=-1, keepdims=True) for h in range(N_HEADS)]

    f_cols = jnp.concatenate(f_rows + [jnp.zeros(((16 - N_HEADS) * bt, HEAD_DIM), F32)], axis=0).T

    row = lax.broadcasted_iota(jnp.int32, (8, HEAD_DIM), 0)

    def rows8(first, second=None):
        out = jnp.where(row == 0, first, 0.0)
        return out if second is None else jnp.where(row == 1, second, out)

    for b in range(bt):
        bs = slice(b, b + 1)
        for h in range(N_HEADS):
            cs = slice(h * HEAD_DIM, (h + 1) * HEAD_DIM)
            s = sd_ref[b, h]
            kq = rows8(k_rows[h][bs, :], q_rows[h][bs, :])
            eg = eg_all[bs, h:h + 1]
            beta = beta_all[bs, N_HEADS + h:N_HEADS + h + 1]
            r = _mm(kq, s)
            v_new = beta * (v_rows[h][bs, :] - eg * r[0:1, :])
            sdo_ref[layer, b, h] = s * eg + _mm_tn(kq, rows8(v_new))
            oa_scr[bs, cs] = eg * r[1:2, :] + qk_a[h][bs, :] * v_new
            s = sh_ref[b, h]
            fc = f_cols[:, h * bt + b:h * bt + b + 1]
            vb = pb_ref[bs, 2 * MIX_HALF + h * HEAD_DIM:2 * MIX_HALF + (h + 1) * HEAD_DIM]
            r = _mm(rows8(qb_rows[h][bs, :] * f_rows[h][bs, :]), s)
            sho_ref[layer, b, h] = s * fc + _mm_tn(rows8(1.0 - f_rows[h][bs, :]), rows8(vb))
            ob_scr[bs, cs] = r[0:1, :] + qk_b[h][bs, :] * vb

    for h in range(N_HEADS):
        cs = slice(h * HEAD_DIM, (h + 1) * HEAD_DIM)
        mix_ref[:, cs] = (_rms(oa_scr[:, cs], na_ref[...]) * _silu(z_ref[:, cs])).astype(mix_ref.dtype)
        gate = _sigmoid(pb_ref[:, 3 * MIX_HALF + h * HEAD_DIM:3 * MIX_HALF + (h + 1) * HEAD_DIM])
        mix_ref[:, MIX_HALF + h * HEAD_DIM:MIX_HALF + (h + 1) * HEAD_DIM] = (
            _rms(ob_scr[:, cs], nb_ref[...]) * gate).astype(mix_ref.dtype)


def _decode_call(proj, conv_state, s_delta, s_hgrn, prev_states, conv_w, a_log_rows, dt_bias_rows,
                 norm_a, norm_b, lb_logits, *, layer):
    batch = proj.shape[0]
    bt = 8
    kern = functools.partial(_decode_kernel, bt=bt, layer=layer)
    state_dims = (bt, N_HEADS, HEAD_DIM, HEAD_DIM)
    old_state_spec = pl.BlockSpec((None,) + state_dims, lambda i: (layer, i, 0, 0, 0))
    prev_state_spec = pl.BlockSpec((layer,) + state_dims, lambda i: (0, i, 0, 0, 0))
    new_state_spec = pl.BlockSpec((layer + 1,) + state_dims, lambda i: (0, i, 0, 0, 0))
    new_state_shape = jax.ShapeDtypeStruct((layer + 1, batch, N_HEADS, HEAD_DIM, HEAD_DIM), F32)
    return pl.pallas_call(
        kern,
        grid=(batch // bt,),
        in_specs=[
            pl.BlockSpec((bt, CONV_DIM), lambda i: (i, COL_QKV // CONV_DIM)),
            pl.BlockSpec((bt, MIX_HALF), lambda i: (i, COL_Z // MIX_HALF)),
            pl.BlockSpec((bt, 128), lambda i: (i, COL_AB // 128)),
            pl.BlockSpec((bt, 4 * MIX_HALF), lambda i: (i, COL_B // (4 * MIX_HALF))),
            pl.BlockSpec((None, bt, (CONV_W - 1) * CONV_DIM), lambda i: (layer, i, 0)),
            old_state_spec,
            old_state_spec,
            _layer_spec((CONV_W, CONV_DIM), layer),
            _layer_spec((1, 128), layer),
            _layer_spec((1, 128), layer),
            _layer_spec((1, HEAD_DIM), layer),
            _layer_spec((1, HEAD_DIM), layer),
            pl.BlockSpec((DEPTH, MIX_HALF), lambda i: (0, 0)),
        ] + [prev_state_spec] * len(prev_states),
        out_specs=[
            pl.BlockSpec((bt, 2 * MIX_HALF), lambda i: (i, 0)),
            pl.BlockSpec((bt, (CONV_W - 1) * CONV_DIM), lambda i: (i, 0)),
            new_state_spec,
            new_state_spec,
        ],
        out_shape=[
            jax.ShapeDtypeStruct((batch, 2 * MIX_HALF), BF16),
            jax.ShapeDtypeStruct((batch, (CONV_W - 1) * CONV_DIM), F32),
            new_state_shape,
            new_state_shape,
        ],
        scratch_shapes=[
            pltpu.VMEM((bt, MIX_HALF), F32),
            pltpu.VMEM((bt, MIX_HALF), F32),
        ],
        compiler_params=pltpu.CompilerParams(
            dimension_semantics=("arbitrary",), vmem_limit_bytes=VMEM_LIMIT),
        name="mixers_decode",
    )(proj, proj, proj, proj, conv_state, s_delta, s_hgrn, conv_w, a_log_rows, dt_bias_rows,
      norm_a, norm_b, lb_logits, *prev_states)


FF_CHUNK = 256


def _out_ffn_kernel(x_ref, ma_ref, mb_ref, g1_ref, sh2_ref, sc2_ref, g2_ref, wo_ref, n2_ref,
                    wg_ref, wu_ref, wd_ref, nf_ref, o_ref, acc_ref, *, final_norm, tiles_per_seq):
    gate1 = _mod_rows(g1_ref, tiles_per_seq)
    shift2 = _mod_rows(sh2_ref, tiles_per_seq)
    scale2 = _mod_rows(sc2_ref, tiles_per_seq)
    gate2 = _mod_rows(g2_ref, tiles_per_seq)
    mix = _mm(ma_ref[...], wo_ref[0:MIX_HALF, :]) + _mm(mb_ref[...], wo_ref[MIX_HALF:, :])
    x1 = x_ref[...] + gate1 * mix
    h = (_rms(x1, n2_ref[...]) * (1.0 + scale2) + shift2).astype(BF16)
    for c in range(D_FF // FF_CHUNK):
        fs = slice(c * FF_CHUNK, (c + 1) * FF_CHUNK)
        gate = jnp.dot(h, wg_ref[:, fs], preferred_element_type=F32)
        up = jnp.dot(h, wu_ref[:, fs], preferred_element_type=F32)
        part = _mm(_silu(gate) * up, wd_ref[fs, :])
        if c == 0:
            acc_ref[...] = part
        else:
            acc_ref[...] += part
    x2 = x1 + gate2 * acc_ref[...]
    if final_norm:
        x2 = _rms(x2, nf_ref[...])
    o_ref[...] = x2


def _out_ffn_call(x, mix_a, mix_b, mix_b_col, mods, w_out, norm2, w_gate, w_up, w_down, norm_f,
                  *, layer, n_prompt, per_token, seq_len, final_norm):
    rows = x.shape[0]
    tm = rows if per_token else 512
    tiles_per_seq = None if per_token else seq_len // tm
    resident = functools.partial(_layer_spec, layer=layer, pipeline_mode=pl.Buffered(1))
    mod = functools.partial(_mod_spec, mods, layer, n_prompt, per_token)
    kern = functools.partial(_out_ffn_kernel, final_norm=final_norm, tiles_per_seq=tiles_per_seq)
    return pl.pallas_call(
        kern,
        grid=(rows // tm,),
        in_specs=[
            pl.BlockSpec((tm, D_MODEL), lambda i: (i, 0)),
            pl.BlockSpec((tm, MIX_HALF), lambda i: (i, 0)),
            pl.BlockSpec((tm, MIX_HALF), lambda i: (i, mix_b_col)),
            mod(MOD_GATE1),
            mod(MOD_SHIFT2),
            mod(MOD_SCALE2),
            mod(MOD_GATE2),
            resident((D_MODEL, D_MODEL)),
            _layer_spec((1, D_MODEL), layer),
            resident((D_MODEL, D_FF)),
            resident((D_MODEL, D_FF)),
            resident((D_FF, D_MODEL)),
            pl.BlockSpec((1, D_MODEL), lambda i: (0, 0)),
        ],
        out_specs=pl.BlockSpec((tm, D_MODEL), lambda i: (i, 0)),
        out_shape=jax.ShapeDtypeStruct((rows, D_MODEL), F32),
        scratch_shapes=[pltpu.VMEM((tm, D_MODEL), F32)],
        compiler_params=pltpu.CompilerParams(
            dimension_semantics=("arbitrary",), vmem_limit_bytes=VMEM_LIMIT),
        name="out_ffn",
    )(x, mix_a, mix_b, mods, mods, mods, mods, w_out, norm2, w_gate, w_up, w_down, norm_f)


N_AB = 2 * N_HEADS
REORDER_COLS = 512
AB_BLOCK = COL_AB // REORDER_COLS


def _reorder_in_proj_kernel(wt_ref, o_ref):
    w = wt_ref[0]
    row = lax.broadcasted_iota(jnp.int32, w.shape, 0)
    w = jnp.where((pl.program_id(1) == AB_BLOCK) & (row >= N_AB), 0.0, w)
    o_ref[...] = w.T.astype(BF16)


def _reorder_in_proj(w):
    wt = jnp.swapaxes(w, 1, 2)

    def src_row(l, j):
        is_ab = j // AB_BLOCK
        is_b = j // (COL_B // REORDER_COLS) - 2 * is_ab
        start = j * REORDER_COLS + N_AB * is_b - (COL_AB - COL_B) * is_ab
        return (l, pl.multiple_of(start, 8), 0)

    return pl.pallas_call(
        _reorder_in_proj_kernel,
        grid=(DEPTH, pl.cdiv(IN_COLS, REORDER_COLS)),
        in_specs=[pl.BlockSpec((pl.Element(1), pl.Element(REORDER_COLS), pl.Element(D_MODEL)),
                               src_row)],
        out_specs=pl.BlockSpec((None, D_MODEL, REORDER_COLS), lambda l, j: (l, 0, j)),
        out_shape=jax.ShapeDtypeStruct((DEPTH, D_MODEL, IN_COLS), BF16),
        compiler_params=pltpu.CompilerParams(
            dimension_semantics=("arbitrary", "arbitrary"), vmem_limit_bytes=VMEM_LIMIT),
        name="reorder_in_proj",
    )(wt)


def _lane_rows(v):
    return jnp.pad(v.astype(F32), ((0, 0), (0, 128 - v.shape[1])))[:, None, :]


def kernel(x_prompt, x_sample, c_prompt, c_sample, state_delta, state_conv, state_hgrn, w_ada, b_ada,
           norm1, w_in, conv_w, a_log, dt_bias, norm_a, norm_b, lb_logits, w_out, norm2, w_gate, w_up,
           w_down, norm_f):
    bp, seq_len, _ = x_prompt.shape
    bs = x_sample.shape[0]

    mods = _ada_call(jnp.concatenate([c_sample, c_prompt], axis=0), w_ada, b_ada)

    xp = x_prompt.reshape(bp * seq_len, D_MODEL)
    xs = x_sample.reshape(bs, D_MODEL)
    zero_buf = jnp.zeros((bp, 8, CONV_DIM), F32)
    zero_state = jnp.zeros((bp, N_HEADS, HEAD_DIM, HEAD_DIM), F32)

    w_in_b = _reorder_in_proj(w_in)
    w_out_b = w_out.astype(BF16)
    w_gate_b = w_gate.astype(BF16)
    w_up_b = w_up.astype(BF16)
    w_down_b = w_down.astype(BF16)
    a_log_rows = _lane_rows(a_log)
    dt_bias_rows = _lane_rows(dt_bias)
    norm1_r = norm1[:, None, :]
    norm2_r = norm2[:, None, :]
    norm_a_r = norm_a[:, None, :]
    norm_b_r = norm_b[:, None, :]
    norm_f_r = norm_f[None, :]
    conv_state = state_conv.reshape(DEPTH, bs, (CONV_W - 1) * CONV_DIM)

    delta_p, conv_p, hgrn_p, conv_s = [], [], [], []
    decode_states = ()
    for l in range(DEPTH):
        last = l == DEPTH - 1
        dense = dict(layer=l, n_prompt=bp)

        proj, cv = _in_proj_call(xp, mods, norm1_r, w_in_b, zero_buf, conv_w, lb_logits,
                                 per_token=False, seq_len=seq_len, **dense)
        proj = proj.reshape(bp, seq_len, PROMPT_COLS)
        mix_a, s_a = _gdn_call(proj, zero_state, a_log_rows, dt_bias_rows, norm_a_r, layer=l)
        mix_b, s_b = _hgrn_call(proj, norm_b_r, zero_state, layer=l)
        mix_a = mix_a.reshape(bp * seq_len, MIX_HALF)
        mix_b = mix_b.reshape(bp * seq_len, MIX_HALF)
        xp = _out_ffn_call(xp, mix_a, mix_b, 0, mods, w_out_b, norm2_r, w_gate_b, w_up_b, w_down_b,
                           norm_f_r, per_token=False, seq_len=seq_len, final_norm=last, **dense)
        delta_p.append(s_a)
        conv_p.append(cv)
        hgrn_p.append(s_b)

        proj = _in_proj_call(xs, mods, norm1_r, w_in_b, None, None, None,
                             per_token=True, seq_len=1, **dense)
        mix, cv, *decode_states = _decode_call(
            proj, conv_state, state_delta, state_hgrn, decode_states,
            conv_w, a_log_rows, dt_bias_rows, norm_a_r, norm_b_r, lb_logits, layer=l)
        xs = _out_ffn_call(xs, mix, mix, 1, mods, w_out_b, norm2_r, w_gate_b, w_up_b, w_down_b,
                           norm_f_r, per_token=True, seq_len=1, final_norm=last, **dense)
        conv_s.append(cv.reshape(bs, CONV_W - 1, CONV_DIM))

    delta_s, hgrn_s = decode_states
    return (xp.reshape(bp, seq_len, D_MODEL), xs.reshape(bs, 1, D_MODEL),
            jnp.stack(delta_p), jnp.stack(conv_p), jnp.stack(hgrn_p),
            delta_s, jnp.stack(conv_s), hgrn_s)
```

```python
import functools

import jax
import jax.numpy as jnp
from jax import lax
from jax.experimental import pallas as pl
from jax.experimental.pallas import tpu as pltpu

F32 = jnp.float32
BF16 = jnp.bfloat16

D_MODEL = 1024
DEPTH = 2
N_HEADS = 4
HEAD_DIM = 128
MIX_HALF = N_HEADS * HEAD_DIM
CONV_W = 4
CONV_DIM = 3 * MIX_HALF
D_FF = 2816
EPS = 1e-6
GDN_CHUNK = 128
GDN_SUB = 16
HGRN_CHUNK = 64

COL_QKV = 0
COL_Z = CONV_DIM
COL_B = COL_Z + MIX_HALF
COL_AB = COL_B + 4 * MIX_HALF
IN_COLS = COL_AB + 128
PROMPT_COL_G = COL_AB
PROMPT_COL_AB = PROMPT_COL_G + MIX_HALF
PROMPT_COLS = PROMPT_COL_AB + 128

MOD_SHIFT1, MOD_SCALE1, MOD_GATE1, MOD_SHIFT2, MOD_SCALE2, MOD_GATE2 = range(6)

VMEM_LIMIT = 56 * 1024 * 1024


def _sigmoid(x):
    return 1.0 / (1.0 + jnp.exp(-x))


def _silu(x):
    return x * _sigmoid(x)


def _softplus(x):
    return jnp.maximum(x, 0.0) + jnp.log1p(jnp.exp(-jnp.abs(x)))


def _mm(a, b):
    return jnp.dot(a.astype(BF16), b.astype(BF16), preferred_element_type=F32)


def _mm_nt(a, b):
    return lax.dot_general(a.astype(BF16), b.astype(BF16), (((1,), (1,)), ((), ())),
                           preferred_element_type=F32)


def _mm_tn(a, b):
    return lax.dot_general(a.astype(BF16), b.astype(BF16), (((0,), (0,)), ((), ())),
                           preferred_element_type=F32)


def _rms(x, w):
    return x * lax.rsqrt(jnp.mean(x * x, axis=-1, keepdims=True) + EPS) * w


def _chunk_cumsum(x, chunk):
    rows = lax.broadcasted_iota(jnp.int32, x.shape, 0) & (chunk - 1)
    s = 1
    while s < chunk:
        x = x + jnp.where(rows >= s, pltpu.roll(x, s, 0), 0.0)
        s *= 2
    return x


def _ada_kernel(c_ref, w_ref, b_ref, o_ref):
    a = _silu(c_ref[...])
    o_ref[...] = _mm(a, w_ref[...]) + b_ref[...]


def _ada_call(c_all, w_ada, b_ada):
    rows = c_all.shape[0]
    tn = 2048
    return pl.pallas_call(
        _ada_kernel,
        grid=(DEPTH, 6 * D_MODEL // tn),
        in_specs=[
            pl.BlockSpec((rows, D_MODEL), lambda l, j: (0, 0)),
            pl.BlockSpec((None, D_MODEL, tn), lambda l, j: (l, 0, j)),
            pl.BlockSpec((None, 1, tn), lambda l, j: (l, 0, j)),
        ],
        out_specs=pl.BlockSpec((None, rows, tn), lambda l, j: (l, 0, j)),
        out_shape=jax.ShapeDtypeStruct((DEPTH, rows, 6 * D_MODEL), F32),
        compiler_params=pltpu.CompilerParams(
            dimension_semantics=("arbitrary", "arbitrary"), vmem_limit_bytes=VMEM_LIMIT),
        name="ada_mod",
    )(c_all, w_ada, b_ada.reshape(DEPTH, 1, 6 * D_MODEL))


def _mod_spec(mods, layer, n_prompt, per_token, col):
    n_decode = mods.shape[1] - n_prompt
    if per_token:
        return pl.BlockSpec((None, n_decode, D_MODEL), lambda i: (layer, 0, col))
    return pl.BlockSpec((None, n_prompt, D_MODEL), lambda i: (layer, n_decode // n_prompt, col))


def _mod_rows(ref, tiles_per_seq):
    if tiles_per_seq is None:
        return ref[...]
    return ref[pl.ds(pl.program_id(0) // tiles_per_seq, 1), :]


def _layer_spec(shape, layer, **kwargs):
    zeros = (0,) * len(shape)
    return pl.BlockSpec((None,) + tuple(shape), lambda *_: (layer,) + zeros, **kwargs)


def _modulated_input(x_ref, scale_ref, shift_ref, nw_ref, tile, tiles_per_seq):
    if tiles_per_seq is None:
        scale, shift = scale_ref[...], shift_ref[...]
    else:
        seq = tile // tiles_per_seq
        scale, shift = scale_ref[pl.ds(seq, 1), :], shift_ref[pl.ds(seq, 1), :]
    return (_rms(x_ref[...], nw_ref[...]) * (1.0 + scale) + shift).astype(BF16)


def _in_proj_decode_kernel(x_ref, scale_ref, shift_ref, nw_ref, w_ref, o_ref):
    h = _modulated_input(x_ref, scale_ref, shift_ref, nw_ref, 0, None)
    o_ref[...] = jnp.dot(h, w_ref[...], preferred_element_type=F32)


def _in_proj_prompt_kernel(x_ref, scale_ref, shift_ref, nw_ref, w_ref, buf_ref, cw_ref, lbl_ref,
                           o_ref, cfin_ref, raw, h_scr, *, tiles_per_seq, layer):
    i = pl.program_id(0)
    tm = x_ref.shape[0]
    tile_in_seq = i % tiles_per_seq
    body = slice(8, 8 + tm)

    n_slabs = CONV_DIM // HEAD_DIM

    def slab_cols(ct):
        return slice(ct * HEAD_DIM, (ct + 1) * HEAD_DIM)

    @pl.when(tile_in_seq == 0)
    def _():
        for ct in range(n_slabs):
            raw[ct, 0:8, :] = buf_ref[:, slab_cols(ct)]

    h_scr[...] = _modulated_input(x_ref, scale_ref, shift_ref, nw_ref, i, tiles_per_seq)
    lb_all = _forget_lower_bound(lbl_ref[...], layer)

    def project(cs):
        return jnp.dot(h_scr[...], w_ref[:, cs], preferred_element_type=F32)

    def conv_slabs(cs):
        pre = project(cs)
        out = []
        for ct in range(cs.start // HEAD_DIM, cs.stop // HEAD_DIM):
            cols = slab_cols(ct)
            raw[ct, body, :] = pre[:, cols.start - cs.start:cols.stop - cs.start]
            acc = raw[ct, 5:5 + tm, :] * cw_ref[0:1, cols]
            for j in range(1, CONV_W):
                acc = acc + raw[ct, 5 + j:5 + j + tm, :] * cw_ref[j:j + 1, cols]
            out.append((cols, _silu(acc)))
        return out

    def l2_normalised(cs, scale):
        for cols, v in conv_slabs(cs):
            o_ref[:, cols] = v * lax.rsqrt(jnp.sum(v * v, axis=-1, keepdims=True) + EPS) * scale

    def post_q(cs):
        l2_normalised(cs, HEAD_DIM ** -0.5)

    def post_k(cs):
        l2_normalised(cs, 1.0)

    def post_v(cs):
        for cols, v in conv_slabs(cs):
            o_ref[:, cols] = v

    def post_copy(cs):
        o_ref[:, cs] = project(cs)

    def post_silu(cs):
        o_ref[:, cs] = _silu(project(cs))

    def post_forget(cs):
        gate_cols = slice(cs.start - COL_B - MIX_HALF, cs.stop - COL_B - MIX_HALF)
        lb = lb_all[:, gate_cols]
        f = lb + (1.0 - lb) * _sigmoid(project(cs))
        o_ref[:, cs] = 1.0 - f
        o_ref[:, PROMPT_COL_G + gate_cols.start:PROMPT_COL_G + gate_cols.stop] = (
            _chunk_cumsum(jnp.log(f), HGRN_CHUNK))

    width = 2 * HEAD_DIM
    pieces = ([post_q] * 2 + [post_k] * 2 + [post_v] * 2 + [post_copy] * 2
              + [post_silu] * 2 + [post_forget] * 2 + [post_copy] * 4)
    order = [0, 6, 1, 7, 2, 12, 3, 13, 4, 14, 5, 15, 8, 9, 10, 11]
    for piece in order:
        pieces[piece](slice(piece * width, (piece + 1) * width))
    o_ref[:, PROMPT_COL_AB:PROMPT_COLS] = project(slice(COL_AB, IN_COLS))

    for ct in range(n_slabs):
        raw[ct, 0:8, :] = raw[ct, tm:tm + 8, :]

    @pl.when(tile_in_seq == tiles_per_seq - 1)
    def _():
        for ct in range(n_slabs):
            cfin_ref[:, slab_cols(ct)] = raw[ct, 8 - (CONV_W - 1):8, :]


def _in_proj_call(x, mods, norm1, w_in, conv_buf, conv_w, lb_logits, *, layer, n_prompt, per_token,
                  seq_len):
    rows = x.shape[0]
    in_specs = [
        None,
        _mod_spec(mods, layer, n_prompt, per_token, MOD_SCALE1),
        _mod_spec(mods, layer, n_prompt, per_token, MOD_SHIFT1),
        _layer_spec((1, D_MODEL), layer),
        _layer_spec((D_MODEL, IN_COLS), layer, pipeline_mode=pl.Buffered(1)),
    ]
    params = pltpu.CompilerParams(dimension_semantics=("arbitrary",), vmem_limit_bytes=VMEM_LIMIT)
    if per_token:
        in_specs[0] = pl.BlockSpec((rows, D_MODEL), lambda i: (0, 0))
        return pl.pallas_call(
            _in_proj_decode_kernel,
            grid=(1,),
            in_specs=in_specs,
            out_specs=pl.BlockSpec((rows, IN_COLS), lambda i: (0, 0)),
            out_shape=jax.ShapeDtypeStruct((rows, IN_COLS), F32),
            compiler_params=params,
            name="in_proj_decode",
        )(x, mods, mods, norm1, w_in)

    tm = 512
    tiles_per_seq = seq_len // tm
    in_specs[0] = pl.BlockSpec((tm, D_MODEL), lambda i: (i, 0))
    in_specs += [
        pl.BlockSpec((None, 8, CONV_DIM), lambda i: (i // tiles_per_seq, 0, 0)),
        _layer_spec((CONV_W, CONV_DIM), layer),
        pl.BlockSpec((DEPTH, MIX_HALF), lambda i: (0, 0)),
    ]
    kern = functools.partial(_in_proj_prompt_kernel, tiles_per_seq=tiles_per_seq, layer=layer)
    return pl.pallas_call(
        kern,
        grid=(rows // tm,),
        in_specs=in_specs,
        out_specs=[
            pl.BlockSpec((tm, PROMPT_COLS), lambda i: (i, 0)),
            pl.BlockSpec((None, CONV_W - 1, CONV_DIM), lambda i: (i // tiles_per_seq, 0, 0)),
        ],
        out_shape=[
            jax.ShapeDtypeStruct((rows, PROMPT_COLS), F32),
            jax.ShapeDtypeStruct((n_prompt, CONV_W - 1, CONV_DIM), F32),
        ],
        scratch_shapes=[
            pltpu.VMEM((CONV_DIM // HEAD_DIM, tm + 8, HEAD_DIM), F32),
            pltpu.VMEM((tm, D_MODEL), BF16),
        ],
        compiler_params=params,
        name="in_proj_prompt",
    )(x, mods, mods, norm1, w_in, conv_buf, conv_w, lb_logits)


def _unit_lower_inverse(mats, sub_mask, eye):
    d = [jnp.where(sub_mask, a, 0.0) for a in mats]
    n = [a - x for a, x in zip(mats, d)]
    d2 = [_mm(x, x) for x in d]
    d4 = [_mm(x, x) for x in d2]
    d8 = [_mm(x, x) for x in d4]
    p = [eye - x + x2 - _mm(x, x2) for x, x2 in zip(d, d2)]
    p = [x + _mm(x, y) for x, y in zip(p, d4)]
    dinv = [x + _mm(x, y) for x, y in zip(p, d8)]
    e = [_mm(x, y) for x, y in zip(dinv, n)]
    e2 = [_mm(x, x) for x in e]
    e4 = [_mm(x, x) for x in e2]
    f = [x2 - x - _mm(x, x2) for x, x2 in zip(e, e2)]
    q = [x + y + _mm(x, y) for x, y in zip(f, e4)]
    return [x + _mm(y, x) for x, y in zip(dinv, q)]


def _gdn_kernel(qkv_ref, z_ref, ab_ref, s0_ref, alog_ref, dtb_ref, na_ref,
                mix_ref, sfin_ref, s_scr, *, tile):
    t = pl.program_id(1)

    @pl.when(t == 0)
    def _():
        s_scr[...] = s0_ref[...]

    ri = lax.broadcasted_iota(jnp.int32, (GDN_CHUNK, GDN_CHUNK), 0)
    ci = lax.broadcasted_iota(jnp.int32, (GDN_CHUNK, GDN_CHUNK), 1)
    m_incl = ri >= ci
    m_strict = ri > ci
    sub_mask = jnp.bitwise_xor(ri, ci) < GDN_SUB
    eye = jnp.where(ri == ci, 1.0, 0.0).astype(F32)
    n_chunks = tile // GDN_CHUNK
    pairs = [(b, h) for b in range(qkv_ref.shape[0]) for h in range(N_HEADS)]
    chains = [(p, c) for p in pairs for c in range(n_chunks)]

    def rows(c):
        return slice(c * GDN_CHUNK, (c + 1) * GDN_CHUNK)

    g_cum_t = {}
    qn, kn, kb, gc, eg, rhs = {}, {}, {}, {}, {}, {}
    for b in range(qkv_ref.shape[0]):
        ab = ab_ref[b]
        g_all = -jnp.exp(alog_ref[...]) * _softplus(ab + dtb_ref[...])
        sig_all = _sigmoid(ab)
        g_cum = _chunk_cumsum(g_all, GDN_CHUNK)
        g_cum_t[b] = g_cum.T
        for h in range(N_HEADS):
            p = (b, h)
            qn[p] = qkv_ref[b, :, h * HEAD_DIM:(h + 1) * HEAD_DIM]
            kn[p] = qkv_ref[b, :, MIX_HALF + h * HEAD_DIM:MIX_HALF + (h + 1) * HEAD_DIM]
            vh = qkv_ref[b, :, 2 * MIX_HALF + h * HEAD_DIM:2 * MIX_HALF + (h + 1) * HEAD_DIM]
            beta = sig_all[:, N_HEADS + h:N_HEADS + h + 1]
            gc[p] = g_cum[:, h:h + 1]
            eg[p] = jnp.exp(gc[p])
            kb[p] = kn[p] * beta
            rhs[p] = jnp.concatenate([vh * beta, kb[p] * eg[p]], axis=1)

    dec = {}
    for p, c in chains:
        gr = g_cum_t[p[0]][p[1]:p[1] + 1, rows(c)]
        dec[p, c] = jnp.exp(jnp.where(m_incl, gc[p][rows(c)] - gr, -jnp.inf))
    st = {(p, c): _mm_nt(jnp.concatenate([kb[p][rows(c)], qn[p][rows(c)]], axis=0), kn[p][rows(c)])
          for p, c in chains}
    a = [st[pc][:GDN_CHUNK] * jnp.where(m_strict, dec[pc], 0.0) for pc in chains]
    qk = {pc: st[pc][GDN_CHUNK:] * dec[pc] for pc in chains}
    tinv = dict(zip(chains, _unit_lower_inverse(a, sub_mask, eye)))
    uw = {(p, c): _mm(tinv[p, c], rhs[p][rows(c)]) for p, c in chains}

    s = {p: s_scr[p] for p in pairs}
    for c in range(n_chunks):
        rs = rows(c)
        r = {p: _mm(jnp.concatenate([uw[p, c][:, HEAD_DIM:], qn[p][rs] * eg[p][rs]], axis=0), s[p])
             for p in pairs}
        v_new = {p: uw[p, c][:, :HEAD_DIM] - r[p][:GDN_CHUNK] for p in pairs}
        for p in pairs:
            g_last = gc[p][(c + 1) * GDN_CHUNK - 1:(c + 1) * GDN_CHUNK, :]
            kd = kn[p][rs] * jnp.exp(g_last - gc[p][rs])
            s[p] = s[p] * jnp.exp(g_last) + _mm_tn(kd, v_new[p])
        for b, h in pairs:
            cs = slice(h * HEAD_DIM, (h + 1) * HEAD_DIM)
            o = r[b, h][GDN_CHUNK:] + _mm(qk[(b, h), c], v_new[b, h])
            mix_ref[b, rs, cs] = (_rms(o, na_ref[...]) * _silu(z_ref[b, rs, cs])).astype(mix_ref.dtype)
    for p in pairs:
        s_scr[p] = s[p]

    @pl.when(t == pl.num_programs(1) - 1)
    def _():
        sfin_ref[...] = s_scr[...]


MIXER_SEQS = 2
MIXER_TILE = 256


def _mixer_specs(cols, col_block):
    return pl.BlockSpec((MIXER_SEQS, MIXER_TILE, cols), lambda b, t: (b, t, col_block))


_STATE_BLOCK = (MIXER_SEQS, N_HEADS, HEAD_DIM, HEAD_DIM)


def _gdn_call(proj, s0, a_log_rows, dt_bias_rows, norm_a, *, layer):
    batch, seq_len, _ = proj.shape
    state_spec = pl.BlockSpec(_STATE_BLOCK, lambda b, t: (b, 0, 0, 0))
    return pl.pallas_call(
        functools.partial(_gdn_kernel, tile=MIXER_TILE),
        grid=(batch // MIXER_SEQS, seq_len // MIXER_TILE),
        in_specs=[
            _mixer_specs(CONV_DIM, COL_QKV // CONV_DIM),
            _mixer_specs(MIX_HALF, COL_Z // MIX_HALF),
            _mixer_specs(128, PROMPT_COL_AB // 128),
            state_spec,
            _layer_spec((1, 128), layer),
            _layer_spec((1, 128), layer),
            _layer_spec((1, HEAD_DIM), layer),
        ],
        out_specs=[_mixer_specs(MIX_HALF, 0), state_spec],
        out_shape=[
            jax.ShapeDtypeStruct((batch, seq_len, MIX_HALF), BF16),
            jax.ShapeDtypeStruct((batch, N_HEADS, HEAD_DIM, HEAD_DIM), F32),
        ],
        scratch_shapes=[pltpu.VMEM(_STATE_BLOCK, F32)],
        compiler_params=pltpu.CompilerParams(
            dimension_semantics=("arbitrary", "arbitrary"), vmem_limit_bytes=VMEM_LIMIT),
        name="gdn_prompt",
    )(proj, proj, proj, s0, a_log_rows, dt_bias_rows, norm_a)


def _forget_lower_bound(lbl, layer):
    m = jnp.max(lbl, axis=0, keepdims=True)
    e = jnp.exp(lbl - m)
    sm = e / jnp.sum(e, axis=0, keepdims=True)
    cs = sm[0:1, :]
    for i in range(1, layer + 1):
        cs = cs + sm[i:i + 1, :]
    return cs - sm[0:1, :]


def _midpoint_rows(g, s, row_id):
    n, width = g.shape
    if s >= 8:
        parts = [jnp.broadcast_to(g[b + s:b + s + 1, :], (2 * s, width))
                 for b in range(0, n, 2 * s)]
        return jnp.concatenate(parts, axis=0)
    tiles = g.reshape(n // 8, 8, width)
    sub = row_id.reshape(n // 8, 8, width) & 7
    mids = range(s, 8, 2 * s)
    out = jnp.broadcast_to(tiles[:, mids[-1]:mids[-1] + 1, :], tiles.shape)
    for m in reversed(mids[:-1]):
        out = jnp.where(sub < m + s, jnp.broadcast_to(tiles[:, m:m + 1, :], tiles.shape), out)
    return out.reshape(n, width)


def _hgrn_kernel(p_ref, g_ref, nb_ref, s0_ref, mix_ref, sfin_ref, st_scr, *, tile):
    t = pl.program_id(1)
    pairs = [(b, h) for b in range(p_ref.shape[0]) for h in range(N_HEADS)]

    @pl.when(t == 0)
    def _():
        for p in pairs:
            st_scr[p] = s0_ref[p].T

    n_chunks = tile // HGRN_CHUNK
    chains = [(p, c) for p in pairs for c in range(n_chunks)]

    def rows(c):
        return slice(c * HGRN_CHUNK, (c + 1) * HGRN_CHUNK)

    def cols(block, h):
        return slice(block * MIX_HALF + h * HEAD_DIM, block * MIX_HALF + (h + 1) * HEAD_DIM)

    row_id = lax.broadcasted_iota(jnp.int32, (tile, HEAD_DIM), 0)
    ri = lax.broadcasted_iota(jnp.int32, (HGRN_CHUNK, HGRN_CHUNK), 0)
    ci = lax.broadcasted_iota(jnp.int32, (HGRN_CHUNK, HGRN_CHUNK), 1)
    dist = jnp.bitwise_xor(ri, ci)
    lower = ri > ci

    q = {(b, h): p_ref[b, :, cols(0, h)] for b, h in pairs}
    k = {(b, h): p_ref[b, :, cols(1, h)] for b, h in pairs}
    g = {(b, h): g_ref[b, :, cols(0, h)] for b, h in pairs}

    chunk_row = row_id[:HGRN_CHUNK]
    levels = [HGRN_CHUNK >> (i + 1) for i in range(HGRN_CHUNK.bit_length() - 1)]
    upper = {s: (chunk_row & s) != 0 for s in levels}
    level_mask = {s: lower & (dist >= s) & (dist < 2 * s) for s in levels}
    a = {}
    for p, c in chains:
        qc, kc, gc = q[p][rows(c)], k[p][rows(c)], g[p][rows(c)]
        acc = jnp.where(ri == ci, _mm_nt(qc, kc), 0.0)
        for s in levels:
            d = gc - _midpoint_rows(gc, s, chunk_row)
            xc = jnp.where(upper[s], qc, kc) * jnp.exp(-jnp.abs(d))
            acc = jnp.where(level_mask[s], _mm_nt(xc, xc), acc)
        a[p, c] = acc

    st = {p: st_scr[p] for p in pairs}
    for c in range(n_chunks):
        rs = rows(c)
        for b, h in pairs:
            p = (b, h)
            gc = g[p][rs]
            vc = p_ref[b, rs, cols(2, h)]
            o = _mm(a[p, c], vc) + _mm_nt(q[p][rs] * jnp.exp(gc), st[p])
            g_last = gc[HGRN_CHUNK - 1:HGRN_CHUNK, :]
            st[p] = st[p] * jnp.exp(g_last) + _mm_tn(vc, k[p][rs] * jnp.exp(g_last - gc))
            gate = _sigmoid(p_ref[b, rs, cols(3, h)])
            mix_ref[b, rs, cols(0, h)] = (_rms(o, nb_ref[...]) * gate).astype(mix_ref.dtype)
    for p in pairs:
        st_scr[p] = st[p]

    @pl.when(t == pl.num_programs(1) - 1)
    def _():
        for p in pairs:
            sfin_ref[p] = st_scr[p].T


def _hgrn_call(proj, norm_b, s0, *, layer):
    batch, seq_len, _ = proj.shape
    state_spec = pl.BlockSpec(_STATE_BLOCK, lambda b, t: (b, 0, 0, 0))
    return pl.pallas_call(
        functools.partial(_hgrn_kernel, tile=MIXER_TILE),
        grid=(batch // MIXER_SEQS, seq_len // MIXER_TILE),
        in_specs=[
            _mixer_specs(4 * MIX_HALF, COL_B // (4 * MIX_HALF)),
            _mixer_specs(MIX_HALF, PROMPT_COL_G // MIX_HALF),
            _layer_spec((1, HEAD_DIM), layer),
            state_spec,
        ],
        out_specs=[_mixer_specs(MIX_HALF, 0), state_spec],
        out_shape=[
            jax.ShapeDtypeStruct((batch, seq_len, MIX_HALF), BF16),
            jax.ShapeDtypeStruct((batch, N_HEADS, HEAD_DIM, HEAD_DIM), F32),
        ],
        scratch_shapes=[pltpu.VMEM(_STATE_BLOCK, F32)],
        compiler_params=pltpu.CompilerParams(
            dimension_semantics=("arbitrary", "arbitrary"), vmem_limit_bytes=VMEM_LIMIT),
        name="hgrn_prompt",
    )(proj, proj, norm_b, s0)


def _decode_kernel(*refs, bt, layer):
    (qkv_ref, z_ref, ab_ref, pb_ref, conv_ref, sd_ref, sh_ref,
     cw_ref, alog_ref, dtb_ref, na_ref, nb_ref, lbl_ref) = refs[:13]
    mix_ref, convo_ref, sdo_ref, sho_ref, oa_scr, ob_scr = refs[-6:]
    if layer > 0:
        prev_sd_ref, prev_sh_ref = refs[13:15]
        sdo_ref[0:layer] = prev_sd_ref[...]
        sho_ref[0:layer] = prev_sh_ref[...]
    cw = cw_ref[...]
    u = qkv_ref[...]
    b0 = conv_ref[:, 0:CONV_DIM]
    b1 = conv_ref[:, CONV_DIM:2 * CONV_DIM]
    b2 = conv_ref[:, 2 * CONV_DIM:3 * CONV_DIM]
    acc = b0 * cw[0:1, :]
    acc = acc + b1 * cw[1:2, :]
    acc = acc + b2 * cw[2:3, :]
    acc = acc + u * cw[3:4, :]
    conv = _silu(acc)
    convo_ref[:, 0:CONV_DIM] = b1
    convo_ref[:, CONV_DIM:2 * CONV_DIM] = b2
    convo_ref[:, 2 * CONV_DIM:3 * CONV_DIM] = u

    ab = ab_ref[...]
    eg_all = jnp.exp(-jnp.exp(alog_ref[...]) * _softplus(ab + dtb_ref[...]))
    beta_all = _sigmoid(ab)

    lb = _forget_lower_bound(lbl_ref[...], layer)
    f = lb + (1.0 - lb) * _sigmoid(pb_ref[:, MIX_HALF:2 * MIX_HALF])
    qb = _silu(pb_ref[:, 0:MIX_HALF])

    k_rows, q_rows, v_rows = [], [], []
    for h in range(N_HEADS):
        cs = slice(h * HEAD_DIM, (h + 1) * HEAD_DIM)
        qh = conv[:, cs]
        kh = conv[:, MIX_HALF + h * HEAD_DIM:MIX_HALF + (h + 1) * HEAD_DIM]
        q_rows.append(qh * lax.rsqrt(jnp.sum(qh * qh, axis=-1, keepdims=True) + EPS)
                      * (HEAD_DIM ** -0.5))
        k_rows.append(kh * lax.rsqrt(jnp.sum(kh * kh, axis=-1, keepdims=True) + EPS))
        v_rows.append(conv[:, 2 * MIX_HALF + h * HEAD_DIM:2 * MIX_HALF + (h + 1) * HEAD_DIM])
    f_rows = [f[:, h * HEAD_DIM:(h + 1) * HEAD_DIM] for h in range(N_HEADS)]
    qb_rows = [qb[:, h * HEAD_DIM:(h + 1) * HEAD_DIM] for h in range(N_HEADS)]
    qk_a = [jnp.sum(q_rows[h] * k_rows[h], axis=-1, keepdims=True) for h in range(N_HEADS)]
    qk_b = [jnp.sum(qb_rows[h] * (1.0 - f_rows[h]), axis=-1, keepdims=True) for h in range(N_HEADS)]

    f_cols = jnp.concatenate(f_rows + [jnp.zeros(((16 - N_HEADS) * bt, HEAD_DIM), F32)], axis=0).T

    row = lax.broadcasted_iota(jnp.int32, (8, HEAD_DIM), 0)

    def rows8(first, second=None):
        out = jnp.where(row == 0, first, 0.0)
        return out if second is None else jnp.where(row == 1, second, out)

    for b in range(bt):
        bs = slice(b, b + 1)
        for h in range(N_HEADS):
            cs = slice(h * HEAD_DIM, (h + 1) * HEAD_DIM)
            s = sd_ref[b, h]
            kq = rows8(k_rows[h][bs, :], q_rows[h][bs, :])
            eg = eg_all[bs, h:h + 1]
            beta = beta_all[bs, N_HEADS + h:N_HEADS + h + 1]
            r = _mm(kq, s)
            v_new = beta * (v_rows[h][bs, :] - eg * r[0:1, :])
            sdo_ref[layer, b, h] = s * eg + _mm_tn(kq, rows8(v_new))
            oa_scr[bs, cs] = eg * r[1:2, :] + qk_a[h][bs, :] * v_new
            s = sh_ref[b, h]
            fc = f_cols[:, h * bt + b:h * bt + b + 1]
            vb = pb_ref[bs, 2 * MIX_HALF + h * HEAD_DIM:2 * MIX_HALF + (h + 1) * HEAD_DIM]
            r = _mm(rows8(qb_rows[h][bs, :] * f_rows[h][bs, :]), s)
            sho_ref[layer, b, h] = s * fc + _mm_tn(rows8(1.0 - f_rows[h][bs, :]), rows8(vb))
            ob_scr[bs, cs] = r[0:1, :] + qk_b[h][bs, :] * vb

    for h in range(N_HEADS):
        cs = slice(h * HEAD_DIM, (h + 1) * HEAD_DIM)
        mix_ref[:, cs] = (_rms(oa_scr[:, cs], na_ref[...]) * _silu(z_ref[:, cs])).astype(mix_ref.dtype)
        gate = _sigmoid(pb_ref[:, 3 * MIX_HALF + h * HEAD_DIM:3 * MIX_HALF + (h + 1) * HEAD_DIM])
        mix_ref[:, MIX_HALF + h * HEAD_DIM:MIX_HALF + (h + 1) * HEAD_DIM] = (
            _rms(ob_scr[:, cs], nb_ref[...]) * gate).astype(mix_ref.dtype)


def _decode_call(proj, conv_state, s_delta, s_hgrn, prev_states, conv_w, a_log_rows, dt_bias_rows,
                 norm_a, norm_b, lb_logits, *, layer):
    batch = proj.shape[0]
    bt = 8
    kern = functools.partial(_decode_kernel, bt=bt, layer=layer)
    state_dims = (bt, N_HEADS, HEAD_DIM, HEAD_DIM)
    old_state_spec = pl.BlockSpec((None,) + state_dims, lambda i: (layer, i, 0, 0, 0))
    prev_state_spec = pl.BlockSpec((layer,) + state_dims, lambda i: (0, i, 0, 0, 0))
    new_state_spec = pl.BlockSpec((layer + 1,) + state_dims, lambda i: (0, i, 0, 0, 0))
    new_state_shape = jax.ShapeDtypeStruct((layer + 1, batch, N_HEADS, HEAD_DIM, HEAD_DIM), F32)
    return pl.pallas_call(
        kern,
        grid=(batch // bt,),
        in_specs=[
            pl.BlockSpec((bt, CONV_DIM), lambda i: (i, COL_QKV // CONV_DIM)),
            pl.BlockSpec((bt, MIX_HALF), lambda i: (i, COL_Z // MIX_HALF)),
            pl.BlockSpec((bt, 128), lambda i: (i, COL_AB // 128)),
            pl.BlockSpec((bt, 4 * MIX_HALF), lambda i: (i, COL_B // (4 * MIX_HALF))),
            pl.BlockSpec((None, bt, (CONV_W - 1) * CONV_DIM), lambda i: (layer, i, 0)),
            old_state_spec,
            old_state_spec,
            _layer_spec((CONV_W, CONV_DIM), layer),
            _layer_spec((1, 128), layer),
            _layer_spec((1, 128), layer),
            _layer_spec((1, HEAD_DIM), layer),
            _layer_spec((1, HEAD_DIM), layer),
            pl.BlockSpec((DEPTH, MIX_HALF), lambda i: (0, 0)),
        ] + [prev_state_spec] * len(prev_states),
        out_specs=[
            pl.BlockSpec((bt, 2 * MIX_HALF), lambda i: (i, 0)),
            pl.BlockSpec((bt, (CONV_W - 1) * CONV_DIM), lambda i: (i, 0)),
            new_state_spec,
            new_state_spec,
        ],
        out_shape=[
            jax.ShapeDtypeStruct((batch, 2 * MIX_HALF), BF16),
            jax.ShapeDtypeStruct((batch, (CONV_W - 1) * CONV_DIM), F32),
            new_state_shape,
            new_state_shape,
        ],
        scratch_shapes=[
            pltpu.VMEM((bt, MIX_HALF), F32),
            pltpu.VMEM((bt, MIX_HALF), F32),
        ],
        compiler_params=pltpu.CompilerParams(
            dimension_semantics=("arbitrary",), vmem_limit_bytes=VMEM_LIMIT),
        name="mixers_decode",
    )(proj, proj, proj, proj, conv_state, s_delta, s_hgrn, conv_w, a_log_rows, dt_bias_rows,
      norm_a, norm_b, lb_logits, *prev_states)


FF_CHUNK = 256


def _out_ffn_kernel(x_ref, ma_ref, mb_ref, g1_ref, sh2_ref, sc2_ref, g2_ref, wo_ref, n2_ref,
                    wg_ref, wu_ref, wd_ref, nf_ref, o_ref, acc_ref, *, final_norm, tiles_per_seq):
    gate1 = _mod_rows(g1_ref, tiles_per_seq)
    shift2 = _mod_rows(sh2_ref, tiles_per_seq)
    scale2 = _mod_rows(sc2_ref, tiles_per_seq)
    gate2 = _mod_rows(g2_ref, tiles_per_seq)
    mix = _mm(ma_ref[...], wo_ref[0:MIX_HALF, :]) + _mm(mb_ref[...], wo_ref[MIX_HALF:, :])
    x1 = x_ref[...] + gate1 * mix
    h = (_rms(x1, n2_ref[...]) * (1.0 + scale2) + shift2).astype(BF16)
    for c in range(D_FF // FF_CHUNK):
        fs = slice(c * FF_CHUNK, (c + 1) * FF_CHUNK)
        gate = jnp.dot(h, wg_ref[:, fs], preferred_element_type=F32)
        up = jnp.dot(h, wu_ref[:, fs], preferred_element_type=F32)
        part = _mm(_silu(gate) * up, wd_ref[fs, :])
        if c == 0:
            acc_ref[...] = part
        else:
            acc_ref[...] += part
    x2 = x1 + gate2 * acc_ref[...]
    if final_norm:
        x2 = _rms(x2, nf_ref[...])
    o_ref[...] = x2


def _out_ffn_call(x, mix_a, mix_b, mix_b_col, mods, w_out, norm2, w_gate, w_up, w_down, norm_f,
                  *, layer, n_prompt, per_token, seq_len, final_norm):
    rows = x.shape[0]
    tm = rows if per_token else 1024
    tiles_per_seq = None if per_token else seq_len // tm
    resident = functools.partial(_layer_spec, layer=layer, pipeline_mode=pl.Buffered(1))
    mod = functools.partial(_mod_spec, mods, layer, n_prompt, per_token)
    kern = functools.partial(_out_ffn_kernel, final_norm=final_norm, tiles_per_seq=tiles_per_seq)
    return pl.pallas_call(
        kern,
        grid=(rows // tm,),
        in_specs=[
            pl.BlockSpec((tm, D_MODEL), lambda i: (i, 0)),
            pl.BlockSpec((tm, MIX_HALF), lambda i: (i, 0)),
            pl.BlockSpec((tm, MIX_HALF), lambda i: (i, mix_b_col)),
            mod(MOD_GATE1),
            mod(MOD_SHIFT2),
            mod(MOD_SCALE2),
            mod(MOD_GATE2),
            resident((D_MODEL, D_MODEL)),
            _layer_spec((1, D_MODEL), layer),
            resident((D_MODEL, D_FF)),
            resident((D_MODEL, D_FF)),
            resident((D_FF, D_MODEL)),
            pl.BlockSpec((1, D_MODEL), lambda i: (0, 0)),
        ],
        out_specs=pl.BlockSpec((tm, D_MODEL), lambda i: (i, 0)),
        out_shape=jax.ShapeDtypeStruct((rows, D_MODEL), F32),
        scratch_shapes=[pltpu.VMEM((tm, D_MODEL), F32)],
        compiler_params=pltpu.CompilerParams(
            dimension_semantics=("arbitrary",), vmem_limit_bytes=VMEM_LIMIT),
        name="out_ffn",
    )(x, mix_a, mix_b, mods, mods, mods, mods, w_out, norm2, w_gate, w_up, w_down, norm_f)


N_AB = 2 * N_HEADS
REORDER_COLS = 512
AB_BLOCK = COL_AB // REORDER_COLS


def _reorder_in_proj_kernel(wt_ref, o_ref):
    w = wt_ref[0]
    row = lax.broadcasted_iota(jnp.int32, w.shape, 0)
    w = jnp.where((pl.program_id(1) == AB_BLOCK) & (row >= N_AB), 0.0, w)
    o_ref[...] = w.T.astype(BF16)


def _reorder_in_proj(w):
    wt = jnp.swapaxes(w, 1, 2)

    def src_row(l, j):
        is_ab = j // AB_BLOCK
        is_b = j // (COL_B // REORDER_COLS) - 2 * is_ab
        start = j * REORDER_COLS + N_AB * is_b - (COL_AB - COL_B) * is_ab
        return (l, pl.multiple_of(start, 8), 0)

    return pl.pallas_call(
        _reorder_in_proj_kernel,
        grid=(DEPTH, pl.cdiv(IN_COLS, REORDER_COLS)),
        in_specs=[pl.BlockSpec((pl.Element(1), pl.Element(REORDER_COLS), pl.Element(D_MODEL)),
                               src_row)],
        out_specs=pl.BlockSpec((None, D_MODEL, REORDER_COLS), lambda l, j: (l, 0, j)),
        out_shape=jax.ShapeDtypeStruct((DEPTH, D_MODEL, IN_COLS), BF16),
        compiler_params=pltpu.CompilerParams(
            dimension_semantics=("arbitrary", "arbitrary"), vmem_limit_bytes=VMEM_LIMIT),
        name="reorder_in_proj",
    )(wt)


def _lane_rows(v):
    return jnp.pad(v.astype(F32), ((0, 0), (0, 128 - v.shape[1])))[:, None, :]


def kernel(x_prompt, x_sample, c_prompt, c_sample, state_delta, state_conv, state_hgrn, w_ada, b_ada,
           norm1, w_in, conv_w, a_log, dt_bias, norm_a, norm_b, lb_logits, w_out, norm2, w_gate, w_up,
           w_down, norm_f):
    bp, seq_len, _ = x_prompt.shape
    bs = x_sample.shape[0]

    mods = _ada_call(jnp.concatenate([c_sample, c_prompt], axis=0), w_ada, b_ada)

    xp = x_prompt.reshape(bp * seq_len, D_MODEL)
    xs = x_sample.reshape(bs, D_MODEL)
    zero_buf = jnp.zeros((bp, 8, CONV_DIM), F32)
    zero_state = jnp.zeros((bp, N_HEADS, HEAD_DIM, HEAD_DIM), F32)

    w_in_b = _reorder_in_proj(w_in)
    w_out_b = w_out.astype(BF16)
    w_gate_b = w_gate.astype(BF16)
    w_up_b = w_up.astype(BF16)
    w_down_b = w_down.astype(BF16)
    a_log_rows = _lane_rows(a_log)
    dt_bias_rows = _lane_rows(dt_bias)
    norm1_r = norm1[:, None, :]
    norm2_r = norm2[:, None, :]
    norm_a_r = norm_a[:, None, :]
    norm_b_r = norm_b[:, None, :]
    norm_f_r = norm_f[None, :]
    conv_state = state_conv.reshape(DEPTH, bs, (CONV_W - 1) * CONV_DIM)

    delta_p, conv_p, hgrn_p, conv_s = [], [], [], []
    decode_states = ()
    for l in range(DEPTH):
        last = l == DEPTH - 1
        dense = dict(layer=l, n_prompt=bp)

        proj, cv = _in_proj_call(xp, mods, norm1_r, w_in_b, zero_buf, conv_w, lb_logits,
                                 per_token=False, seq_len=seq_len, **dense)
        proj = proj.reshape(bp, seq_len, PROMPT_COLS)
        mix_a, s_a = _gdn_call(proj, zero_state, a_log_rows, dt_bias_rows, norm_a_r, layer=l)
        mix_b, s_b = _hgrn_call(proj, norm_b_r, zero_state, layer=l)
        mix_a = mix_a.reshape(bp * seq_len, MIX_HALF)
        mix_b = mix_b.reshape(bp * seq_len, MIX_HALF)
        xp = _out_ffn_call(xp, mix_a, mix_b, 0, mods, w_out_b, norm2_r, w_gate_b, w_up_b, w_down_b,
                           norm_f_r, per_token=False, seq_len=seq_len, final_norm=last, **dense)
        delta_p.append(s_a)
        conv_p.append(cv)
        hgrn_p.append(s_b)

        proj = _in_proj_call(xs, mods, norm1_r, w_in_b, None, None, None,
                             per_token=True, seq_len=1, **dense)
        mix, cv, *decode_states = _decode_call(
            proj, conv_state, state_delta, state_hgrn, decode_states,
            conv_w, a_log_rows, dt_bias_rows, norm_a_r, norm_b_r, lb_logits, layer=l)
        xs = _out_ffn_call(xs, mix, mix, 1, mods, w_out_b, norm2_r, w_gate_b, w_up_b, w_down_b,
                           norm_f_r, per_token=True, seq_len=1, final_norm=last, **dense)
        conv_s.append(cv.reshape(bs, CONV_W - 1, CONV_DIM))

    delta_s, hgrn_s = decode_states
    return (xp.reshape(bp, seq_len, D_MODEL), xs.reshape(bs, 1, D_MODEL),
            jnp.stack(delta_p), jnp.stack(conv_p), jnp.stack(hgrn_p),
            delta_s, jnp.stack(conv_s), hgrn_s)
```

```python
import functools

import jax
import jax.numpy as jnp
from jax import lax
from jax.experimental import pallas as pl
from jax.experimental.pallas import tpu as pltpu

F32 = jnp.float32
BF16 = jnp.bfloat16

D_MODEL = 1024
DEPTH = 2
N_HEADS = 4
HEAD_DIM = 128
MIX_HALF = N_HEADS * HEAD_DIM
CONV_W = 4
CONV_DIM = 3 * MIX_HALF
D_FF = 2816
EPS = 1e-6
GDN_CHUNK = 128
GDN_SUB = 16
HGRN_CHUNK = 64

COL_QKV = 0
COL_Z = CONV_DIM
COL_B = COL_Z + MIX_HALF
COL_AB = COL_B + 4 * MIX_HALF
IN_COLS = COL_AB + 128
PROMPT_COL_G = COL_AB
PROMPT_COL_AB = PROMPT_COL_G + MIX_HALF
PROMPT_COLS = PROMPT_COL_AB + 128

MOD_SHIFT1, MOD_SCALE1, MOD_GATE1, MOD_SHIFT2, MOD_SCALE2, MOD_GATE2 = range(6)

VMEM_LIMIT = 56 * 1024 * 1024


def _sigmoid(x):
    return 1.0 / (1.0 + jnp.exp(-x))


def _silu(x):
    return x * _sigmoid(x)


def _softplus(x):
    return jnp.maximum(x, 0.0) + jnp.log1p(jnp.exp(-jnp.abs(x)))


def _mm(a, b):
    return jnp.dot(a.astype(BF16), b.astype(BF16), preferred_element_type=F32)


def _mm_nt(a, b):
    return lax.dot_general(a.astype(BF16), b.astype(BF16), (((1,), (1,)), ((), ())),
                           preferred_element_type=F32)


def _mm_tn(a, b):
    return lax.dot_general(a.astype(BF16), b.astype(BF16), (((0,), (0,)), ((), ())),
                           preferred_element_type=F32)


def _rms(x, w):
    return x * lax.rsqrt(jnp.mean(x * x, axis=-1, keepdims=True) + EPS) * w


def _chunk_cumsum(x, chunk):
    rows = lax.broadcasted_iota(jnp.int32, x.shape, 0) & (chunk - 1)
    s = 1
    while s < chunk:
        x = x + jnp.where(rows >= s, pltpu.roll(x, s, 0), 0.0)
        s *= 2
    return x


def _ada_kernel(c_ref, w_ref, b_ref, o_ref):
    a = _silu(c_ref[...])
    o_ref[...] = _mm(a, w_ref[...]) + b_ref[...]


def _ada_call(c_all, w_ada, b_ada):
    rows = c_all.shape[0]
    tn = 1024
    return pl.pallas_call(
        _ada_kernel,
        grid=(DEPTH, 6 * D_MODEL // tn),
        in_specs=[
            pl.BlockSpec((rows, D_MODEL), lambda l, j: (0, 0)),
            pl.BlockSpec((None, D_MODEL, tn), lambda l, j: (l, 0, j)),
            pl.BlockSpec((None, 1, tn), lambda l, j: (l, 0, j)),
        ],
        out_specs=pl.BlockSpec((None, rows, tn), lambda l, j: (l, 0, j)),
        out_shape=jax.ShapeDtypeStruct((DEPTH, rows, 6 * D_MODEL), F32),
        compiler_params=pltpu.CompilerParams(
            dimension_semantics=("arbitrary", "arbitrary"), vmem_limit_bytes=VMEM_LIMIT),
        name="ada_mod",
    )(c_all, w_ada, b_ada.reshape(DEPTH, 1, 6 * D_MODEL))


def _mod_spec(mods, layer, n_prompt, per_token, col):
    n_decode = mods.shape[1] - n_prompt
    if per_token:
        return pl.BlockSpec((None, n_decode, D_MODEL), lambda i: (layer, 0, col))
    return pl.BlockSpec((None, n_prompt, D_MODEL), lambda i: (layer, n_decode // n_prompt, col))


def _mod_rows(ref, tiles_per_seq):
    if tiles_per_seq is None:
        return ref[...]
    return ref[pl.ds(pl.program_id(0) // tiles_per_seq, 1), :]


def _layer_spec(shape, layer, **kwargs):
    zeros = (0,) * len(shape)
    return pl.BlockSpec((None,) + tuple(shape), lambda *_: (layer,) + zeros, **kwargs)


def _modulated_input(x_ref, scale_ref, shift_ref, nw_ref, tile, tiles_per_seq):
    if tiles_per_seq is None:
        scale, shift = scale_ref[...], shift_ref[...]
    else:
        seq = tile // tiles_per_seq
        scale, shift = scale_ref[pl.ds(seq, 1), :], shift_ref[pl.ds(seq, 1), :]
    return (_rms(x_ref[...], nw_ref[...]) * (1.0 + scale) + shift).astype(BF16)


def _in_proj_decode_kernel(x_ref, scale_ref, shift_ref, nw_ref, w_ref, o_ref):
    h = _modulated_input(x_ref, scale_ref, shift_ref, nw_ref, 0, None)
    o_ref[...] = jnp.dot(h, w_ref[...], preferred_element_type=F32)


def _in_proj_prompt_kernel(x_ref, scale_ref, shift_ref, nw_ref, w_ref, buf_ref, cw_ref, lbl_ref,
                           o_ref, cfin_ref, raw, h_scr, *, tiles_per_seq, layer):
    i = pl.program_id(0)
    tm = x_ref.shape[0]
    tile_in_seq = i % tiles_per_seq
    body = slice(8, 8 + tm)

    n_slabs = CONV_DIM // HEAD_DIM

    def slab_cols(ct):
        return slice(ct * HEAD_DIM, (ct + 1) * HEAD_DIM)

    @pl.when(tile_in_seq == 0)
    def _():
        for ct in range(n_slabs):
            raw[ct, 0:8, :] = buf_ref[:, slab_cols(ct)]

    h_scr[...] = _modulated_input(x_ref, scale_ref, shift_ref, nw_ref, i, tiles_per_seq)
    lb_all = _forget_lower_bound(lbl_ref[...], layer)

    def project(cs):
        return jnp.dot(h_scr[...], w_ref[:, cs], preferred_element_type=F32)

    def conv_slabs(cs):
        pre = project(cs)
        out = []
        for ct in range(cs.start // HEAD_DIM, cs.stop // HEAD_DIM):
            cols = slab_cols(ct)
            raw[ct, body, :] = pre[:, cols.start - cs.start:cols.stop - cs.start]
            acc = raw[ct, 5:5 + tm, :] * cw_ref[0:1, cols]
            for j in range(1, CONV_W):
                acc = acc + raw[ct, 5 + j:5 + j + tm, :] * cw_ref[j:j + 1, cols]
            out.append((cols, _silu(acc)))
        return out

    def l2_normalised(cs, scale):
        for cols, v in conv_slabs(cs):
            o_ref[:, cols] = v * lax.rsqrt(jnp.sum(v * v, axis=-1, keepdims=True) + EPS) * scale

    def post_q(cs):
        l2_normalised(cs, HEAD_DIM ** -0.5)

    def post_k(cs):
        l2_normalised(cs, 1.0)

    def post_v(cs):
        for cols, v in conv_slabs(cs):
            o_ref[:, cols] = v

    def post_copy(cs):
        o_ref[:, cs] = project(cs)

    def post_silu(cs):
        o_ref[:, cs] = _silu(project(cs))

    def post_forget(cs):
        gate_cols = slice(cs.start - COL_B - MIX_HALF, cs.stop - COL_B - MIX_HALF)
        lb = lb_all[:, gate_cols]
        f = lb + (1.0 - lb) * _sigmoid(project(cs))
        o_ref[:, cs] = 1.0 - f
        o_ref[:, PROMPT_COL_G + gate_cols.start:PROMPT_COL_G + gate_cols.stop] = (
            _chunk_cumsum(jnp.log(f), HGRN_CHUNK))

    width = 2 * HEAD_DIM
    pieces = ([post_q] * 2 + [post_k] * 2 + [post_v] * 2 + [post_copy] * 2
              + [post_silu] * 2 + [post_forget] * 2 + [post_copy] * 4)
    order = [0, 6, 1, 7, 2, 12, 3, 13, 4, 14, 5, 15, 8, 9, 10, 11]
    for piece in order:
        pieces[piece](slice(piece * width, (piece + 1) * width))
    o_ref[:, PROMPT_COL_AB:PROMPT_COLS] = project(slice(COL_AB, IN_COLS))

    for ct in range(n_slabs):
        raw[ct, 0:8, :] = raw[ct, tm:tm + 8, :]

    @pl.when(tile_in_seq == tiles_per_seq - 1)
    def _():
        for ct in range(n_slabs):
            cfin_ref[:, slab_cols(ct)] = raw[ct, 8 - (CONV_W - 1):8, :]


def _in_proj_call(x, mods, norm1, w_in, conv_buf, conv_w, lb_logits, *, layer, n_prompt, per_token,
                  seq_len):
    rows = x.shape[0]
    in_specs = [
        None,
        _mod_spec(mods, layer, n_prompt, per_token, MOD_SCALE1),
        _mod_spec(mods, layer, n_prompt, per_token, MOD_SHIFT1),
        _layer_spec((1, D_MODEL), layer),
        _layer_spec((D_MODEL, IN_COLS), layer, pipeline_mode=pl.Buffered(1)),
    ]
    params = pltpu.CompilerParams(dimension_semantics=("arbitrary",), vmem_limit_bytes=VMEM_LIMIT)
    if per_token:
        in_specs[0] = pl.BlockSpec((rows, D_MODEL), lambda i: (0, 0))
        return pl.pallas_call(
            _in_proj_decode_kernel,
            grid=(1,),
            in_specs=in_specs,
            out_specs=pl.BlockSpec((rows, IN_COLS), lambda i: (0, 0)),
            out_shape=jax.ShapeDtypeStruct((rows, IN_COLS), F32),
            compiler_params=params,
            name="in_proj_decode",
        )(x, mods, mods, norm1, w_in)

    tm = 256
    tiles_per_seq = seq_len // tm
    in_specs[0] = pl.BlockSpec((tm, D_MODEL), lambda i: (i, 0))
    in_specs += [
        pl.BlockSpec((None, 8, CONV_DIM), lambda i: (i // tiles_per_seq, 0, 0)),
        _layer_spec((CONV_W, CONV_DIM), layer),
        pl.BlockSpec((DEPTH, MIX_HALF), lambda i: (0, 0)),
    ]
    kern = functools.partial(_in_proj_prompt_kernel, tiles_per_seq=tiles_per_seq, layer=layer)
    return pl.pallas_call(
        kern,
        grid=(rows // tm,),
        in_specs=in_specs,
        out_specs=[
            pl.BlockSpec((tm, PROMPT_COLS), lambda i: (i, 0)),
            pl.BlockSpec((None, CONV_W - 1, CONV_DIM), lambda i: (i // tiles_per_seq, 0, 0)),
        ],
        out_shape=[
            jax.ShapeDtypeStruct((rows, PROMPT_COLS), F32),
            jax.ShapeDtypeStruct((n_prompt, CONV_W - 1, CONV_DIM), F32),
        ],
        scratch_shapes=[
            pltpu.VMEM((CONV_DIM // HEAD_DIM, tm + 8, HEAD_DIM), F32),
            pltpu.VMEM((tm, D_MODEL), BF16),
        ],
        compiler_params=params,
        name="in_proj_prompt",
    )(x, mods, mods, norm1, w_in, conv_buf, conv_w, lb_logits)


def _unit_lower_inverse(mats, sub_mask, eye):
    d = [jnp.where(sub_mask, a, 0.0) for a in mats]
    n = [a - x for a, x in zip(mats, d)]
    d2 = [_mm(x, x) for x in d]
    d4 = [_mm(x, x) for x in d2]
    d8 = [_mm(x, x) for x in d4]
    p = [eye - x + x2 - _mm(x, x2) for x, x2 in zip(d, d2)]
    p = [x + _mm(x, y) for x, y in zip(p, d4)]
    dinv = [x + _mm(x, y) for x, y in zip(p, d8)]
    e = [_mm(x, y) for x, y in zip(dinv, n)]
    e2 = [_mm(x, x) for x in e]
    e4 = [_mm(x, x) for x in e2]
    f = [x2 - x - _mm(x, x2) for x, x2 in zip(e, e2)]
    q = [x + y + _mm(x, y) for x, y in zip(f, e4)]
    return [x + _mm(y, x) for x, y in zip(dinv, q)]


def _gdn_kernel(qkv_ref, z_ref, ab_ref, s0_ref, alog_ref, dtb_ref, na_ref,
                mix_ref, sfin_ref, s_scr, *, tile):
    t = pl.program_id(1)

    @pl.when(t == 0)
    def _():
        s_scr[...] = s0_ref[...]

    ri = lax.broadcasted_iota(jnp.int32, (GDN_CHUNK, GDN_CHUNK), 0)
    ci = lax.broadcasted_iota(jnp.int32, (GDN_CHUNK, GDN_CHUNK), 1)
    m_incl = ri >= ci
    m_strict = ri > ci
    sub_mask = jnp.bitwise_xor(ri, ci) < GDN_SUB
    eye = jnp.where(ri == ci, 1.0, 0.0).astype(F32)
    n_chunks = tile // GDN_CHUNK
    pairs = [(b, h) for b in range(qkv_ref.shape[0]) for h in range(N_HEADS)]
    chains = [(p, c) for p in pairs for c in range(n_chunks)]

    def rows(c):
        return slice(c * GDN_CHUNK, (c + 1) * GDN_CHUNK)

    g_cum_t = {}
    qn, kn, kb, gc, eg, rhs = {}, {}, {}, {}, {}, {}
    for b in range(qkv_ref.shape[0]):
        ab = ab_ref[b]
        g_all = -jnp.exp(alog_ref[...]) * _softplus(ab + dtb_ref[...])
        sig_all = _sigmoid(ab)
        g_cum = _chunk_cumsum(g_all, GDN_CHUNK)
        g_cum_t[b] = g_cum.T
        for h in range(N_HEADS):
            p = (b, h)
            qn[p] = qkv_ref[b, :, h * HEAD_DIM:(h + 1) * HEAD_DIM]
            kn[p] = qkv_ref[b, :, MIX_HALF + h * HEAD_DIM:MIX_HALF + (h + 1) * HEAD_DIM]
            vh = qkv_ref[b, :, 2 * MIX_HALF + h * HEAD_DIM:2 * MIX_HALF + (h + 1) * HEAD_DIM]
            beta = sig_all[:, N_HEADS + h:N_HEADS + h + 1]
            gc[p] = g_cum[:, h:h + 1]
            eg[p] = jnp.exp(gc[p])
            kb[p] = kn[p] * beta
            rhs[p] = jnp.concatenate([vh * beta, kb[p] * eg[p]], axis=1)

    dec = {}
    for p, c in chains:
        gr = g_cum_t[p[0]][p[1]:p[1] + 1, rows(c)]
        dec[p, c] = jnp.exp(jnp.where(m_incl, gc[p][rows(c)] - gr, -jnp.inf))
    st = {(p, c): _mm_nt(jnp.concatenate([kb[p][rows(c)], qn[p][rows(c)]], axis=0), kn[p][rows(c)])
          for p, c in chains}
    a = [st[pc][:GDN_CHUNK] * jnp.where(m_strict, dec[pc], 0.0) for pc in chains]
    qk = {pc: st[pc][GDN_CHUNK:] * dec[pc] for pc in chains}
    tinv = dict(zip(chains, _unit_lower_inverse(a, sub_mask, eye)))
    uw = {(p, c): _mm(tinv[p, c], rhs[p][rows(c)]) for p, c in chains}

    s = {p: s_scr[p] for p in pairs}
    for c in range(n_chunks):
        rs = rows(c)
        r = {p: _mm(jnp.concatenate([uw[p, c][:, HEAD_DIM:], qn[p][rs] * eg[p][rs]], axis=0), s[p])
             for p in pairs}
        v_new = {p: uw[p, c][:, :HEAD_DIM] - r[p][:GDN_CHUNK] for p in pairs}
        for p in pairs:
            g_last = gc[p][(c + 1) * GDN_CHUNK - 1:(c + 1) * GDN_CHUNK, :]
            kd = kn[p][rs] * jnp.exp(g_last - gc[p][rs])
            s[p] = s[p] * jnp.exp(g_last) + _mm_tn(kd, v_new[p])
        for b, h in pairs:
            cs = slice(h * HEAD_DIM, (h + 1) * HEAD_DIM)
            o = r[b, h][GDN_CHUNK:] + _mm(qk[(b, h), c], v_new[b, h])
            mix_ref[b, rs, cs] = (_rms(o, na_ref[...]) * _silu(z_ref[b, rs, cs])).astype(mix_ref.dtype)
    for p in pairs:
        s_scr[p] = s[p]

    @pl.when(t == pl.num_programs(1) - 1)
    def _():
        sfin_ref[...] = s_scr[...]


MIXER_SEQS = 2
MIXER_TILE = 256


def _mixer_specs(cols, col_block):
    return pl.BlockSpec((MIXER_SEQS, MIXER_TILE, cols), lambda b, t: (b, t, col_block))


_STATE_BLOCK = (MIXER_SEQS, N_HEADS, HEAD_DIM, HEAD_DIM)


def _gdn_call(proj, s0, a_log_rows, dt_bias_rows, norm_a, *, layer):
    batch, seq_len, _ = proj.shape
    state_spec = pl.BlockSpec(_STATE_BLOCK, lambda b, t: (b, 0, 0, 0))
    return pl.pallas_call(
        functools.partial(_gdn_kernel, tile=MIXER_TILE),
        grid=(batch // MIXER_SEQS, seq_len // MIXER_TILE),
        in_specs=[
            _mixer_specs(CONV_DIM, COL_QKV // CONV_DIM),
            _mixer_specs(MIX_HALF, COL_Z // MIX_HALF),
            _mixer_specs(128, PROMPT_COL_AB // 128),
            state_spec,
            _layer_spec((1, 128), layer),
            _layer_spec((1, 128), layer),
            _layer_spec((1, HEAD_DIM), layer),
        ],
        out_specs=[_mixer_specs(MIX_HALF, 0), state_spec],
        out_shape=[
            jax.ShapeDtypeStruct((batch, seq_len, MIX_HALF), BF16),
            jax.ShapeDtypeStruct((batch, N_HEADS, HEAD_DIM, HEAD_DIM), F32),
        ],
        scratch_shapes=[pltpu.VMEM(_STATE_BLOCK, F32)],
        compiler_params=pltpu.CompilerParams(
            dimension_semantics=("arbitrary", "arbitrary"), vmem_limit_bytes=VMEM_LIMIT),
        name="gdn_prompt",
    )(proj, proj, proj, s0, a_log_rows, dt_bias_rows, norm_a)


def _forget_lower_bound(lbl, layer):
    m = jnp.max(lbl, axis=0, keepdims=True)
    e = jnp.exp(lbl - m)
    sm = e / jnp.sum(e, axis=0, keepdims=True)
    cs = sm[0:1, :]
    for i in range(1, layer + 1):
        cs = cs + sm[i:i + 1, :]
    return cs - sm[0:1, :]


def _midpoint_rows(g, s, row_id):
    n, width = g.shape
    if s >= 8:
        parts = [jnp.broadcast_to(g[b + s:b + s + 1, :], (2 * s, width))
                 for b in range(0, n, 2 * s)]
        return jnp.concatenate(parts, axis=0)
    tiles = g.reshape(n // 8, 8, width)
    sub = row_id.reshape(n // 8, 8, width) & 7
    mids = range(s, 8, 2 * s)
    out = jnp.broadcast_to(tiles[:, mids[-1]:mids[-1] + 1, :], tiles.shape)
    for m in reversed(mids[:-1]):
        out = jnp.where(sub < m + s, jnp.broadcast_to(tiles[:, m:m + 1, :], tiles.shape), out)
    return out.reshape(n, width)


def _hgrn_kernel(p_ref, g_ref, nb_ref, s0_ref, mix_ref, sfin_ref, st_scr, *, tile):
    t = pl.program_id(1)
    pairs = [(b, h) for b in range(p_ref.shape[0]) for h in range(N_HEADS)]

    @pl.when(t == 0)
    def _():
        for p in pairs:
            st_scr[p] = s0_ref[p].T

    n_chunks = tile // HGRN_CHUNK
    chains = [(p, c) for p in pairs for c in range(n_chunks)]

    def rows(c):
        return slice(c * HGRN_CHUNK, (c + 1) * HGRN_CHUNK)

    def cols(block, h):
        return slice(block * MIX_HALF + h * HEAD_DIM, block * MIX_HALF + (h + 1) * HEAD_DIM)

    row_id = lax.broadcasted_iota(jnp.int32, (tile, HEAD_DIM), 0)
    ri = lax.broadcasted_iota(jnp.int32, (HGRN_CHUNK, HGRN_CHUNK), 0)
    ci = lax.broadcasted_iota(jnp.int32, (HGRN_CHUNK, HGRN_CHUNK), 1)
    dist = jnp.bitwise_xor(ri, ci)
    lower = ri > ci

    q = {(b, h): p_ref[b, :, cols(0, h)] for b, h in pairs}
    k = {(b, h): p_ref[b, :, cols(1, h)] for b, h in pairs}
    g = {(b, h): g_ref[b, :, cols(0, h)] for b, h in pairs}

    chunk_row = row_id[:HGRN_CHUNK]
    levels = [HGRN_CHUNK >> (i + 1) for i in range(HGRN_CHUNK.bit_length() - 1)]
    upper = {s: (chunk_row & s) != 0 for s in levels}
    level_mask = {s: lower & (dist >= s) & (dist < 2 * s) for s in levels}
    a = {}
    for p, c in chains:
        qc, kc, gc = q[p][rows(c)], k[p][rows(c)], g[p][rows(c)]
        acc = jnp.where(ri == ci, _mm_nt(qc, kc), 0.0)
        for s in levels:
            d = gc - _midpoint_rows(gc, s, chunk_row)
            xc = jnp.where(upper[s], qc, kc) * jnp.exp(-jnp.abs(d))
            acc = jnp.where(level_mask[s], _mm_nt(xc, xc), acc)
        a[p, c] = acc

    st = {p: st_scr[p] for p in pairs}
    for c in range(n_chunks):
        rs = rows(c)
        for b, h in pairs:
            p = (b, h)
            gc = g[p][rs]
            vc = p_ref[b, rs, cols(2, h)]
            o = _mm(a[p, c], vc) + _mm_nt(q[p][rs] * jnp.exp(gc), st[p])
            g_last = gc[HGRN_CHUNK - 1:HGRN_CHUNK, :]
            st[p] = st[p] * jnp.exp(g_last) + _mm_tn(vc, k[p][rs] * jnp.exp(g_last - gc))
            gate = _sigmoid(p_ref[b, rs, cols(3, h)])
            mix_ref[b, rs, cols(0, h)] = (_rms(o, nb_ref[...]) * gate).astype(mix_ref.dtype)
    for p in pairs:
        st_scr[p] = st[p]

    @pl.when(t == pl.num_programs(1) - 1)
    def _():
        for p in pairs:
            sfin_ref[p] = st_scr[p].T


def _hgrn_call(proj, norm_b, s0, *, layer):
    batch, seq_len, _ = proj.shape
    state_spec = pl.BlockSpec(_STATE_BLOCK, lambda b, t: (b, 0, 0, 0))
    return pl.pallas_call(
        functools.partial(_hgrn_kernel, tile=MIXER_TILE),
        grid=(batch // MIXER_SEQS, seq_len // MIXER_TILE),
        in_specs=[
            _mixer_specs(4 * MIX_HALF, COL_B // (4 * MIX_HALF)),
            _mixer_specs(MIX_HALF, PROMPT_COL_G // MIX_HALF),
            _layer_spec((1, HEAD_DIM), layer),
            state_spec,
        ],
        out_specs=[_mixer_specs(MIX_HALF, 0), state_spec],
        out_shape=[
            jax.ShapeDtypeStruct((batch, seq_len, MIX_HALF), BF16),
            jax.ShapeDtypeStruct((batch, N_HEADS, HEAD_DIM, HEAD_DIM), F32),
        ],
        scratch_shapes=[pltpu.VMEM(_STATE_BLOCK, F32)],
        compiler_params=pltpu.CompilerParams(
            dimension_semantics=("arbitrary", "arbitrary"), vmem_limit_bytes=VMEM_LIMIT),
        name="hgrn_prompt",
    )(proj, proj, norm_b, s0)


def _decode_kernel(*refs, bt, layer):
    (qkv_ref, z_ref, ab_ref, pb_ref, conv_ref, sd_ref, sh_ref,
     cw_ref, alog_ref, dtb_ref, na_ref, nb_ref, lbl_ref) = refs[:13]
    mix_ref, convo_ref, sdo_ref, sho_ref, oa_scr, ob_scr = refs[-6:]
    if layer > 0:
        prev_sd_ref, prev_sh_ref = refs[13:15]
        sdo_ref[0:layer] = prev_sd_ref[...]
        sho_ref[0:layer] = prev_sh_ref[...]
    cw = cw_ref[...]
    u = qkv_ref[...]
    b0 = conv_ref[:, 0:CONV_DIM]
    b1 = conv_ref[:, CONV_DIM:2 * CONV_DIM]
    b2 = conv_ref[:, 2 * CONV_DIM:3 * CONV_DIM]
    acc = b0 * cw[0:1, :]
    acc = acc + b1 * cw[1:2, :]
    acc = acc + b2 * cw[2:3, :]
    acc = acc + u * cw[3:4, :]
    conv = _silu(acc)
    convo_ref[:, 0:CONV_DIM] = b1
    convo_ref[:, CONV_DIM:2 * CONV_DIM] = b2
    convo_ref[:, 2 * CONV_DIM:3 * CONV_DIM] = u

    ab = ab_ref[...]
    eg_all = jnp.exp(-jnp.exp(alog_ref[...]) * _softplus(ab + dtb_ref[...]))
    beta_all = _sigmoid(ab)

    lb = _forget_lower_bound(lbl_ref[...], layer)
    f = lb + (1.0 - lb) * _sigmoid(pb_ref[:, MIX_HALF:2 * MIX_HALF])
    qb = _silu(pb_ref[:, 0:MIX_HALF])

    k_rows, q_rows, v_rows = [], [], []
    for h in range(N_HEADS):
        cs = slice(h * HEAD_DIM, (h + 1) * HEAD_DIM)
        qh = conv[:, cs]
        kh = conv[:, MIX_HALF + h * HEAD_DIM:MIX_HALF + (h + 1) * HEAD_DIM]
        q_rows.append(qh * lax.rsqrt(jnp.sum(qh * qh, axis=-1, keepdims=True) + EPS)
                      * (HEAD_DIM ** -0.5))
        k_rows.append(kh * lax.rsqrt(jnp.sum(kh * kh, axis=-1, keepdims=True) + EPS))
        v_rows.append(conv[:, 2 * MIX_HALF + h * HEAD_DIM:2 * MIX_HALF + (h + 1) * HEAD_DIM])
    f_rows = [f[:, h * HEAD_DIM:(h + 1) * HEAD_DIM] for h in range(N_HEADS)]
    qb_rows = [qb[:, h * HEAD_DIM:(h + 1) * HEAD_DIM] for h in range(N_HEADS)]
    qk_a = [jnp.sum(q_rows[h] * k_rows[h], axis=-1, keepdims=True) for h in range(N_HEADS)]
    qk_b = [jnp.sum(qb_rows[h] * (1.0 - f_rows[h]), axis=-1, keepdims=True) for h in range(N_HEADS)]

    f_cols = jnp.concatenate(f_rows + [jnp.zeros(((16 - N_HEADS) * bt, HEAD_DIM), F32)], axis=0).T

    row = lax.broadcasted_iota(jnp.int32, (8, HEAD_DIM), 0)

    def rows8(first, second=None):
        out = jnp.where(row == 0, first, 0.0)
        return out if second is None else jnp.where(row == 1, second, out)

    for b in range(bt):
        bs = slice(b, b + 1)
        for h in range(N_HEADS):
            cs = slice(h * HEAD_DIM, (h + 1) * HEAD_DIM)
            s = sd_ref[b, h]
            kq = rows8(k_rows[h][bs, :], q_rows[h][bs, :])
            eg = eg_all[bs, h:h + 1]
            beta = beta_all[bs, N_HEADS + h:N_HEADS + h + 1]
            r = _mm(kq, s)
            v_new = beta * (v_rows[h][bs, :] - eg * r[0:1, :])
            sdo_ref[layer, b, h] = s * eg + _mm_tn(kq, rows8(v_new))
            oa_scr[bs, cs] = eg * r[1:2, :] + qk_a[h][bs, :] * v_new
            s = sh_ref[b, h]
            fc = f_cols[:, h * bt + b:h * bt + b + 1]
            vb = pb_ref[bs, 2 * MIX_HALF + h * HEAD_DIM:2 * MIX_HALF + (h + 1) * HEAD_DIM]
            r = _mm(rows8(qb_rows[h][bs, :] * f_rows[h][bs, :]), s)
            sho_ref[layer, b, h] = s * fc + _mm_tn(rows8(1.0 - f_rows[h][bs, :]), rows8(vb))
            ob_scr[bs, cs] = r[0:1, :] + qk_b[h][bs, :] * vb

    for h in range(N_HEADS):
        cs = slice(h * HEAD_DIM, (h + 1) * HEAD_DIM)
        mix_ref[:, cs] = (_rms(oa_scr[:, cs], na_ref[...]) * _silu(z_ref[:, cs])).astype(mix_ref.dtype)
        gate = _sigmoid(pb_ref[:, 3 * MIX_HALF + h * HEAD_DIM:3 * MIX_HALF + (h + 1) * HEAD_DIM])
        mix_ref[:, MIX_HALF + h * HEAD_DIM:MIX_HALF + (h + 1) * HEAD_DIM] = (
            _rms(ob_scr[:, cs], nb_ref[...]) * gate).astype(mix_ref.dtype)


def _decode_call(proj, conv_state, s_delta, s_hgrn, prev_states, conv_w, a_log_rows, dt_bias_rows,
                 norm_a, norm_b, lb_logits, *, layer):
    batch = proj.shape[0]
    bt = 8
    kern = functools.partial(_decode_kernel, bt=bt, layer=layer)
    state_dims = (bt, N_HEADS, HEAD_DIM, HEAD_DIM)
    old_state_spec = pl.BlockSpec((None,) + state_dims, lambda i: (layer, i, 0, 0, 0))
    prev_state_spec = pl.BlockSpec((layer,) + state_dims, lambda i: (0, i, 0, 0, 0))
    new_state_spec = pl.BlockSpec((layer + 1,) + state_dims, lambda i: (0, i, 0, 0, 0))
    new_state_shape = jax.ShapeDtypeStruct((layer + 1, batch, N_HEADS, HEAD_DIM, HEAD_DIM), F32)
    return pl.pallas_call(
        kern,
        grid=(batch // bt,),
        in_specs=[
            pl.BlockSpec((bt, CONV_DIM), lambda i: (i, COL_QKV // CONV_DIM)),
            pl.BlockSpec((bt, MIX_HALF), lambda i: (i, COL_Z // MIX_HALF)),
            pl.BlockSpec((bt, 128), lambda i: (i, COL_AB // 128)),
            pl.BlockSpec((bt, 4 * MIX_HALF), lambda i: (i, COL_B // (4 * MIX_HALF))),
            pl.BlockSpec((None, bt, (CONV_W - 1) * CONV_DIM), lambda i: (layer, i, 0)),
            old_state_spec,
            old_state_spec,
            _layer_spec((CONV_W, CONV_DIM), layer),
            _layer_spec((1, 128), layer),
            _layer_spec((1, 128), layer),
            _layer_spec((1, HEAD_DIM), layer),
            _layer_spec((1, HEAD_DIM), layer),
            pl.BlockSpec((DEPTH, MIX_HALF), lambda i: (0, 0)),
        ] + [prev_state_spec] * len(prev_states),
        out_specs=[
            pl.BlockSpec((bt, 2 * MIX_HALF), lambda i: (i, 0)),
            pl.BlockSpec((bt, (CONV_W - 1) * CONV_DIM), lambda i: (i, 0)),
            new_state_spec,
            new_state_spec,
        ],
        out_shape=[
            jax.ShapeDtypeStruct((batch, 2 * MIX_HALF), BF16),
            jax.ShapeDtypeStruct((batch, (CONV_W - 1) * CONV_DIM), F32),
            new_state_shape,
            new_state_shape,
        ],
        scratch_shapes=[
            pltpu.VMEM((bt, MIX_HALF), F32),
            pltpu.VMEM((bt, MIX_HALF), F32),
        ],
        compiler_params=pltpu.CompilerParams(
            dimension_semantics=("arbitrary",), vmem_limit_bytes=VMEM_LIMIT),
        name="mixers_decode",
    )(proj, proj, proj, proj, conv_state, s_delta, s_hgrn, conv_w, a_log_rows, dt_bias_rows,
      norm_a, norm_b, lb_logits, *prev_states)


FF_CHUNK = 256


def _out_ffn_kernel(x_ref, ma_ref, mb_ref, g1_ref, sh2_ref, sc2_ref, g2_ref, wo_ref, n2_ref,
                    wg_ref, wu_ref, wd_ref, nf_ref, o_ref, acc_ref, *, final_norm, tiles_per_seq):
    gate1 = _mod_rows(g1_ref, tiles_per_seq)
    shift2 = _mod_rows(sh2_ref, tiles_per_seq)
    scale2 = _mod_rows(sc2_ref, tiles_per_seq)
    gate2 = _mod_rows(g2_ref, tiles_per_seq)
    mix = _mm(ma_ref[...], wo_ref[0:MIX_HALF, :]) + _mm(mb_ref[...], wo_ref[MIX_HALF:, :])
    x1 = x_ref[...] + gate1 * mix
    h = (_rms(x1, n2_ref[...]) * (1.0 + scale2) + shift2).astype(BF16)
    for c in range(D_FF // FF_CHUNK):
        fs = slice(c * FF_CHUNK, (c + 1) * FF_CHUNK)
        gate = jnp.dot(h, wg_ref[:, fs], preferred_element_type=F32)
        up = jnp.dot(h, wu_ref[:, fs], preferred_element_type=F32)
        part = _mm(_silu(gate) * up, wd_ref[fs, :])
        if c == 0:
            acc_ref[...] = part
        else:
            acc_ref[...] += part
    x2 = x1 + gate2 * acc_ref[...]
    if final_norm:
        x2 = _rms(x2, nf_ref[...])
    o_ref[...] = x2


def _out_ffn_call(x, mix_a, mix_b, mix_b_col, mods, w_out, norm2, w_gate, w_up, w_down, norm_f,
                  *, layer, n_prompt, per_token, seq_len, final_norm):
    rows = x.shape[0]
    tm = rows if per_token else 512
    tiles_per_seq = None if per_token else seq_len // tm
    resident = functools.partial(_layer_spec, layer=layer, pipeline_mode=pl.Buffered(1))
    mod = functools.partial(_mod_spec, mods, layer, n_prompt, per_token)
    kern = functools.partial(_out_ffn_kernel, final_norm=final_norm, tiles_per_seq=tiles_per_seq)
    return pl.pallas_call(
        kern,
        grid=(rows // tm,),
        in_specs=[
            pl.BlockSpec((tm, D_MODEL), lambda i: (i, 0)),
            pl.BlockSpec((tm, MIX_HALF), lambda i: (i, 0)),
            pl.BlockSpec((tm, MIX_HALF), lambda i: (i, mix_b_col)),
            mod(MOD_GATE1),
            mod(MOD_SHIFT2),
            mod(MOD_SCALE2),
            mod(MOD_GATE2),
            resident((D_MODEL, D_MODEL)),
            _layer_spec((1, D_MODEL), layer),
            resident((D_MODEL, D_FF)),
            resident((D_MODEL, D_FF)),
            resident((D_FF, D_MODEL)),
            pl.BlockSpec((1, D_MODEL), lambda i: (0, 0)),
        ],
        out_specs=pl.BlockSpec((tm, D_MODEL), lambda i: (i, 0)),
        out_shape=jax.ShapeDtypeStruct((rows, D_MODEL), F32),
        scratch_shapes=[pltpu.VMEM((tm, D_MODEL), F32)],
        compiler_params=pltpu.CompilerParams(
            dimension_semantics=("arbitrary",), vmem_limit_bytes=VMEM_LIMIT),
        name="out_ffn",
    )(x, mix_a, mix_b, mods, mods, mods, mods, w_out, norm2, w_gate, w_up, w_down, norm_f)


N_AB = 2 * N_HEADS
REORDER_COLS = 512
AB_BLOCK = COL_AB // REORDER_COLS


def _reorder_in_proj_kernel(wt_ref, o_ref):
    w = wt_ref[0]
    row = lax.broadcasted_iota(jnp.int32, w.shape, 0)
    w = jnp.where((pl.program_id(1) == AB_BLOCK) & (row >= N_AB), 0.0, w)
    o_ref[...] = w.T.astype(BF16)


def _reorder_in_proj(w):
    wt = jnp.swapaxes(w, 1, 2)

    def src_row(l, j):
        is_ab = j // AB_BLOCK
        is_b = j // (COL_B // REORDER_COLS) - 2 * is_ab
        start = j * REORDER_COLS + N_AB * is_b - (COL_AB - COL_B) * is_ab
        return (l, pl.multiple_of(start, 8), 0)

    return pl.pallas_call(
        _reorder_in_proj_kernel,
        grid=(DEPTH, pl.cdiv(IN_COLS, REORDER_COLS)),
        in_specs=[pl.BlockSpec((pl.Element(1), pl.Element(REORDER_COLS), pl.Element(D_MODEL)),
                               src_row)],
        out_specs=pl.BlockSpec((None, D_MODEL, REORDER_COLS), lambda l, j: (l, 0, j)),
        out_shape=jax.ShapeDtypeStruct((DEPTH, D_MODEL, IN_COLS), BF16),
        compiler_params=pltpu.CompilerParams(
            dimension_semantics=("arbitrary", "arbitrary"), vmem_limit_bytes=VMEM_LIMIT),
        name="reorder_in_proj",
    )(wt)


def _lane_rows(v):
    return jnp.pad(v.astype(F32), ((0, 0), (0, 128 - v.shape[1])))[:, None, :]


def kernel(x_prompt, x_sample, c_prompt, c_sample, state_delta, state_conv, state_hgrn, w_ada, b_ada,
           norm1, w_in, conv_w, a_log, dt_bias, norm_a, norm_b, lb_logits, w_out, norm2, w_gate, w_up,
           w_down, norm_f):
    bp, seq_len, _ = x_prompt.shape
    bs = x_sample.shape[0]

    mods = _ada_call(jnp.concatenate([c_sample, c_prompt], axis=0), w_ada, b_ada)

    xp = x_prompt.reshape(bp * seq_len, D_MODEL)
    xs = x_sample.reshape(bs, D_MODEL)
    zero_buf = jnp.zeros((bp, 8, CONV_DIM), F32)
    zero_state = jnp.zeros((bp, N_HEADS, HEAD_DIM, HEAD_DIM), F32)

    w_in_b = _reorder_in_proj(w_in)
    w_out_b = w_out.astype(BF16)
    w_gate_b = w_gate.astype(BF16)
    w_up_b = w_up.astype(BF16)
    w_down_b = w_down.astype(BF16)
    a_log_rows = _lane_rows(a_log)
    dt_bias_rows = _lane_rows(dt_bias)
    norm1_r = norm1[:, None, :]
    norm2_r = norm2[:, None, :]
    norm_a_r = norm_a[:, None, :]
    norm_b_r = norm_b[:, None, :]
    norm_f_r = norm_f[None, :]
    conv_state = state_conv.reshape(DEPTH, bs, (CONV_W - 1) * CONV_DIM)

    delta_p, conv_p, hgrn_p, conv_s = [], [], [], []
    decode_states = ()
    for l in range(DEPTH):
        last = l == DEPTH - 1
        dense = dict(layer=l, n_prompt=bp)

        proj, cv = _in_proj_call(xp, mods, norm1_r, w_in_b, zero_buf, conv_w, lb_logits,
                                 per_token=False, seq_len=seq_len, **dense)
        proj = proj.reshape(bp, seq_len, PROMPT_COLS)
        mix_a, s_a = _gdn_call(proj, zero_state, a_log_rows, dt_bias_rows, norm_a_r, layer=l)
        mix_b, s_b = _hgrn_call(proj, norm_b_r, zero_state, layer=l)
        mix_a = mix_a.reshape(bp * seq_len, MIX_HALF)
        mix_b = mix_b.reshape(bp * seq_len, MIX_HALF)
        xp = _out_ffn_call(xp, mix_a, mix_b, 0, mods, w_out_b, norm2_r, w_gate_b, w_up_b, w_down_b,
                           norm_f_r, per_token=False, seq_len=seq_len, final_norm=last, **dense)
        delta_p.append(s_a)
        conv_p.append(cv)
        hgrn_p.append(s_b)

        proj = _in_proj_call(xs, mods, norm1_r, w_in_b, None, None, None,
                             per_token=True, seq_len=1, **dense)
        mix, cv, *decode_states = _decode_call(
            proj, conv_state, state_delta, state_hgrn, decode_states,
            conv_w, a_log_rows, dt_bias_rows, norm_a_r, norm_b_r, lb_logits, layer=l)
        xs = _out_ffn_call(xs, mix, mix, 1, mods, w_out_b, norm2_r, w_gate_b, w_up_b, w_down_b,
                           norm_f_r, per_token=True, seq_len=1, final_norm=last, **dense)
        conv_s.append(cv.reshape(bs, CONV_W - 1, CONV_DIM))

    delta_s, hgrn_s = decode_states
    return (xp.reshape(bp, seq_len, D_MODEL), xs.reshape(bs, 1, D_MODEL),
            jnp.stack(delta_p), jnp.stack(conv_p), jnp.stack(hgrn_p),
            delta_s, jnp.stack(conv_s), hgrn_s)
```

```python
import functools

import jax
import jax.numpy as jnp
from jax import lax
from jax.experimental import pallas as pl
from jax.experimental.pallas import tpu as pltpu

F32 = jnp.float32
BF16 = jnp.bfloat16

D_MODEL = 1024
DEPTH = 2
N_HEADS = 4
HEAD_DIM = 128
MIX_HALF = N_HEADS * HEAD_DIM
CONV_W = 4
CONV_DIM = 3 * MIX_HALF
D_FF = 2816
EPS = 1e-6
GDN_CHUNK = 128
GDN_SUB = 16
HGRN_CHUNK = 64

COL_QKV = 0
COL_Z = CONV_DIM
COL_B = COL_Z + MIX_HALF
COL_AB = COL_B + 4 * MIX_HALF
IN_COLS = COL_AB + 128
PROMPT_COL_G = COL_AB
PROMPT_COL_AB = PROMPT_COL_G + MIX_HALF
PROMPT_COLS = PROMPT_COL_AB + 128

MOD_SHIFT1, MOD_SCALE1, MOD_GATE1, MOD_SHIFT2, MOD_SCALE2, MOD_GATE2 = range(6)

VMEM_LIMIT = 56 * 1024 * 1024


def _sigmoid(x):
    return 1.0 / (1.0 + jnp.exp(-x))


def _silu(x):
    return x * _sigmoid(x)


def _softplus(x):
    return jnp.maximum(x, 0.0) + jnp.log1p(jnp.exp(-jnp.abs(x)))


def _mm(a, b):
    return jnp.dot(a.astype(BF16), b.astype(BF16), preferred_element_type=F32)


def _mm_nt(a, b):
    return lax.dot_general(a.astype(BF16), b.astype(BF16), (((1,), (1,)), ((), ())),
                           preferred_element_type=F32)


def _mm_tn(a, b):
    return lax.dot_general(a.astype(BF16), b.astype(BF16), (((0,), (0,)), ((), ())),
                           preferred_element_type=F32)


def _rms(x, w):
    return x * lax.rsqrt(jnp.mean(x * x, axis=-1, keepdims=True) + EPS) * w


def _chunk_cumsum(x, chunk):
    rows = lax.broadcasted_iota(jnp.int32, x.shape, 0) & (chunk - 1)
    s = 1
    while s < chunk:
        x = x + jnp.where(rows >= s, pltpu.roll(x, s, 0), 0.0)
        s *= 2
    return x


def _ada_kernel(c_ref, w_ref, b_ref, o_ref):
    a = _silu(c_ref[...])
    o_ref[...] = _mm(a, w_ref[...]) + b_ref[...]


def _ada_call(c_all, w_ada, b_ada):
    rows = c_all.shape[0]
    tn = 2048
    return pl.pallas_call(
        _ada_kernel,
        grid=(DEPTH, 6 * D_MODEL // tn),
        in_specs=[
            pl.BlockSpec((rows, D_MODEL), lambda l, j: (0, 0)),
            pl.BlockSpec((None, D_MODEL, tn), lambda l, j: (l, 0, j)),
            pl.BlockSpec((None, 1, tn), lambda l, j: (l, 0, j)),
        ],
        out_specs=pl.BlockSpec((None, rows, tn), lambda l, j: (l, 0, j)),
        out_shape=jax.ShapeDtypeStruct((DEPTH, rows, 6 * D_MODEL), F32),
        compiler_params=pltpu.CompilerParams(
            dimension_semantics=("arbitrary", "arbitrary"), vmem_limit_bytes=VMEM_LIMIT),
        name="ada_mod",
    )(c_all, w_ada, b_ada.reshape(DEPTH, 1, 6 * D_MODEL))


def _mod_spec(mods, layer, n_prompt, per_token, col):
    n_decode = mods.shape[1] - n_prompt
    if per_token:
        return pl.BlockSpec((None, n_decode, D_MODEL), lambda i: (layer, 0, col))
    return pl.BlockSpec((None, n_prompt, D_MODEL), lambda i: (layer, n_decode // n_prompt, col))


def _mod_rows(ref, tiles_per_seq):
    if tiles_per_seq is None:
        return ref[...]
    return ref[pl.ds(pl.program_id(0) // tiles_per_seq, 1), :]


def _layer_spec(shape, layer, **kwargs):
    zeros = (0,) * len(shape)
    return pl.BlockSpec((None,) + tuple(shape), lambda *_: (layer,) + zeros, **kwargs)


def _modulated_input(x_ref, scale_ref, shift_ref, nw_ref, tile, tiles_per_seq):
    if tiles_per_seq is None:
        scale, shift = scale_ref[...], shift_ref[...]
    else:
        seq = tile // tiles_per_seq
        scale, shift = scale_ref[pl.ds(seq, 1), :], shift_ref[pl.ds(seq, 1), :]
    return (_rms(x_ref[...], nw_ref[...]) * (1.0 + scale) + shift).astype(BF16)


def _in_proj_decode_kernel(x_ref, scale_ref, shift_ref, nw_ref, w_ref, o_ref):
    h = _modulated_input(x_ref, scale_ref, shift_ref, nw_ref, 0, None)
    o_ref[...] = jnp.dot(h, w_ref[...], preferred_element_type=F32)


def _in_proj_prompt_kernel(x_ref, scale_ref, shift_ref, nw_ref, w_ref, buf_ref, cw_ref, lbl_ref,
                           o_ref, cfin_ref, raw, h_scr, *, tiles_per_seq, layer):
    i = pl.program_id(0)
    tm = x_ref.shape[0]
    tile_in_seq = i % tiles_per_seq
    body = slice(8, 8 + tm)

    n_slabs = CONV_DIM // HEAD_DIM

    def slab_cols(ct):
        return slice(ct * HEAD_DIM, (ct + 1) * HEAD_DIM)

    @pl.when(tile_in_seq == 0)
    def _():
        for ct in range(n_slabs):
            raw[ct, 0:8, :] = buf_ref[:, slab_cols(ct)]

    h_scr[...] = _modulated_input(x_ref, scale_ref, shift_ref, nw_ref, i, tiles_per_seq)
    lb_all = _forget_lower_bound(lbl_ref[...], layer)

    def project(cs):
        return jnp.dot(h_scr[...], w_ref[:, cs], preferred_element_type=F32)

    def conv_slabs(cs):
        pre = project(cs)
        out = []
        for ct in range(cs.start // HEAD_DIM, cs.stop // HEAD_DIM):
            cols = slab_cols(ct)
            raw[ct, body, :] = pre[:, cols.start - cs.start:cols.stop - cs.start]
            acc = raw[ct, 5:5 + tm, :] * cw_ref[0:1, cols]
            for j in range(1, CONV_W):
                acc = acc + raw[ct, 5 + j:5 + j + tm, :] * cw_ref[j:j + 1, cols]
            out.append((cols, _silu(acc)))
        return out

    def l2_normalised(cs, scale):
        for cols, v in conv_slabs(cs):
            o_ref[:, cols] = v * lax.rsqrt(jnp.sum(v * v, axis=-1, keepdims=True) + EPS) * scale

    def post_q(cs):
        l2_normalised(cs, HEAD_DIM ** -0.5)

    def post_k(cs):
        l2_normalised(cs, 1.0)

    def post_v(cs):
        for cols, v in conv_slabs(cs):
            o_ref[:, cols] = v

    def post_copy(cs):
        o_ref[:, cs] = project(cs)

    def post_silu(cs):
        o_ref[:, cs] = _silu(project(cs))

    def post_forget(cs):
        gate_cols = slice(cs.start - COL_B - MIX_HALF, cs.stop - COL_B - MIX_HALF)
        lb = lb_all[:, gate_cols]
        f = lb + (1.0 - lb) * _sigmoid(project(cs))
        o_ref[:, cs] = 1.0 - f
        o_ref[:, PROMPT_COL_G + gate_cols.start:PROMPT_COL_G + gate_cols.stop] = (
            _chunk_cumsum(jnp.log(f), HGRN_CHUNK))

    width = 2 * HEAD_DIM
    pieces = ([post_q] * 2 + [post_k] * 2 + [post_v] * 2 + [post_copy] * 2
              + [post_silu] * 2 + [post_forget] * 2 + [post_copy] * 4)
    order = [0, 6, 1, 7, 2, 12, 3, 13, 4, 14, 5, 15, 8, 9, 10, 11]
    for piece in order:
        pieces[piece](slice(piece * width, (piece + 1) * width))
    o_ref[:, PROMPT_COL_AB:PROMPT_COLS] = project(slice(COL_AB, IN_COLS))

    for ct in range(n_slabs):
        raw[ct, 0:8, :] = raw[ct, tm:tm + 8, :]

    @pl.when(tile_in_seq == tiles_per_seq - 1)
    def _():
        for ct in range(n_slabs):
            cfin_ref[:, slab_cols(ct)] = raw[ct, 8 - (CONV_W - 1):8, :]


def _in_proj_call(x, mods, norm1, w_in, conv_buf, conv_w, lb_logits, *, layer, n_prompt, per_token,
                  seq_len):
    rows = x.shape[0]
    in_specs = [
        None,
        _mod_spec(mods, layer, n_prompt, per_token, MOD_SCALE1),
        _mod_spec(mods, layer, n_prompt, per_token, MOD_SHIFT1),
        _layer_spec((1, D_MODEL), layer),
        _layer_spec((D_MODEL, IN_COLS), layer, pipeline_mode=pl.Buffered(1)),
    ]
    params = pltpu.CompilerParams(dimension_semantics=("arbitrary",), vmem_limit_bytes=VMEM_LIMIT)
    if per_token:
        in_specs[0] = pl.BlockSpec((rows, D_MODEL), lambda i: (0, 0))
        return pl.pallas_call(
            _in_proj_decode_kernel,
            grid=(1,),
            in_specs=in_specs,
            out_specs=pl.BlockSpec((rows, IN_COLS), lambda i: (0, 0)),
            out_shape=jax.ShapeDtypeStruct((rows, IN_COLS), F32),
            compiler_params=params,
            name="in_proj_decode",
        )(x, mods, mods, norm1, w_in)

    tm = 256
    tiles_per_seq = seq_len // tm
    in_specs[0] = pl.BlockSpec((tm, D_MODEL), lambda i: (i, 0))
    in_specs += [
        pl.BlockSpec((None, 8, CONV_DIM), lambda i: (i // tiles_per_seq, 0, 0)),
        _layer_spec((CONV_W, CONV_DIM), layer),
        pl.BlockSpec((DEPTH, MIX_HALF), lambda i: (0, 0)),
    ]
    kern = functools.partial(_in_proj_prompt_kernel, tiles_per_seq=tiles_per_seq, layer=layer)
    return pl.pallas_call(
        kern,
        grid=(rows // tm,),
        in_specs=in_specs,
        out_specs=[
            pl.BlockSpec((tm, PROMPT_COLS), lambda i: (i, 0)),
            pl.BlockSpec((None, CONV_W - 1, CONV_DIM), lambda i: (i // tiles_per_seq, 0, 0)),
        ],
        out_shape=[
            jax.ShapeDtypeStruct((rows, PROMPT_COLS), F32),
            jax.ShapeDtypeStruct((n_prompt, CONV_W - 1, CONV_DIM), F32),
        ],
        scratch_shapes=[
            pltpu.VMEM((CONV_DIM // HEAD_DIM, tm + 8, HEAD_DIM), F32),
            pltpu.VMEM((tm, D_MODEL), BF16),
        ],
        compiler_params=params,
        name="in_proj_prompt",
    )(x, mods, mods, norm1, w_in, conv_buf, conv_w, lb_logits)


def _unit_lower_inverse(mats, sub_mask, eye):
    d = [jnp.where(sub_mask, a, 0.0) for a in mats]
    n = [a - x for a, x in zip(mats, d)]
    d2 = [_mm(x, x) for x in d]
    d4 = [_mm(x, x) for x in d2]
    d8 = [_mm(x, x) for x in d4]
    p = [eye - x + x2 - _mm(x, x2) for x, x2 in zip(d, d2)]
    p = [x + _mm(x, y) for x, y in zip(p, d4)]
    dinv = [x + _mm(x, y) for x, y in zip(p, d8)]
    e = [_mm(x, y) for x, y in zip(dinv, n)]
    e2 = [_mm(x, x) for x in e]
    e4 = [_mm(x, x) for x in e2]
    f = [x2 - x - _mm(x, x2) for x, x2 in zip(e, e2)]
    q = [x + y + _mm(x, y) for x, y in zip(f, e4)]
    return [x + _mm(y, x) for x, y in zip(dinv, q)]


def _gdn_kernel(qkv_ref, z_ref, ab_ref, s0_ref, alog_ref, dtb_ref, na_ref,
                mix_ref, sfin_ref, s_scr, *, tile):
    t = pl.program_id(1)

    @pl.when(t == 0)
    def _():
        s_scr[...] = s0_ref[...]

    ri = lax.broadcasted_iota(jnp.int32, (GDN_CHUNK, GDN_CHUNK), 0)
    ci = lax.broadcasted_iota(jnp.int32, (GDN_CHUNK, GDN_CHUNK), 1)
    m_incl = ri >= ci
    m_strict = ri > ci
    sub_mask = jnp.bitwise_xor(ri, ci) < GDN_SUB
    eye = jnp.where(ri == ci, 1.0, 0.0).astype(F32)
    n_chunks = tile // GDN_CHUNK
    pairs = [(b, h) for b in range(qkv_ref.shape[0]) for h in range(N_HEADS)]
    chains = [(p, c) for p in pairs for c in range(n_chunks)]

    def rows(c):
        return slice(c * GDN_CHUNK, (c + 1) * GDN_CHUNK)

    g_cum_t = {}
    qn, kn, kb, gc, eg, rhs = {}, {}, {}, {}, {}, {}
    for b in range(qkv_ref.shape[0]):
        ab = ab_ref[b]
        g_all = -jnp.exp(alog_ref[...]) * _softplus(ab + dtb_ref[...])
        sig_all = _sigmoid(ab)
        g_cum = _chunk_cumsum(g_all, GDN_CHUNK)
        g_cum_t[b] = g_cum.T
        for h in range(N_HEADS):
            p = (b, h)
            qn[p] = qkv_ref[b, :, h * HEAD_DIM:(h + 1) * HEAD_DIM]
            kn[p] = qkv_ref[b, :, MIX_HALF + h * HEAD_DIM:MIX_HALF + (h + 1) * HEAD_DIM]
            vh = qkv_ref[b, :, 2 * MIX_HALF + h * HEAD_DIM:2 * MIX_HALF + (h + 1) * HEAD_DIM]
            beta = sig_all[:, N_HEADS + h:N_HEADS + h + 1]
            gc[p] = g_cum[:, h:h + 1]
            eg[p] = jnp.exp(gc[p])
            kb[p] = kn[p] * beta
            rhs[p] = jnp.concatenate([vh * beta, kb[p] * eg[p]], axis=1)

    dec = {}
    for p, c in chains:
        gr = g_cum_t[p[0]][p[1]:p[1] + 1, rows(c)]
        dec[p, c] = jnp.exp(jnp.where(m_incl, gc[p][rows(c)] - gr, -jnp.inf))
    st = {(p, c): _mm_nt(jnp.concatenate([kb[p][rows(c)], qn[p][rows(c)]], axis=0), kn[p][rows(c)])
          for p, c in chains}
    a = [st[pc][:GDN_CHUNK] * jnp.where(m_strict, dec[pc], 0.0) for pc in chains]
    qk = {pc: st[pc][GDN_CHUNK:] * dec[pc] for pc in chains}
    tinv = dict(zip(chains, _unit_lower_inverse(a, sub_mask, eye)))
    uw = {(p, c): _mm(tinv[p, c], rhs[p][rows(c)]) for p, c in chains}

    s = {p: s_scr[p] for p in pairs}
    for c in range(n_chunks):
        rs = rows(c)
        r = {p: _mm(jnp.concatenate([uw[p, c][:, HEAD_DIM:], qn[p][rs] * eg[p][rs]], axis=0), s[p])
             for p in pairs}
        v_new = {p: uw[p, c][:, :HEAD_DIM] - r[p][:GDN_CHUNK] for p in pairs}
        for p in pairs:
            g_last = gc[p][(c + 1) * GDN_CHUNK - 1:(c + 1) * GDN_CHUNK, :]
            kd = kn[p][rs] * jnp.exp(g_last - gc[p][rs])
            s[p] = s[p] * jnp.exp(g_last) + _mm_tn(kd, v_new[p])
        for b, h in pairs:
            cs = slice(h * HEAD_DIM, (h + 1) * HEAD_DIM)
            o = r[b, h][GDN_CHUNK:] + _mm(qk[(b, h), c], v_new[b, h])
            mix_ref[b, rs, cs] = (_rms(o, na_ref[...]) * _silu(z_ref[b, rs, cs])).astype(mix_ref.dtype)
    for p in pairs:
        s_scr[p] = s[p]

    @pl.when(t == pl.num_programs(1) - 1)
    def _():
        sfin_ref[...] = s_scr[...]


MIXER_SEQS = 2
MIXER_TILE = 256


def _mixer_specs(cols, col_block):
    return pl.BlockSpec((MIXER_SEQS, MIXER_TILE, cols), lambda b, t: (b, t, col_block))


_STATE_BLOCK = (MIXER_SEQS, N_HEADS, HEAD_DIM, HEAD_DIM)


def _gdn_call(proj, s0, a_log_rows, dt_bias_rows, norm_a, *, layer):
    batch, seq_len, _ = proj.shape
    state_spec = pl.BlockSpec(_STATE_BLOCK, lambda b, t: (b, 0, 0, 0))
    return pl.pallas_call(
        functools.partial(_gdn_kernel, tile=MIXER_TILE),
        grid=(batch // MIXER_SEQS, seq_len // MIXER_TILE),
        in_specs=[
            _mixer_specs(CONV_DIM, COL_QKV // CONV_DIM),
            _mixer_specs(MIX_HALF, COL_Z // MIX_HALF),
            _mixer_specs(128, PROMPT_COL_AB // 128),
            state_spec,
            _layer_spec((1, 128), layer),
            _layer_spec((1, 128), layer),
            _layer_spec((1, HEAD_DIM), layer),
        ],
        out_specs=[_mixer_specs(MIX_HALF, 0), state_spec],
        out_shape=[
            jax.ShapeDtypeStruct((batch, seq_len, MIX_HALF), BF16),
            jax.ShapeDtypeStruct((batch, N_HEADS, HEAD_DIM, HEAD_DIM), F32),
        ],
        scratch_shapes=[pltpu.VMEM(_STATE_BLOCK, F32)],
        compiler_params=pltpu.CompilerParams(
            dimension_semantics=("arbitrary", "arbitrary"), vmem_limit_bytes=VMEM_LIMIT),
        name="gdn_prompt",
    )(proj, proj, proj, s0, a_log_rows, dt_bias_rows, norm_a)


def _forget_lower_bound(lbl, layer):
    m = jnp.max(lbl, axis=0, keepdims=True)
    e = jnp.exp(lbl - m)
    sm = e / jnp.sum(e, axis=0, keepdims=True)
    cs = sm[0:1, :]
    for i in range(1, layer + 1):
        cs = cs + sm[i:i + 1, :]
    return cs - sm[0:1, :]


def _midpoint_rows(g, s, row_id):
    n, width = g.shape
    if s >= 8:
        parts = [jnp.broadcast_to(g[b + s:b + s + 1, :], (2 * s, width))
                 for b in range(0, n, 2 * s)]
        return jnp.concatenate(parts, axis=0)
    tiles = g.reshape(n // 8, 8, width)
    sub = row_id.reshape(n // 8, 8, width) & 7
    mids = range(s, 8, 2 * s)
    out = jnp.broadcast_to(tiles[:, mids[-1]:mids[-1] + 1, :], tiles.shape)
    for m in reversed(mids[:-1]):
        out = jnp.where(sub < m + s, jnp.broadcast_to(tiles[:, m:m + 1, :], tiles.shape), out)
    return out.reshape(n, width)


def _hgrn_kernel(p_ref, g_ref, nb_ref, s0_ref, mix_ref, sfin_ref, st_scr, *, tile):
    t = pl.program_id(1)
    pairs = [(b, h) for b in range(p_ref.shape[0]) for h in range(N_HEADS)]

    @pl.when(t == 0)
    def _():
        for p in pairs:
            st_scr[p] = s0_ref[p].T

    n_chunks = tile // HGRN_CHUNK
    chains = [(p, c) for p in pairs for c in range(n_chunks)]

    def rows(c):
        return slice(c * HGRN_CHUNK, (c + 1) * HGRN_CHUNK)

    def cols(block, h):
        return slice(block * MIX_HALF + h * HEAD_DIM, block * MIX_HALF + (h + 1) * HEAD_DIM)

    row_id = lax.broadcasted_iota(jnp.int32, (tile, HEAD_DIM), 0)
    ri = lax.broadcasted_iota(jnp.int32, (HGRN_CHUNK, HGRN_CHUNK), 0)
    ci = lax.broadcasted_iota(jnp.int32, (HGRN_CHUNK, HGRN_CHUNK), 1)
    dist = jnp.bitwise_xor(ri, ci)
    lower = ri > ci

    q = {(b, h): p_ref[b, :, cols(0, h)] for b, h in pairs}
    k = {(b, h): p_ref[b, :, cols(1, h)] for b, h in pairs}
    g = {(b, h): g_ref[b, :, cols(0, h)] for b, h in pairs}

    chunk_row = row_id[:HGRN_CHUNK]
    levels = [HGRN_CHUNK >> (i + 1) for i in range(HGRN_CHUNK.bit_length() - 1)]
    upper = {s: (chunk_row & s) != 0 for s in levels}
    level_mask = {s: lower & (dist >= s) & (dist < 2 * s) for s in levels}
    a = {}
    for p, c in chains:
        qc, kc, gc = q[p][rows(c)], k[p][rows(c)], g[p][rows(c)]
        acc = jnp.where(ri == ci, _mm_nt(qc, kc), 0.0)
        for s in levels:
            d = gc - _midpoint_rows(gc, s, chunk_row)
            xc = jnp.where(upper[s], qc, kc) * jnp.exp(-jnp.abs(d))
            acc = jnp.where(level_mask[s], _mm_nt(xc, xc), acc)
        a[p, c] = acc

    st = {p: st_scr[p] for p in pairs}
    for c in range(n_chunks):
        rs = rows(c)
        for b, h in pairs:
            p = (b, h)
            gc = g[p][rs]
            vc = p_ref[b, rs, cols(2, h)]
            o = _mm(a[p, c], vc) + _mm_nt(q[p][rs] * jnp.exp(gc), st[p])
            g_last = gc[HGRN_CHUNK - 1:HGRN_CHUNK, :]
            st[p] = st[p] * jnp.exp(g_last) + _mm_tn(vc, k[p][rs] * jnp.exp(g_last - gc))
            gate = _sigmoid(p_ref[b, rs, cols(3, h)])
            mix_ref[b, rs, cols(0, h)] = (_rms(o, nb_ref[...]) * gate).astype(mix_ref.dtype)
    for p in pairs:
        st_scr[p] = st[p]

    @pl.when(t == pl.num_programs(1) - 1)
    def _():
        for p in pairs:
            sfin_ref[p] = st_scr[p].T


def _hgrn_call(proj, norm_b, s0, *, layer):
    batch, seq_len, _ = proj.shape
    state_spec = pl.BlockSpec(_STATE_BLOCK, lambda b, t: (b, 0, 0, 0))
    return pl.pallas_call(
        functools.partial(_hgrn_kernel, tile=MIXER_TILE),
        grid=(batch // MIXER_SEQS, seq_len // MIXER_TILE),
        in_specs=[
            _mixer_specs(4 * MIX_HALF, COL_B // (4 * MIX_HALF)),
            _mixer_specs(MIX_HALF, PROMPT_COL_G // MIX_HALF),
            _layer_spec((1, HEAD_DIM), layer),
            state_spec,
        ],
        out_specs=[_mixer_specs(MIX_HALF, 0), state_spec],
        out_shape=[
            jax.ShapeDtypeStruct((batch, seq_len, MIX_HALF), BF16),
            jax.ShapeDtypeStruct((batch, N_HEADS, HEAD_DIM, HEAD_DIM), F32),
        ],
        scratch_shapes=[pltpu.VMEM(_STATE_BLOCK, F32)],
        compiler_params=pltpu.CompilerParams(
            dimension_semantics=("arbitrary", "arbitrary"), vmem_limit_bytes=VMEM_LIMIT),
        name="hgrn_prompt",
    )(proj, proj, norm_b, s0)


def _decode_kernel(*refs, bt, layer):
    (qkv_ref, z_ref, ab_ref, pb_ref, conv_ref, sd_ref, sh_ref,
     cw_ref, alog_ref, dtb_ref, na_ref, nb_ref, lbl_ref) = refs[:13]
    mix_ref, convo_ref, sdo_ref, sho_ref, oa_scr, ob_scr = refs[-6:]
    if layer > 0:
        prev_sd_ref, prev_sh_ref = refs[13:15]
        sdo_ref[0:layer] = prev_sd_ref[...]
        sho_ref[0:layer] = prev_sh_ref[...]
    cw = cw_ref[...]
    u = qkv_ref[...]
    b0 = conv_ref[:, 0:CONV_DIM]
    b1 = conv_ref[:, CONV_DIM:2 * CONV_DIM]
    b2 = conv_ref[:, 2 * CONV_DIM:3 * CONV_DIM]
    acc = b0 * cw[0:1, :]
    acc = acc + b1 * cw[1:2, :]
    acc = acc + b2 * cw[2:3, :]
    acc = acc + u * cw[3:4, :]
    conv = _silu(acc)
    convo_ref[:, 0:CONV_DIM] = b1
    convo_ref[:, CONV_DIM:2 * CONV_DIM] = b2
    convo_ref[:, 2 * CONV_DIM:3 * CONV_DIM] = u

    ab = ab_ref[...]
    eg_all = jnp.exp(-jnp.exp(alog_ref[...]) * _softplus(ab + dtb_ref[...]))
    beta_all = _sigmoid(ab)

    lb = _forget_lower_bound(lbl_ref[...], layer)
    f = lb + (1.0 - lb) * _sigmoid(pb_ref[:, MIX_HALF:2 * MIX_HALF])
    qb = _silu(pb_ref[:, 0:MIX_HALF])

    k_rows, q_rows, v_rows = [], [], []
    for h in range(N_HEADS):
        cs = slice(h * HEAD_DIM, (h + 1) * HEAD_DIM)
        qh = conv[:, cs]
        kh = conv[:, MIX_HALF + h * HEAD_DIM:MIX_HALF + (h + 1) * HEAD_DIM]
        q_rows.append(qh * lax.rsqrt(jnp.sum(qh * qh, axis=-1, keepdims=True) + EPS)
                      * (HEAD_DIM ** -0.5))
        k_rows.append(kh * lax.rsqrt(jnp.sum(kh * kh, axis=-1, keepdims=True) + EPS))
        v_rows.append(conv[:, 2 * MIX_HALF + h * HEAD_DIM:2 * MIX_HALF + (h + 1) * HEAD_DIM])
    f_rows = [f[:, h * HEAD_DIM:(h + 1) * HEAD_DIM] for h in range(N_HEADS)]
    qb_rows = [qb[:, h * HEAD_DIM:(h + 1) * HEAD_DIM] for h in range(N_HEADS)]
    qk_a = [jnp.sum(q_rows[h] * k_rows[h], axis=-1, keepdims=True) for h in range(N_HEADS)]
    qk_b = [jnp.sum(qb_rows[h] * (1.0 - f_rows[h]), axis=-1, keepdims=True) for h in range(N_HEADS)]

    f_cols = jnp.concatenate(f_rows + [jnp.zeros(((16 - N_HEADS) * bt, HEAD_DIM), F32)], axis=0).T

    row = lax.broadcasted_iota(jnp.int32, (8, HEAD_DIM), 0)

    def rows8(first, second=None):
        out = jnp.where(row == 0, first, 0.0)
        return out if second is None else jnp.where(row == 1, second, out)

    for b in range(bt):
        bs = slice(b, b + 1)
        for h in range(N_HEADS):
            cs = slice(h * HEAD_DIM, (h + 1) * HEAD_DIM)
            s = sd_ref[b, h]
            kq = rows8(k_rows[h][bs, :], q_rows[h][bs, :])
            eg = eg_all[bs, h:h + 1]
            beta = beta_all[bs, N_HEADS + h:N_HEADS + h + 1]
            r = _mm(kq, s)
            v_new = beta * (v_rows[h][bs, :] - eg * r[0:1, :])
            sdo_ref[layer, b, h] = s * eg + _mm_tn(kq, rows8(v_new))
            oa_scr[bs, cs] = eg * r[1:2, :] + qk_a[h][bs, :] * v_new
            s = sh_ref[b, h]
            fc = f_cols[:, h * bt + b:h * bt + b + 1]
            vb = pb_ref[bs, 2 * MIX_HALF + h * HEAD_DIM:2 * MIX_HALF + (h + 1) * HEAD_DIM]
            r = _mm(rows8(qb_rows[h][bs, :] * f_rows[h][bs, :]), s)
            sho_ref[layer, b, h] = s * fc + _mm_tn(rows8(1.0 - f_rows[h][bs, :]), rows8(vb))
            ob_scr[bs, cs] = r[0:1, :] + qk_b[h][bs, :] * vb

    for h in range(N_HEADS):
        cs = slice(h * HEAD_DIM, (h + 1) * HEAD_DIM)
        mix_ref[:, cs] = (_rms(oa_scr[:, cs], na_ref[...]) * _silu(z_ref[:, cs])).astype(mix_ref.dtype)
        gate = _sigmoid(pb_ref[:, 3 * MIX_HALF + h * HEAD_DIM:3 * MIX_HALF + (h + 1) * HEAD_DIM])
        mix_ref[:, MIX_HALF + h * HEAD_DIM:MIX_HALF + (h + 1) * HEAD_DIM] = (
            _rms(ob_scr[:, cs], nb_ref[...]) * gate).astype(mix_ref.dtype)


def _decode_call(proj, conv_state, s_delta, s_hgrn, prev_states, conv_w, a_log_rows, dt_bias_rows,
                 norm_a, norm_b, lb_logits, *, layer):
    batch = proj.shape[0]
    bt = 8
    kern = functools.partial(_decode_kernel, bt=bt, layer=layer)
    state_dims = (bt, N_HEADS, HEAD_DIM, HEAD_DIM)
    old_state_spec = pl.BlockSpec((None,) + state_dims, lambda i: (layer, i, 0, 0, 0))
    prev_state_spec = pl.BlockSpec((layer,) + state_dims, lambda i: (0, i, 0, 0, 0))
    new_state_spec = pl.BlockSpec((layer + 1,) + state_dims, lambda i: (0, i, 0, 0, 0))
    new_state_shape = jax.ShapeDtypeStruct((layer + 1, batch, N_HEADS, HEAD_DIM, HEAD_DIM), F32)
    return pl.pallas_call(
        kern,
        grid=(batch // bt,),
        in_specs=[
            pl.BlockSpec((bt, CONV_DIM), lambda i: (i, COL_QKV // CONV_DIM)),
            pl.BlockSpec((bt, MIX_HALF), lambda i: (i, COL_Z // MIX_HALF)),
            pl.BlockSpec((bt, 128), lambda i: (i, COL_AB // 128)),
            pl.BlockSpec((bt, 4 * MIX_HALF), lambda i: (i, COL_B // (4 * MIX_HALF))),
            pl.BlockSpec((None, bt, (CONV_W - 1) * CONV_DIM), lambda i: (layer, i, 0)),
            old_state_spec,
            old_state_spec,
            _layer_spec((CONV_W, CONV_DIM), layer),
            _layer_spec((1, 128), layer),
            _layer_spec((1, 128), layer),
            _layer_spec((1, HEAD_DIM), layer),
            _layer_spec((1, HEAD_DIM), layer),
            pl.BlockSpec((DEPTH, MIX_HALF), lambda i: (0, 0)),
        ] + [prev_state_spec] * len(prev_states),
        out_specs=[
            pl.BlockSpec((bt, 2 * MIX_HALF), lambda i: (i, 0)),
            pl.BlockSpec((bt, (CONV_W - 1) * CONV_DIM), lambda i: (i, 0)),
            new_state_spec,
            new_state_spec,
        ],
        out_shape=[
            jax.ShapeDtypeStruct((batch, 2 * MIX_HALF), BF16),
            jax.ShapeDtypeStruct((batch, (CONV_W - 1) * CONV_DIM), F32),
            new_state_shape,
            new_state_shape,
        ],
        scratch_shapes=[
            pltpu.VMEM((bt, MIX_HALF), F32),
            pltpu.VMEM((bt, MIX_HALF), F32),
        ],
        compiler_params=pltpu.CompilerParams(
            dimension_semantics=("arbitrary",), vmem_limit_bytes=VMEM_LIMIT),
        name="mixers_decode",
    )(proj, proj, proj, proj, conv_state, s_delta, s_hgrn, conv_w, a_log_rows, dt_bias_rows,
      norm_a, norm_b, lb_logits, *prev_states)


FF_CHUNK = 256


def _out_ffn_kernel(x_ref, ma_ref, mb_ref, g1_ref, sh2_ref, sc2_ref, g2_ref, wo_ref, n2_ref,
                    wg_ref, wu_ref, wd_ref, nf_ref, o_ref, acc_ref, *, final_norm, tiles_per_seq):
    gate1 = _mod_rows(g1_ref, tiles_per_seq)
    shift2 = _mod_rows(sh2_ref, tiles_per_seq)
    scale2 = _mod_rows(sc2_ref, tiles_per_seq)
    gate2 = _mod_rows(g2_ref, tiles_per_seq)
    mix = _mm(ma_ref[...], wo_ref[0:MIX_HALF, :]) + _mm(mb_ref[...], wo_ref[MIX_HALF:, :])
    x1 = x_ref[...] + gate1 * mix
    h = (_rms(x1, n2_ref[...]) * (1.0 + scale2) + shift2).astype(BF16)
    for c in range(D_FF // FF_CHUNK):
        fs = slice(c * FF_CHUNK, (c + 1) * FF_CHUNK)
        gate = jnp.dot(h, wg_ref[:, fs], preferred_element_type=F32)
        up = jnp.dot(h, wu_ref[:, fs], preferred_element_type=F32)
        part = _mm(_silu(gate) * up, wd_ref[fs, :])
        if c == 0:
            acc_ref[...] = part
        else:
            acc_ref[...] += part
    x2 = x1 + gate2 * acc_ref[...]
    if final_norm:
        x2 = _rms(x2, nf_ref[...])
    o_ref[...] = x2


def _out_ffn_call(x, mix_a, mix_b, mix_b_col, mods, w_out, norm2, w_gate, w_up, w_down, norm_f,
                  *, layer, n_prompt, per_token, seq_len, final_norm):
    rows = x.shape[0]
    tm = rows if per_token else 512
    tiles_per_seq = None if per_token else seq_len // tm
    resident = functools.partial(_layer_spec, layer=layer, pipeline_mode=pl.Buffered(1))
    mod = functools.partial(_mod_spec, mods, layer, n_prompt, per_token)
    kern = functools.partial(_out_ffn_kernel, final_norm=final_norm, tiles_per_seq=tiles_per_seq)
    return pl.pallas_call(
        kern,
        grid=(rows // tm,),
        in_specs=[
            pl.BlockSpec((tm, D_MODEL), lambda i: (i, 0)),
            pl.BlockSpec((tm, MIX_HALF), lambda i: (i, 0)),
            pl.BlockSpec((tm, MIX_HALF), lambda i: (i, mix_b_col)),
            mod(MOD_GATE1),
            mod(MOD_SHIFT2),
            mod(MOD_SCALE2),
            mod(MOD_GATE2),
            resident((D_MODEL, D_MODEL)),
            _layer_spec((1, D_MODEL), layer),
            resident((D_MODEL, D_FF)),
            resident((D_MODEL, D_FF)),
            resident((D_FF, D_MODEL)),
            pl.BlockSpec((1, D_MODEL), lambda i: (0, 0)),
        ],
        out_specs=pl.BlockSpec((tm, D_MODEL), lambda i: (i, 0)),
        out_shape=jax.ShapeDtypeStruct((rows, D_MODEL), F32),
        scratch_shapes=[pltpu.VMEM((tm, D_MODEL), F32)],
        compiler_params=pltpu.CompilerParams(
            dimension_semantics=("arbitrary",), vmem_limit_bytes=VMEM_LIMIT),
        name="out_ffn",
    )(x, mix_a, mix_b, mods, mods, mods, mods, w_out, norm2, w_gate, w_up, w_down, norm_f)


N_AB = 2 * N_HEADS
REORDER_COLS = 1024
AB_BLOCK = COL_AB // REORDER_COLS


def _reorder_in_proj_kernel(wt_ref, o_ref):
    w = wt_ref[0]
    row = lax.broadcasted_iota(jnp.int32, w.shape, 0)
    w = jnp.where((pl.program_id(1) == AB_BLOCK) & (row >= N_AB), 0.0, w)
    o_ref[...] = w.T.astype(BF16)


def _reorder_in_proj(w):
    wt = jnp.swapaxes(w, 1, 2)

    def src_row(l, j):
        is_ab = j // AB_BLOCK
        is_b = j // (COL_B // REORDER_COLS) - 2 * is_ab
        start = j * REORDER_COLS + N_AB * is_b - (COL_AB - COL_B) * is_ab
        return (l, pl.multiple_of(start, 8), 0)

    return pl.pallas_call(
        _reorder_in_proj_kernel,
        grid=(DEPTH, pl.cdiv(IN_COLS, REORDER_COLS)),
        in_specs=[pl.BlockSpec((pl.Element(1), pl.Element(REORDER_COLS), pl.Element(D_MODEL)),
                               src_row)],
        out_specs=pl.BlockSpec((None, D_MODEL, REORDER_COLS), lambda l, j: (l, 0, j)),
        out_shape=jax.ShapeDtypeStruct((DEPTH, D_MODEL, IN_COLS), BF16),
        compiler_params=pltpu.CompilerParams(
            dimension_semantics=("arbitrary", "arbitrary"), vmem_limit_bytes=VMEM_LIMIT),
        name="reorder_in_proj",
    )(wt)


def _lane_rows(v):
    return jnp.pad(v.astype(F32), ((0, 0), (0, 128 - v.shape[1])))[:, None, :]


def kernel(x_prompt, x_sample, c_prompt, c_sample, state_delta, state_conv, state_hgrn, w_ada, b_ada,
           norm1, w_in, conv_w, a_log, dt_bias, norm_a, norm_b, lb_logits, w_out, norm2, w_gate, w_up,
           w_down, norm_f):
    bp, seq_len, _ = x_prompt.shape
    bs = x_sample.shape[0]

    mods = _ada_call(jnp.concatenate([c_sample, c_prompt], axis=0), w_ada, b_ada)

    xp = x_prompt.reshape(bp * seq_len, D_MODEL)
    xs = x_sample.reshape(bs, D_MODEL)
    zero_buf = jnp.zeros((bp, 8, CONV_DIM), F32)
    zero_state = jnp.zeros((bp, N_HEADS, HEAD_DIM, HEAD_DIM), F32)

    w_in_b = _reorder_in_proj(w_in)
    w_out_b = w_out.astype(BF16)
    w_gate_b = w_gate.astype(BF16)
    w_up_b = w_up.astype(BF16)
    w_down_b = w_down.astype(BF16)
    a_log_rows = _lane_rows(a_log)
    dt_bias_rows = _lane_rows(dt_bias)
    norm1_r = norm1[:, None, :]
    norm2_r = norm2[:, None, :]
    norm_a_r = norm_a[:, None, :]
    norm_b_r = norm_b[:, None, :]
    norm_f_r = norm_f[None, :]
    conv_state = state_conv.reshape(DEPTH, bs, (CONV_W - 1) * CONV_DIM)

    delta_p, conv_p, hgrn_p, conv_s = [], [], [], []
    decode_states = ()
    for l in range(DEPTH):
        last = l == DEPTH - 1
        dense = dict(layer=l, n_prompt=bp)

        proj, cv = _in_proj_call(xp, mods, norm1_r, w_in_b, zero_buf, conv_w, lb_logits,
                                 per_token=False, seq_len=seq_len, **dense)
        proj = proj.reshape(bp, seq_len, PROMPT_COLS)
        mix_a, s_a = _gdn_call(proj, zero_state, a_log_rows, dt_bias_rows, norm_a_r, layer=l)
        mix_b, s_b = _hgrn_call(proj, norm_b_r, zero_state, layer=l)
        mix_a = mix_a.reshape(bp * seq_len, MIX_HALF)
        mix_b = mix_b.reshape(bp * seq_len, MIX_HALF)
        xp = _out_ffn_call(xp, mix_a, mix_b, 0, mods, w_out_b, norm2_r, w_gate_b, w_up_b, w_down_b,
                           norm_f_r, per_token=False, seq_len=seq_len, final_norm=last, **dense)
        delta_p.append(s_a)
        conv_p.append(cv)
        hgrn_p.append(s_b)

        proj = _in_proj_call(xs, mods, norm1_r, w_in_b, None, None, None,
                             per_token=True, seq_len=1, **dense)
        mix, cv, *decode_states = _decode_call(
            proj, conv_state, state_delta, state_hgrn, decode_states,
            conv_w, a_log_rows, dt_bias_rows, norm_a_r, norm_b_r, lb_logits, layer=l)
        xs = _out_ffn_call(xs, mix, mix, 1, mods, w_out_b, norm2_r, w_gate_b, w_up_b, w_down_b,
                           norm_f_r, per_token=True, seq_len=1, final_norm=last, **dense)
        conv_s.append(cv.reshape(bs, CONV_W - 1, CONV_DIM))

    delta_s, hgrn_s = decode_states
    return (xp.reshape(bp, seq_len, D_MODEL), xs.reshape(bs, 1, D_MODEL),
            jnp.stack(delta_p), jnp.stack(conv_p), jnp.stack(hgrn_p),
            delta_s, jnp.stack(conv_s), hgrn_s)
```
